```python
import math
import jax, jax.numpy as jnp
from jax import lax
import numpy as np

D_MODEL = 1024
BATCH = 2
SEQ = 8192
DEPTH = 1

HEAD_DIM = 64
N_HEADS_A = 8
N_HEADS_B = 8
N_KV_B = 2
GQA_GROUP = N_HEADS_B // N_KV_B
D_A = N_HEADS_A * HEAD_DIM
D_B = N_HEADS_B * HEAD_DIM
D_KV_B = N_KV_B * HEAD_DIM
D_MIX = D_A + D_B
SPLITS = (D_A, 2 * D_A, 3 * D_A, 4 * D_A, 4 * D_A + D_B, 4 * D_A + D_B + D_KV_B, 4 * D_A + D_B + 2 * D_KV_B)
D_IN = 4 * D_A + 2 * D_B + 2 * D_KV_B
DIL_PATTERNS = ((128, 1), (512, 4), (2048, 16))
BLOCK = 128
GRID_W = 64
ROPE_THETA = 10000.0
N_BUCKETS = 32
MAX_DISTANCE = 1024
PLE_DIM = 256
EPS = 1e-6
NEG_INF = -1e30

kernel_name = "hybrid_dilated_gqa2drope_encoder_layer"


def rms_norm(x, g):
    xf = x.astype(jnp.float32)
    y = xf * lax.rsqrt(jnp.mean(xf * xf, axis=-1, keepdims=True) + EPS)
    return (y * g.astype(jnp.float32)).astype(x.dtype)


def t5_bucket(rel):
    nb = N_BUCKETS // 2
    max_exact = nb // 2
    ret = jnp.where(rel > 0, nb, 0)
    n = jnp.abs(rel)
    nf = jnp.maximum(n, max_exact).astype(jnp.float32)
    large = max_exact + (jnp.log(nf / max_exact) / math.log(MAX_DISTANCE / max_exact)
                         * (nb - max_exact)).astype(jnp.int32)
    large = jnp.minimum(large, nb - 1)
    return ret + jnp.where(n < max_exact, n, large)


def dilated_band_attention(q, k, v, bias_table, window, dilation):
    B, S, H, Dh = q.shape
    half = window // (2 * dilation)
    L = S // dilation
    nblk = -(-L // BLOCK)
    Lp = nblk * BLOCK
    span = BLOCK + 2 * half

    def to_sub(t):
        return t.reshape(B, L, dilation, H, Dh).transpose(0, 2, 1, 3, 4).reshape(B * dilation, L, H, Dh)

    qs, ks, vs = to_sub(q), to_sub(k), to_sub(v)
    qs = jnp.pad(qs, ((0, 0), (0, Lp - L), (0, 0), (0, 0)))
    kpad = ((0, 0), (half, Lp - L + half), (0, 0), (0, 0))
    ks = jnp.pad(ks, kpad)
    vs = jnp.pad(vs, kpad)
    key_idx = jnp.arange(nblk)[:, None] * BLOCK + jnp.arange(span)[None, :]
    kb = ks[:, key_idx]
    vb = vs[:, key_idx]
    qb = qs.reshape(B * dilation, nblk, BLOCK, H, Dh)
    s = jnp.einsum('gnqhd,gnkhd->gnhqk', qb, kb, preferred_element_type=jnp.float32)
    rel = jnp.arange(span)[None, :] - half - jnp.arange(BLOCK)[:, None]
    bias = bias_table[t5_bucket(rel * dilation)].astype(jnp.float32)
    s = s + jnp.transpose(bias, (2, 0, 1))[None, None]
    key_pos = key_idx - half
    valid = (jnp.abs(rel) <= half)[None] & ((key_pos >= 0) & (key_pos < L))[:, None, :]
    s = jnp.where(valid[None, :, None], s, NEG_INF)
    m = jnp.max(s, axis=-1)
    e = jnp.exp(s - m[..., None])
    den = jnp.sum(e, axis=-1)
    num = jnp.einsum('gnhqk,gnkhd->gnqhd', e, vb.astype(jnp.float32))

    def from_sub(t):
        tail = t.shape[3:]
        t = t.reshape(B, dilation, Lp, *tail)[:, :, :L]
        return jnp.swapaxes(t, 1, 2).reshape(B, S, *tail)

    return from_sub(num), from_sub(jnp.swapaxes(m, 2, 3)), from_sub(jnp.swapaxes(den, 2, 3))


def mixture_of_dilations(q, k, v, bias_table):
    parts = [dilated_band_attention(q, k, v, bias_table, w, d) for (w, d) in DIL_PATTERNS]
    m_all = parts[0][1]
    for _, m_i, _ in parts[1:]:
        m_all = jnp.maximum(m_all, m_i)
    num = 0.0
    den = 0.0
    for num_i, m_i, den_i in parts:
        c = jnp.exp(m_i - m_all)
        num = num + c[..., None] * num_i
        den = den + c * den_i
    return (num / den[..., None]).astype(q.dtype)


def axial_rope_tables(row_id, col_id):
    n = HEAD_DIM // 4
    inv = ROPE_THETA ** (-jnp.arange(n, dtype=jnp.float32) / n)
    ang = jnp.concatenate([row_id[:, None].astype(jnp.float32) * inv,
                           col_id[:, None].astype(jnp.float32) * inv], axis=-1)
    return jnp.cos(ang), jnp.sin(ang)


def apply_rope(x, cos, sin):
    xr = x.astype(jnp.float32).reshape(*x.shape[:-1], HEAD_DIM // 2, 2)
    x1, x2 = xr[..., 0], xr[..., 1]
    c = cos[None, :, None, :]
    s = sin[None, :, None, :]
    out = jnp.stack([x1 * c - x2 * s, x1 * s + x2 * c], axis=-1).reshape(x.shape)
    return out.astype(x.dtype)


def gqa_dense_blocks(q, k, v):
    B, S, _, Dh = q.shape
    nq = S // BLOCK
    qg = q.reshape(B, nq, BLOCK, N_KV_B, GQA_GROUP, Dh)
    qg = jnp.moveaxis(qg, 1, 0)

    def attend(qblk):
        s = jnp.einsum('bqkgd,bskd->bkgqs', qblk, k, preferred_element_type=jnp.float32)
        pr = jax.nn.softmax(s, axis=-1)
        return jnp.einsum('bkgqs,bskd->bqkgd', pr.astype(v.dtype), v)

    out = lax.map(attend, qg)
    return jnp.moveaxis(out, 0, 1).reshape(B, S, N_HEADS_B * Dh)


def setup_inputs(seed: int = 0) -> dict:
    key = jax.random.key(seed)
    ks = jax.random.split(key, 13)
    f32 = jnp.float32
    x = jax.random.normal(ks[0], (BATCH, SEQ, D_MODEL), f32)
    p = jax.random.normal(ks[1], (DEPTH, BATCH, SEQ, PLE_DIM), f32)
    ln_g = 1.0 + 0.02 * jax.random.normal(ks[2], (DEPTH, D_MODEL), f32)
    w_in = jax.random.normal(ks[3], (DEPTH, D_MODEL, D_IN), f32) * D_MODEL ** -0.5
    qn_a = 1.0 + 0.02 * jax.random.normal(ks[4], (DEPTH, HEAD_DIM), f32)
    kn_a = 1.0 + 0.02 * jax.random.normal(ks[5], (DEPTH, HEAD_DIM), f32)
    qn_b = 1.0 + 0.02 * jax.random.normal(ks[6], (DEPTH, HEAD_DIM), f32)
    kn_b = 1.0 + 0.02 * jax.random.normal(ks[7], (DEPTH, HEAD_DIM), f32)
    w_out = jax.random.normal(ks[8], (DEPTH, D_MIX, D_MODEL), f32) * D_MIX ** -0.5
    w_ple = jax.random.normal(ks[9], (DEPTH, PLE_DIM, D_MODEL), f32) * PLE_DIM ** -0.5
    w_pgate = jax.random.normal(ks[10], (DEPTH, D_MODEL, D_MODEL), f32) * D_MODEL ** -0.5
    rel_bias = 0.5 * jax.random.normal(ks[11], (N_BUCKETS, N_HEADS_A), f32)
    return {"x": x, "p": p, "ln_g": ln_g, "w_in": w_in, "qn_a": qn_a, "kn_a": kn_a,
            "qn_b": qn_b, "kn_b": kn_b, "w_out": w_out, "w_ple": w_ple,
            "w_pgate": w_pgate, "rel_bias": rel_bias}


def reference(x, p, ln_g, w_in, qn_a, kn_a, qn_b, kn_b, w_out, w_ple, w_pgate, rel_bias):
    B, S, _ = x.shape
    rows = S // GRID_W
    row_id = jnp.repeat(jnp.arange(rows, dtype=jnp.int32), GRID_W)
    col_id = jnp.arange(S, dtype=jnp.int32) % GRID_W
    cos, sin = axial_rope_tables(row_id, col_id)
    scale = HEAD_DIM ** -0.5

    for i in range(DEPTH):
        h = rms_norm(x, ln_g[i])
        proj = h @ w_in[i]
        qa, ka, va, ga, qb, kb, vb, gb = jnp.split(proj, list(SPLITS), axis=-1)

        qa = rms_norm(qa.reshape(B, S, N_HEADS_A, HEAD_DIM), qn_a[i]) * scale
        ka = rms_norm(ka.reshape(B, S, N_HEADS_A, HEAD_DIM), kn_a[i])
        va = va.reshape(B, S, N_HEADS_A, HEAD_DIM)
        ya = mixture_of_dilations(qa, ka, va, rel_bias).reshape(B, S, D_A)

        qb = apply_rope(rms_norm(qb.reshape(B, S, N_HEADS_B, HEAD_DIM), qn_b[i]), cos, sin) * scale
        kb = apply_rope(rms_norm(kb.reshape(B, S, N_KV_B, HEAD_DIM), kn_b[i]), cos, sin)
        vb = vb.reshape(B, S, N_KV_B, HEAD_DIM)
        yb = gqa_dense_blocks(qb, kb, vb)

        y = jnp.concatenate([ya * jax.nn.silu(ga), yb * jax.nn.silu(gb)], axis=-1)
        x = x + y @ w_out[i]

        x = x + (p[i] @ w_ple[i]) * jax.nn.sigmoid(x @ w_pgate[i])
    return x
```

```python
import functools
import math

import numpy as np
import jax
import jax.numpy as jnp
from jax import lax
from jax.experimental import pallas as pl
from jax.experimental.pallas import tpu as pltpu

HEAD_DIM = 64
N_HEADS_A = 8
N_HEADS_B = 8
N_KV_B = 2
GQA_GROUP = N_HEADS_B // N_KV_B
D_A = N_HEADS_A * HEAD_DIM
D_B = N_HEADS_B * HEAD_DIM
D_KV_B = N_KV_B * HEAD_DIM
DIL_PATTERNS = ((128, 1), (512, 4), (2048, 16))
BAND_BLOCK = 128
BAND_HALF = 64
BAND_SPAN = BAND_BLOCK + 2 * BAND_HALF
GRID_W = 64
ROPE_THETA = 10000.0
N_BUCKETS = 32
MAX_DISTANCE = 1024
EPS = 1e-6
NEG_INF = -1e30
ONES_ROWS = 16

ROW_TILE = 512
Q_TILE = 512
VMEM_LIMIT = 56 * 1024 * 1024

F32 = jnp.float32
BF16 = jnp.bfloat16


def _nt_dot(a, b):
    return lax.dot_general(a, b, (((1,), (1,)), ((), ())), preferred_element_type=F32)


def _dot(a, b):
    return jnp.dot(a, b, preferred_element_type=F32)


def _head_rms(t, gain):
    ms = jnp.mean(t * t, axis=1, keepdims=True)
    return t * lax.rsqrt(ms + EPS) * gain[None]


def _rope_halves(t, cos, sin):
    half = HEAD_DIM // 2
    x1 = t[:, :half, :]
    x2 = t[:, half:, :]
    c = cos[None]
    s = sin[None]
    return jnp.concatenate([x1 * c - x2 * s, x1 * s + x2 * c], axis=1)


def _proj_kernel(x_ref, g_ref, wtok_ref, wfeat_ref, gqa_ref, gka_ref, gqb_ref, gkb_ref, cos_ref, sin_ref,
                 va_ref, ga_ref, gb_ref, qa_ref, ka_ref, qbt_ref, kb_ref, vbt_ref):
    tm = x_ref.shape[1]
    scale = HEAD_DIM ** -0.5
    x = x_ref[0]
    ms = jnp.mean(x * x, axis=-1, keepdims=True)
    h = (x * lax.rsqrt(ms + EPS) * g_ref[...]).astype(BF16)

    tok = _dot(h, wtok_ref[...])
    va_ref[0] = tok[:, :D_A].astype(BF16)
    ga_ref[0] = tok[:, D_A:2 * D_A].astype(BF16)
    gb_ref[0] = tok[:, 2 * D_A:].astype(BF16)

    feat = _nt_dot(wfeat_ref[...], h)
    qa = _head_rms(feat[:D_A].reshape(N_HEADS_A, HEAD_DIM, tm), gqa_ref[...]) * scale
    qa_ref[0] = qa.reshape(D_A, tm).T.astype(BF16)
    ka = _head_rms(feat[D_A:2 * D_A].reshape(N_HEADS_A, HEAD_DIM, tm), gka_ref[...])
    ka_ref[0] = ka.reshape(D_A, tm).T.astype(BF16)

    o = 2 * D_A
    cos = cos_ref[...]
    sin = sin_ref[...]
    qb = _head_rms(feat[o:o + D_B].reshape(N_HEADS_B, HEAD_DIM, tm), gqb_ref[...])
    qbt_ref[0] = (_rope_halves(qb, cos, sin) * scale).astype(BF16)
    o += D_B
    kb = _head_rms(feat[o:o + D_KV_B].reshape(N_KV_B, HEAD_DIM, tm), gkb_ref[...])
    kb = _rope_halves(kb, cos, sin)
    zeros = jnp.zeros((HEAD_DIM, tm), F32)
    for kv in range(N_KV_B):
        kb_ref[0, kv] = jnp.concatenate([kb[kv], zeros], axis=0).T.astype(BF16)
    o += D_KV_B
    vb = feat[o:o + D_KV_B].reshape(N_KV_B, HEAD_DIM, tm)
    ones = jnp.ones((N_KV_B, ONES_ROWS, tm), F32)
    vbt_ref[0, :, 0] = jnp.concatenate([vb, ones], axis=1).astype(BF16)


def _proj_call(x, ln_g, wtok, wfeat, gqa, gka, gqb, gkb, cos_t, sin_t):
    B, S, D = x.shape
    tm = ROW_TILE
    nt = S // tm
    full = lambda shape: pl.BlockSpec(shape, lambda b, i: (0,) * len(shape))
    tok_spec = pl.BlockSpec((1, tm, D_A), lambda b, i: (b, i, 0))
    out_shape = (
        jax.ShapeDtypeStruct((B, S, D_A), BF16),
        jax.ShapeDtypeStruct((B, S, D_A), BF16),
        jax.ShapeDtypeStruct((B, S, D_B), BF16),
        jax.ShapeDtypeStruct((B, S, D_A), BF16),
        jax.ShapeDtypeStruct((B, S, D_A), BF16),
        jax.ShapeDtypeStruct((B, N_HEADS_B, HEAD_DIM, S), BF16),
        jax.ShapeDtypeStruct((B, N_KV_B, S, 2 * HEAD_DIM), BF16),
        jax.ShapeDtypeStruct((B, N_KV_B, nt, HEAD_DIM + ONES_ROWS, tm), BF16),
    )
    out_specs = (
        tok_spec, tok_spec, tok_spec, tok_spec, tok_spec,
        pl.BlockSpec((1, N_HEADS_B, HEAD_DIM, tm), lambda b, i: (b, 0, 0, i)),
        pl.BlockSpec((1, N_KV_B, tm, 2 * HEAD_DIM), lambda b, i: (b, 0, i, 0)),
        pl.BlockSpec((1, N_KV_B, 1, HEAD_DIM + ONES_ROWS, tm), lambda b, i: (b, 0, i, 0, 0)),
    )
    in_specs = [
        pl.BlockSpec((1, tm, D), lambda b, i: (b, i, 0)),
        full((1, D)),
        full(wtok.shape),
        full(wfeat.shape),
        full((HEAD_DIM, tm)), full((HEAD_DIM, tm)), full((HEAD_DIM, tm)), full((HEAD_DIM, tm)),
        pl.BlockSpec((HEAD_DIM // 2, tm), lambda b, i: (0, i)),
        pl.BlockSpec((HEAD_DIM // 2, tm), lambda b, i: (0, i)),
    ]
    return pl.pallas_call(
        _proj_kernel,
        grid=(B, nt),
        in_specs=in_specs,
        out_specs=out_specs,
        out_shape=out_shape,
        compiler_params=pltpu.CompilerParams(
            dimension_semantics=("arbitrary", "arbitrary"), vmem_limit_bytes=VMEM_LIMIT),
        name="proj",
    )(x, ln_g, wtok, wfeat, gqa, gka, gqb, gkb, cos_t, sin_t)


def _gqa_kernel(qt_ref, k_ref, vt_ref, o_ref, acc_ref):
    tq = qt_ref.shape[3]
    n_chunks = vt_ref.shape[2]
    tk = vt_ref.shape[4]
    q = qt_ref[0, 0]
    qext = jnp.concatenate([q, jnp.zeros_like(q)], axis=0)
    acc_ref[...] = jnp.zeros_like(acc_ref)

    def chunk(j, carry):
        k = k_ref[0, 0, pl.ds(pl.multiple_of(j * tk, tk), tk), :]
        s = _dot(k, qext)
        p = jnp.exp(s).astype(BF16)
        acc_ref[...] += _dot(vt_ref[0, 0, j], p)
        return carry

    lax.fori_loop(0, n_chunks, chunk, 0)
    acc = acc_ref[...]
    o_ref[0, 0] = acc[:HEAD_DIM] / acc[HEAD_DIM:HEAD_DIM + 1]


def _gqa_call(qbt, kb, vbt):
    B, H, _, S = qbt.shape
    tq = Q_TILE
    nt, rows, tk = vbt.shape[2], vbt.shape[3], vbt.shape[4]
    return pl.pallas_call(
        _gqa_kernel,
        grid=(B, H, S // tq),
        in_specs=[
            pl.BlockSpec((1, 1, HEAD_DIM, tq), lambda b, h, i: (b, h, 0, i)),
            pl.BlockSpec((1, 1, S, 2 * HEAD_DIM), lambda b, h, i: (b, h // GQA_GROUP, 0, 0)),
            pl.BlockSpec((1, 1, nt, rows, tk), lambda b, h, i: (b, h // GQA_GROUP, 0, 0, 0)),
        ],
        out_specs=pl.BlockSpec((1, 1, HEAD_DIM, tq), lambda b, h, i: (b, h, 0, i)),
        out_shape=jax.ShapeDtypeStruct((B, H, HEAD_DIM, S), F32),
        scratch_shapes=[pltpu.VMEM((rows, tq), F32)],
        compiler_params=pltpu.CompilerParams(
            dimension_semantics=("arbitrary", "arbitrary", "arbitrary"), vmem_limit_bytes=VMEM_LIMIT),
        name="gqa",
    )(qbt, kb, vbt)


def _band_bucket_index():
    nb = N_BUCKETS // 2
    max_exact = nb // 2
    rel = np.arange(BAND_SPAN)[None, :] - BAND_HALF - np.arange(BAND_BLOCK)[:, None]
    out = []
    for _, dilation in DIL_PATTERNS:
        dist = rel * dilation
        ret = np.where(dist > 0, nb, 0)
        n = np.abs(dist)
        nf = np.maximum(n, max_exact).astype(np.float32)
        large = max_exact + (np.log(nf / np.float32(max_exact)) / np.float32(math.log(MAX_DISTANCE / max_exact))
                             * np.float32(nb - max_exact)).astype(np.int32)
        large = np.minimum(large, nb - 1)
        bucket = ret + np.where(n < max_exact, n, large)
        out.append(np.where(np.abs(rel) <= BAND_HALF, bucket, -1))
    return np.stack(out).astype(np.int32)


def _bias_kernel(table_ref, idx_ref, o_ref):
    h = pl.program_id(1)
    idx = idx_ref[0]
    acc = jnp.full(idx.shape, NEG_INF, F32)
    for b in range(N_BUCKETS):
        acc = jnp.where(idx == b, table_ref[b, h], acc)
    o_ref[0, 0] = acc


def _bias_call(rel_bias):
    idx = jnp.asarray(_band_bucket_index())
    n_pat = len(DIL_PATTERNS)
    return pl.pallas_call(
        _bias_kernel,
        grid=(n_pat, N_HEADS_A),
        in_specs=[
            pl.BlockSpec(memory_space=pltpu.SMEM),
            pl.BlockSpec((1, BAND_BLOCK, BAND_SPAN), lambda p, h: (p, 0, 0)),
        ],
        out_specs=pl.BlockSpec((1, 1, BAND_BLOCK, BAND_SPAN), lambda p, h: (p, h, 0, 0)),
        out_shape=jax.ShapeDtypeStruct((n_pat, N_HEADS_A, BAND_BLOCK, BAND_SPAN), F32),
        name="bias_bands",
    )(rel_bias, idx)


def _band_kernel(q_ref, kp_ref, kc_ref, kn_ref, vp_ref, vc_ref, vn_ref, bias_ref, num_ref, den_ref):
    n = pl.program_id(1)
    last = pl.num_programs(1) - 1
    col = lax.broadcasted_iota(jnp.int32, (1, BAND_SPAN), 1)
    off_seq = ((n == 0) & (col < BAND_HALF)) | ((n == last) & (col >= BAND_BLOCK + BAND_HALF))
    edge = jnp.where(off_seq, NEG_INF, 0.0).astype(F32)
    lane = lax.broadcasted_iota(jnp.int32, (BAND_BLOCK, 2 * HEAD_DIM), 1)
    first_half = lane < HEAD_DIM
    ones = jnp.ones((BAND_SPAN, 2 * HEAD_DIM), BF16)
    for pair in range(N_HEADS_A // 2):
        cols = slice(pair * 2 * HEAD_DIM, (pair + 1) * 2 * HEAD_DIM)
        q = q_ref[0, :, cols]
        kband = jnp.concatenate([kp_ref[0, BAND_HALF:, cols], kc_ref[0, :, cols], kn_ref[0, :BAND_HALF, cols]], axis=0)
        vband = jnp.concatenate([vp_ref[0, BAND_HALF:, cols], vc_ref[0, :, cols], vn_ref[0, :BAND_HALF, cols]], axis=0)
        vext = jnp.concatenate([vband, ones], axis=1)
        outs = []
        for half in range(2):
            keep = first_half if half == 0 else jnp.logical_not(first_half)
            qh = jnp.where(keep, q, jnp.zeros_like(q))
            s = _nt_dot(qh, kband) + bias_ref[2 * pair + half] + edge
            p = jnp.exp(s).astype(BF16)
            outs.append(_dot(p, vext))
        num_ref[0, :, cols] = jnp.where(first_half, outs[0][:, :2 * HEAD_DIM], outs[1][:, :2 * HEAD_DIM])
        den_ref[0, :, cols] = jnp.where(first_half, outs[0][:, 2 * HEAD_DIM:], outs[1][:, 2 * HEAD_DIM:])


def _band_call(q, k, v, bias):
    G, L, _ = q.shape
    nb = L // BAND_BLOCK
    blk = (1, BAND_BLOCK, D_A)
    cur = pl.BlockSpec(blk, lambda g, n: (g, n, 0))
    prev = pl.BlockSpec(blk, lambda g, n: (g, jnp.maximum(n - 1, 0), 0))
    nxt = pl.BlockSpec(blk, lambda g, n: (g, jnp.minimum(n + 1, nb - 1), 0))
    return pl.pallas_call(
        _band_kernel,
        grid=(G, nb),
        in_specs=[cur, prev, cur, nxt, prev, cur, nxt,
                  pl.BlockSpec((N_HEADS_A, BAND_BLOCK, BAND_SPAN), lambda g, n: (0, 0, 0))],
        out_specs=(cur, cur),
        out_shape=(jax.ShapeDtypeStruct((G, L, D_A), F32), jax.ShapeDtypeStruct((G, L, D_A), F32)),
        compiler_params=pltpu.CompilerParams(
            dimension_semantics=("arbitrary", "arbitrary"), vmem_limit_bytes=VMEM_LIMIT),
        name="band",
    )(q, k, k, k, v, v, v, bias)


def _to_sub(t, dilation):
    B, S, C = t.shape
    if dilation == 1:
        return t
    L = S // dilation
    return t.reshape(B, L, dilation, C).transpose(0, 2, 1, 3).reshape(B * dilation, L, C)


def _from_sub(t, dilation, B):
    if dilation == 1:
        return t
    G, L, C = t.shape
    return t.reshape(B, dilation, L, C).transpose(0, 2, 1, 3).reshape(B, L * dilation, C)


def _silu(g):
    return g * jax.nn.sigmoid(g)


def _out_kernel(x_ref, p_ref, n0_ref, n1_ref, n2_ref, d0_ref, d1_ref, d2_ref, ga_ref, gb_ref, ybt_ref,
                wo_ref, wple_ref, wgate_ref, o_ref):
    tm = x_ref.shape[1]
    ya = (n0_ref[0] + n1_ref[0] + n2_ref[0]) / (d0_ref[0] + d1_ref[0] + d2_ref[0])
    ya = ya * _silu(ga_ref[0].astype(F32))
    yb = ybt_ref[0].reshape(D_B, tm).T * _silu(gb_ref[0].astype(F32))
    x1 = x_ref[0] + _dot(ya.astype(BF16), wo_ref[:D_A, :]) + _dot(yb.astype(BF16), wo_ref[D_A:, :])
    gate = jax.nn.sigmoid(_dot(x1.astype(BF16), wgate_ref[...]))
    ple = _dot(p_ref[0].astype(BF16), wple_ref[...])
    o_ref[0] = x1 + ple * gate


def _out_call(x, p, nums, dens, ga, gb, ybt, wo, wple, wgate):
    B, S, D = x.shape
    tm = ROW_TILE
    full = lambda shape: pl.BlockSpec(shape, lambda b, i: (0,) * len(shape))
    tok = lambda c: pl.BlockSpec((1, tm, c), lambda b, i: (b, i, 0))
    return pl.pallas_call(
        _out_kernel,
        grid=(B, S // tm),
        in_specs=[tok(D), tok(p.shape[-1])] + [tok(D_A)] * 6 + [tok(D_A), tok(D_B),
                  pl.BlockSpec((1, N_HEADS_B, HEAD_DIM, tm), lambda b, i: (b, 0, 0, i)),
                  full(wo.shape), full(wple.shape), full(wgate.shape)],
        out_specs=tok(D),
        out_shape=jax.ShapeDtypeStruct((B, S, D), F32),
        compiler_params=pltpu.CompilerParams(
            dimension_semantics=("arbitrary", "arbitrary"), vmem_limit_bytes=VMEM_LIMIT),
        name="out_proj",
    )(x, p, *nums, *dens, ga, gb, ybt, wo, wple, wgate)


def _rope_tables(S):
    n = HEAD_DIM // 4
    inv = ROPE_THETA ** (-jnp.arange(n, dtype=F32) / n)
    pos = jnp.arange(S, dtype=jnp.int32)
    row_id = (pos // GRID_W).astype(F32)
    col_id = (pos % GRID_W).astype(F32)
    ang = jnp.concatenate([inv[:, None] * row_id[None, :], inv[:, None] * col_id[None, :]], axis=0)
    return jnp.cos(ang), jnp.sin(ang)


def kernel(x, p, ln_g, w_in, qn_a, kn_a, qn_b, kn_b, w_out, w_ple, w_pgate, rel_bias):
    B, S, D = x.shape
    depth = w_in.shape[0]
    cos_t, sin_t = _rope_tables(S)
    bias = _bias_call(rel_bias)
    perm = np.concatenate([np.arange(0, HEAD_DIM, 2), np.arange(1, HEAD_DIM, 2)])
    bcast = lambda g: jnp.broadcast_to(g.astype(F32)[:, None], (HEAD_DIM, ROW_TILE))
    for i in range(depth):
        w = w_in[i]
        c = 0
        wqa = w[:, c:c + D_A]; c += D_A
        wka = w[:, c:c + D_A]; c += D_A
        wva = w[:, c:c + D_A]; c += D_A
        wga = w[:, c:c + D_A]; c += D_A
        wqb = w[:, c:c + D_B]; c += D_B
        wkb = w[:, c:c + D_KV_B]; c += D_KV_B
        wvb = w[:, c:c + D_KV_B]; c += D_KV_B
        wgb = w[:, c:c + D_B]
        wqb = wqb.reshape(D, N_HEADS_B, HEAD_DIM)[:, :, perm].reshape(D, D_B)
        wkb = wkb.reshape(D, N_KV_B, HEAD_DIM)[:, :, perm].reshape(D, D_KV_B)
        wtok = jnp.concatenate([wva, wga, wgb], axis=1).astype(BF16)
        wfeat = jnp.concatenate([wqa, wka, wqb, wkb, wvb], axis=1).T.astype(BF16)

        va, ga, gb, qa, ka, qbt, kb, vbt = _proj_call(
            x, ln_g[i][None, :], wtok, wfeat, bcast(qn_a[i]), bcast(kn_a[i]),
            bcast(qn_b[i][perm]), bcast(kn_b[i][perm]), cos_t, sin_t)

        ybt = _gqa_call(qbt, kb, vbt)

        nums, dens = [], []
        for pat, (_, dilation) in enumerate(DIL_PATTERNS):
            num, den = _band_call(_to_sub(qa, dilation), _to_sub(ka, dilation), _to_sub(va, dilation), bias[pat])
            nums.append(_from_sub(num, dilation, B))
            dens.append(_from_sub(den, dilation, B))

        x = _out_call(x, p[i], nums, dens, ga, gb, ybt, w_out[i].astype(BF16), w_ple[i].astype(BF16),
                      w_pgate[i].astype(BF16))
    return x
```

```python
import functools
import math

import numpy as np
import jax
import jax.numpy as jnp
from jax import lax
from jax.experimental import pallas as pl
from jax.experimental.pallas import tpu as pltpu

HEAD_DIM = 64
N_HEADS_A = 8
N_HEADS_B = 8
N_KV_B = 2
GQA_GROUP = N_HEADS_B // N_KV_B
D_A = N_HEADS_A * HEAD_DIM
D_B = N_HEADS_B * HEAD_DIM
D_KV_B = N_KV_B * HEAD_DIM
DIL_PATTERNS = ((128, 1), (512, 4), (2048, 16))
BAND_BLOCK = 128
BAND_HALF = 64
BAND_SPAN = BAND_BLOCK + 2 * BAND_HALF
GRID_W = 64
ROPE_THETA = 10000.0
N_BUCKETS = 32
MAX_DISTANCE = 1024
EPS = 1e-6
NEG_INF = -1e30
ONES_ROWS = 16

ROW_TILE = 512
Q_TILE = 512
KEY_STEP = 256
LOOKAHEAD = 2
VMEM_LIMIT = 56 * 1024 * 1024

F32 = jnp.float32
BF16 = jnp.bfloat16


def _nt_dot(a, b):
    return lax.dot_general(a, b, (((1,), (1,)), ((), ())), preferred_element_type=F32)


def _dot(a, b):
    return jnp.dot(a, b, preferred_element_type=F32)


def _head_rms(t, gain):
    ms = jnp.mean(t * t, axis=1, keepdims=True)
    return t * lax.rsqrt(ms + EPS) * gain[None]


def _rope_halves(t, cos, sin):
    half = HEAD_DIM // 2
    x1 = t[:, :half, :]
    x2 = t[:, half:, :]
    c = cos[None]
    s = sin[None]
    return jnp.concatenate([x1 * c - x2 * s, x1 * s + x2 * c], axis=1)


def _proj_kernel(x_ref, g_ref, wtok_ref, wfeat_ref, gqa_ref, gka_ref, gqb_ref, gkb_ref, cos_ref, sin_ref,
                 va_ref, ga_ref, gb_ref, qa_ref, ka_ref, qbt_ref, kb_ref, vbt_ref):
    tm = x_ref.shape[1]
    scale = HEAD_DIM ** -0.5
    x = x_ref[0]
    ms = jnp.mean(x * x, axis=-1, keepdims=True)
    h = (x * lax.rsqrt(ms + EPS) * g_ref[...]).astype(BF16)

    tok = _dot(h, wtok_ref[...])
    va_ref[0] = tok[:, :D_A].astype(BF16)
    ga_ref[0] = tok[:, D_A:2 * D_A].astype(BF16)
    gb_ref[0] = tok[:, 2 * D_A:].astype(BF16)

    feat = _nt_dot(wfeat_ref[...], h)
    qa = _head_rms(feat[:D_A].reshape(N_HEADS_A, HEAD_DIM, tm), gqa_ref[...]) * scale
    qa_ref[0] = qa.reshape(D_A, tm).T.astype(BF16)
    ka = _head_rms(feat[D_A:2 * D_A].reshape(N_HEADS_A, HEAD_DIM, tm), gka_ref[...])
    ka_ref[0] = ka.reshape(D_A, tm).T.astype(BF16)

    o = 2 * D_A
    cos = cos_ref[...]
    sin = sin_ref[...]
    qb = _head_rms(feat[o:o + D_B].reshape(N_HEADS_B, HEAD_DIM, tm), gqb_ref[...])
    qbt_ref[0] = (_rope_halves(qb, cos, sin) * scale).astype(BF16)
    o += D_B
    kb = _head_rms(feat[o:o + D_KV_B].reshape(N_KV_B, HEAD_DIM, tm), gkb_ref[...])
    kb = _rope_halves(kb, cos, sin)
    zeros = jnp.zeros((HEAD_DIM, tm), F32)
    for kv in range(N_KV_B):
        kb_ref[0, kv] = jnp.concatenate([kb[kv], zeros], axis=0).T.astype(BF16)
    o += D_KV_B
    vb = feat[o:o + D_KV_B].reshape(N_KV_B, HEAD_DIM, tm)
    ones = jnp.ones((N_KV_B, ONES_ROWS, tm), F32)
    vbt_ref[0, :, 0] = jnp.concatenate([vb, ones], axis=1).astype(BF16)


def _proj_call(x, ln_g, wtok, wfeat, gqa, gka, gqb, gkb, cos_t, sin_t):
    B, S, D = x.shape
    tm = ROW_TILE
    nt = S // tm
    full = lambda shape: pl.BlockSpec(shape, lambda b, i: (0,) * len(shape))
    tok_spec = pl.BlockSpec((1, tm, D_A), lambda b, i: (b, i, 0))
    out_shape = (
        jax.ShapeDtypeStruct((B, S, D_A), BF16),
        jax.ShapeDtypeStruct((B, S, D_A), BF16),
        jax.ShapeDtypeStruct((B, S, D_B), BF16),
        jax.ShapeDtypeStruct((B, S, D_A), BF16),
        jax.ShapeDtypeStruct((B, S, D_A), BF16),
        jax.ShapeDtypeStruct((B, N_HEADS_B, HEAD_DIM, S), BF16),
        jax.ShapeDtypeStruct((B, N_KV_B, S, 2 * HEAD_DIM), BF16),
        jax.ShapeDtypeStruct((B, N_KV_B, nt, HEAD_DIM + ONES_ROWS, tm), BF16),
    )
    out_specs = (
        tok_spec, tok_spec, tok_spec, tok_spec, tok_spec,
        pl.BlockSpec((1, N_HEADS_B, HEAD_DIM, tm), lambda b, i: (b, 0, 0, i)),
        pl.BlockSpec((1, N_KV_B, tm, 2 * HEAD_DIM), lambda b, i: (b, 0, i, 0)),
        pl.BlockSpec((1, N_KV_B, 1, HEAD_DIM + ONES_ROWS, tm), lambda b, i: (b, 0, i, 0, 0)),
    )
    in_specs = [
        pl.BlockSpec((1, tm, D), lambda b, i: (b, i, 0)),
        full((1, D)),
        full(wtok.shape),
        full(wfeat.shape),
        full((HEAD_DIM, tm)), full((HEAD_DIM, tm)), full((HEAD_DIM, tm)), full((HEAD_DIM, tm)),
        pl.BlockSpec((HEAD_DIM // 2, tm), lambda b, i: (0, i)),
        pl.BlockSpec((HEAD_DIM // 2, tm), lambda b, i: (0, i)),
    ]
    return pl.pallas_call(
        _proj_kernel,
        grid=(B, nt),
        in_specs=in_specs,
        out_specs=out_specs,
        out_shape=out_shape,
        compiler_params=pltpu.CompilerParams(
            dimension_semantics=("arbitrary", "arbitrary"), vmem_limit_bytes=VMEM_LIMIT),
        name="proj",
    )(x, ln_g, wtok, wfeat, gqa, gka, gqb, gkb, cos_t, sin_t)


def _gqa_kernel(qt_ref, k_ref, vt_ref, o_ref):
    tq = qt_ref.shape[3]
    n_chunks = vt_ref.shape[2]
    tk = vt_ref.shape[4]
    q = qt_ref[0, 0]
    qext = jnp.concatenate([q, jnp.zeros_like(q)], axis=0)
    acc = jnp.zeros((vt_ref.shape[3], tq), F32)
    per_chunk = tk // KEY_STEP
    n_steps = n_chunks * per_chunk

    def scores(t):
        return _dot(k_ref[0, 0, t * KEY_STEP:(t + 1) * KEY_STEP, :], qext)

    pending = [scores(t) for t in range(min(LOOKAHEAD, n_steps))]
    for t in range(n_steps):
        if t + LOOKAHEAD < n_steps:
            pending.append(scores(t + LOOKAHEAD))
        p = jnp.exp(pending[t]).astype(BF16)
        lo = (t % per_chunk) * KEY_STEP
        acc = acc + _dot(vt_ref[0, 0, t // per_chunk, :, lo:lo + KEY_STEP], p)
    o_ref[0, 0] = acc[:HEAD_DIM] / acc[HEAD_DIM:HEAD_DIM + 1]


def _gqa_call(qbt, kb, vbt):
    B, H, _, S = qbt.shape
    tq = Q_TILE
    nt, rows, tk = vbt.shape[2], vbt.shape[3], vbt.shape[4]
    return pl.pallas_call(
        _gqa_kernel,
        grid=(B, H, S // tq),
        in_specs=[
            pl.BlockSpec((1, 1, HEAD_DIM, tq), lambda b, h, i: (b, h, 0, i)),
            pl.BlockSpec((1, 1, S, 2 * HEAD_DIM), lambda b, h, i: (b, h // GQA_GROUP, 0, 0)),
            pl.BlockSpec((1, 1, nt, rows, tk), lambda b, h, i: (b, h // GQA_GROUP, 0, 0, 0)),
        ],
        out_specs=pl.BlockSpec((1, 1, HEAD_DIM, tq), lambda b, h, i: (b, h, 0, i)),
        out_shape=jax.ShapeDtypeStruct((B, H, HEAD_DIM, S), F32),
        compiler_params=pltpu.CompilerParams(
            dimension_semantics=("arbitrary", "arbitrary", "arbitrary"), vmem_limit_bytes=VMEM_LIMIT),
        name="gqa",
    )(qbt, kb, vbt)


def _band_bucket_index():
    nb = N_BUCKETS // 2
    max_exact = nb // 2
    rel = np.arange(BAND_SPAN)[None, :] - BAND_HALF - np.arange(BAND_BLOCK)[:, None]
    out = []
    for _, dilation in DIL_PATTERNS:
        dist = rel * dilation
        ret = np.where(dist > 0, nb, 0)
        n = np.abs(dist)
        nf = np.maximum(n, max_exact).astype(np.float32)
        large = max_exact + (np.log(nf / np.float32(max_exact)) / np.float32(math.log(MAX_DISTANCE / max_exact))
                             * np.float32(nb - max_exact)).astype(np.int32)
        large = np.minimum(large, nb - 1)
        bucket = ret + np.where(n < max_exact, n, large)
        out.append(np.where(np.abs(rel) <= BAND_HALF, bucket, -1))
    return np.stack(out).astype(np.int32)


def _bias_kernel(table_ref, idx_ref, o_ref):
    h = pl.program_id(1)
    idx = idx_ref[0]
    acc = jnp.full(idx.shape, NEG_INF, F32)
    for b in range(N_BUCKETS):
        acc = jnp.where(idx == b, table_ref[b, h], acc)
    o_ref[0, 0] = acc


def _bias_call(rel_bias):
    idx = jnp.asarray(_band_bucket_index())
    n_pat = len(DIL_PATTERNS)
    return pl.pallas_call(
        _bias_kernel,
        grid=(n_pat, N_HEADS_A),
        in_specs=[
            pl.BlockSpec(memory_space=pltpu.SMEM),
            pl.BlockSpec((1, BAND_BLOCK, BAND_SPAN), lambda p, h: (p, 0, 0)),
        ],
        out_specs=pl.BlockSpec((1, 1, BAND_BLOCK, BAND_SPAN), lambda p, h: (p, h, 0, 0)),
        out_shape=jax.ShapeDtypeStruct((n_pat, N_HEADS_A, BAND_BLOCK, BAND_SPAN), F32),
        name="bias_bands",
    )(rel_bias, idx)


def _band_kernel(q_ref, kp_ref, kc_ref, kn_ref, vp_ref, vc_ref, vn_ref, bias_ref, num_ref, den_ref):
    n = pl.program_id(1)
    last = pl.num_programs(1) - 1
    col = lax.broadcasted_iota(jnp.int32, (1, BAND_SPAN), 1)
    off_seq = ((n == 0) & (col < BAND_HALF)) | ((n == last) & (col >= BAND_BLOCK + BAND_HALF))
    edge = jnp.where(off_seq, NEG_INF, 0.0).astype(F32)
    lane = lax.broadcasted_iota(jnp.int32, (BAND_BLOCK, 2 * HEAD_DIM), 1)
    first_half = lane < HEAD_DIM
    ones = jnp.ones((BAND_SPAN, 2 * HEAD_DIM), BF16)
    for pair in range(N_HEADS_A // 2):
        cols = slice(pair * 2 * HEAD_DIM, (pair + 1) * 2 * HEAD_DIM)
        q = q_ref[0, :, cols]
        kband = jnp.concatenate([kp_ref[0, BAND_HALF:, cols], kc_ref[0, :, cols], kn_ref[0, :BAND_HALF, cols]], axis=0)
        vband = jnp.concatenate([vp_ref[0, BAND_HALF:, cols], vc_ref[0, :, cols], vn_ref[0, :BAND_HALF, cols]], axis=0)
        vext = jnp.concatenate([vband, ones], axis=1)
        outs = []
        for half in range(2):
            keep = first_half if half == 0 else jnp.logical_not(first_half)
            qh = jnp.where(keep, q, jnp.zeros_like(q))
            s = _nt_dot(qh, kband) + bias_ref[2 * pair + half] + edge
            p = jnp.exp(s).astype(BF16)
            outs.append(_dot(p, vext))
        num_ref[0, :, cols] = jnp.where(first_half, outs[0][:, :2 * HEAD_DIM], outs[1][:, :2 * HEAD_DIM])
        den_ref[0, :, cols] = jnp.where(first_half, outs[0][:, 2 * HEAD_DIM:], outs[1][:, 2 * HEAD_DIM:])


def _band_call(q, k, v, bias):
    G, L, _ = q.shape
    nb = L // BAND_BLOCK
    blk = (1, BAND_BLOCK, D_A)
    cur = pl.BlockSpec(blk, lambda g, n: (g, n, 0))
    prev = pl.BlockSpec(blk, lambda g, n: (g, jnp.maximum(n - 1, 0), 0))
    nxt = pl.BlockSpec(blk, lambda g, n: (g, jnp.minimum(n + 1, nb - 1), 0))
    return pl.pallas_call(
        _band_kernel,
        grid=(G, nb),
        in_specs=[cur, prev, cur, nxt, prev, cur, nxt,
                  pl.BlockSpec((N_HEADS_A, BAND_BLOCK, BAND_SPAN), lambda g, n: (0, 0, 0))],
        out_specs=(cur, cur),
        out_shape=(jax.ShapeDtypeStruct((G, L, D_A), F32), jax.ShapeDtypeStruct((G, L, D_A), F32)),
        compiler_params=pltpu.CompilerParams(
            dimension_semantics=("arbitrary", "arbitrary"), vmem_limit_bytes=VMEM_LIMIT),
        name="band",
    )(q, k, k, k, v, v, v, bias)


def _to_sub(t, dilation):
    B, S, C = t.shape
    if dilation == 1:
        return t
    L = S // dilation
    return t.reshape(B, L, dilation, C).transpose(0, 2, 1, 3).reshape(B * dilation, L, C)


def _from_sub(t, dilation, B):
    if dilation == 1:
        return t
    G, L, C = t.shape
    return t.reshape(B, dilation, L, C).transpose(0, 2, 1, 3).reshape(B, L * dilation, C)


def _silu(g):
    return g * jax.nn.sigmoid(g)


def _out_kernel(x_ref, p_ref, n0_ref, n1_ref, n2_ref, d0_ref, d1_ref, d2_ref, ga_ref, gb_ref, ybt_ref,
                wo_ref, wple_ref, wgate_ref, o_ref):
    tm = x_ref.shape[1]
    ya = (n0_ref[0] + n1_ref[0] + n2_ref[0]) / (d0_ref[0] + d1_ref[0] + d2_ref[0])
    ya = ya * _silu(ga_ref[0].astype(F32))
    yb = ybt_ref[0].reshape(D_B, tm).T * _silu(gb_ref[0].astype(F32))
    x1 = x_ref[0] + _dot(ya.astype(BF16), wo_ref[:D_A, :]) + _dot(yb.astype(BF16), wo_ref[D_A:, :])
    gate = jax.nn.sigmoid(_dot(x1.astype(BF16), wgate_ref[...]))
    ple = _dot(p_ref[0].astype(BF16), wple_ref[...])
    o_ref[0] = x1 + ple * gate


def _out_call(x, p, nums, dens, ga, gb, ybt, wo, wple, wgate):
    B, S, D = x.shape
    tm = ROW_TILE
    full = lambda shape: pl.BlockSpec(shape, lambda b, i: (0,) * len(shape))
    tok = lambda c: pl.BlockSpec((1, tm, c), lambda b, i: (b, i, 0))
    return pl.pallas_call(
        _out_kernel,
        grid=(B, S // tm),
        in_specs=[tok(D), tok(p.shape[-1])] + [tok(D_A)] * 6 + [tok(D_A), tok(D_B),
                  pl.BlockSpec((1, N_HEADS_B, HEAD_DIM, tm), lambda b, i: (b, 0, 0, i)),
                  full(wo.shape), full(wple.shape), full(wgate.shape)],
        out_specs=tok(D),
        out_shape=jax.ShapeDtypeStruct((B, S, D), F32),
        compiler_params=pltpu.CompilerParams(
            dimension_semantics=("arbitrary", "arbitrary"), vmem_limit_bytes=VMEM_LIMIT),
        name="out_proj",
    )(x, p, *nums, *dens, ga, gb, ybt, wo, wple, wgate)


def _rope_tables(S):
    n = HEAD_DIM // 4
    inv = ROPE_THETA ** (-jnp.arange(n, dtype=F32) / n)
    pos = jnp.arange(S, dtype=jnp.int32)
    row_id = (pos // GRID_W).astype(F32)
    col_id = (pos % GRID_W).astype(F32)
    ang = jnp.concatenate([inv[:, None] * row_id[None, :], inv[:, None] * col_id[None, :]], axis=0)
    return jnp.cos(ang), jnp.sin(ang)


def kernel(x, p, ln_g, w_in, qn_a, kn_a, qn_b, kn_b, w_out, w_ple, w_pgate, rel_bias):
    B, S, D = x.shape
    depth = w_in.shape[0]
    cos_t, sin_t = _rope_tables(S)
    bias = _bias_call(rel_bias)
    perm = np.concatenate([np.arange(0, HEAD_DIM, 2), np.arange(1, HEAD_DIM, 2)])
    bcast = lambda g: jnp.broadcast_to(g.astype(F32)[:, None], (HEAD_DIM, ROW_TILE))
    for i in range(depth):
        w = w_in[i]
        c = 0
        wqa = w[:, c:c + D_A]; c += D_A
        wka = w[:, c:c + D_A]; c += D_A
        wva = w[:, c:c + D_A]; c += D_A
        wga = w[:, c:c + D_A]; c += D_A
        wqb = w[:, c:c + D_B]; c += D_B
        wkb = w[:, c:c + D_KV_B]; c += D_KV_B
        wvb = w[:, c:c + D_KV_B]; c += D_KV_B
        wgb = w[:, c:c + D_B]
        wqb = wqb.reshape(D, N_HEADS_B, HEAD_DIM)[:, :, perm].reshape(D, D_B)
        wkb = wkb.reshape(D, N_KV_B, HEAD_DIM)[:, :, perm].reshape(D, D_KV_B)
        wtok = jnp.concatenate([wva, wga, wgb], axis=1).astype(BF16)
        wfeat = jnp.concatenate([wqa, wka, wqb, wkb, wvb], axis=1).T.astype(BF16)

        va, ga, gb, qa, ka, qbt, kb, vbt = _proj_call(
            x, ln_g[i][None, :], wtok, wfeat, bcast(qn_a[i]), bcast(kn_a[i]),
            bcast(qn_b[i][perm]), bcast(kn_b[i][perm]), cos_t, sin_t)

        ybt = _gqa_call(qbt, kb, vbt)

        nums, dens = [], []
        for pat, (_, dilation) in enumerate(DIL_PATTERNS):
            num, den = _band_call(_to_sub(qa, dilation), _to_sub(ka, dilation), _to_sub(va, dilation), bias[pat])
            nums.append(_from_sub(num, dilation, B))
            dens.append(_from_sub(den, dilation, B))

        x = _out_call(x, p[i], nums, dens, ga, gb, ybt, w_out[i].astype(BF16), w_ple[i].astype(BF16),
                      w_pgate[i].astype(BF16))
    return x
```

```python
import functools
import math

import numpy as np
import jax
import jax.numpy as jnp
from jax import lax
from jax.experimental import pallas as pl
from jax.experimental.pallas import tpu as pltpu

HEAD_DIM = 64
N_HEADS_A = 8
N_HEADS_B = 8
N_KV_B = 2
GQA_GROUP = N_HEADS_B // N_KV_B
D_A = N_HEADS_A * HEAD_DIM
D_B = N_HEADS_B * HEAD_DIM
D_KV_B = N_KV_B * HEAD_DIM
DILATIONS = (1, 4, 16)
BAND_BLOCK = 128
BAND_HALF = 64
BAND_SPAN = BAND_BLOCK + 2 * BAND_HALF
SUPER_BLOCK = BAND_BLOCK * max(DILATIONS)
LANES = 128
N_PAIRS = D_A // LANES
GRID_W = 64
ROPE_THETA = 10000.0
N_BUCKETS = 32
MAX_DISTANCE = 1024
EPS = 1e-6
NEG_INF = -1e30
ONES_ROWS = 16

ROW_TILE = 512
Q_TILE = 512
KEY_STEP = 256
LOOKAHEAD = 2
VMEM_LIMIT = 56 * 1024 * 1024

F32 = jnp.float32
BF16 = jnp.bfloat16


def _nt_dot(a, b):
    return lax.dot_general(a, b, (((1,), (1,)), ((), ())), preferred_element_type=F32)


def _dot(a, b):
    return jnp.dot(a, b, preferred_element_type=F32)


def _head_rms(t, gain):
    ms = jnp.mean(t * t, axis=1, keepdims=True)
    return t * lax.rsqrt(ms + EPS) * gain[None]


def _rope_halves(t, cos, sin):
    half = HEAD_DIM // 2
    x1 = t[:, :half, :]
    x2 = t[:, half:, :]
    c = cos[None]
    s = sin[None]
    return jnp.concatenate([x1 * c - x2 * s, x1 * s + x2 * c], axis=1)


def _emit_layouts(y, scr, o1, o4, o16):
    tm = y.shape[0]
    o1[0] = y.astype(BF16)
    for pr in range(N_PAIRS):
        scr[pr] = y[:, pr * LANES:(pr + 1) * LANES]
    for pr in range(N_PAIRS):
        for dil, out in ((4, o4), (16, o16)):
            for r in range(dil):
                lo = (pr * dil + r) * LANES
                out[0, :, lo:lo + LANES] = scr[pr, pl.ds(r, tm // dil, stride=dil), :].astype(BF16)


def _proj_kernel(x_ref, g_ref, wtok_ref, wfeat_ref, gqa_ref, gka_ref, gqb_ref, gkb_ref, cos_ref, sin_ref,
                 ga_ref, gb_ref, q1_ref, q4_ref, q16_ref, k1_ref, k4_ref, k16_ref, v1_ref, v4_ref, v16_ref,
                 qbt_ref, kb_ref, vbt_ref, sq_ref, sk_ref, sv_ref):
    tm = x_ref.shape[1]
    scale = HEAD_DIM ** -0.5
    x = x_ref[0]
    ms = jnp.mean(x * x, axis=-1, keepdims=True)
    h = (x * lax.rsqrt(ms + EPS) * g_ref[...]).astype(BF16)

    tok = _dot(h, wtok_ref[...])
    _emit_layouts(tok[:, :D_A], sv_ref, v1_ref, v4_ref, v16_ref)
    ga_ref[0] = tok[:, D_A:2 * D_A].astype(BF16)
    gb_ref[0] = tok[:, 2 * D_A:].astype(BF16)

    feat = _nt_dot(wfeat_ref[...], h)
    qa = _head_rms(feat[:D_A].reshape(N_HEADS_A, HEAD_DIM, tm), gqa_ref[...]) * scale
    _emit_layouts(qa.reshape(D_A, tm).T, sq_ref, q1_ref, q4_ref, q16_ref)
    ka = _head_rms(feat[D_A:2 * D_A].reshape(N_HEADS_A, HEAD_DIM, tm), gka_ref[...])
    _emit_layouts(ka.reshape(D_A, tm).T, sk_ref, k1_ref, k4_ref, k16_ref)

    o = 2 * D_A
    cos = cos_ref[...]
    sin = sin_ref[...]
    qb = _head_rms(feat[o:o + D_B].reshape(N_HEADS_B, HEAD_DIM, tm), gqb_ref[...])
    qbt_ref[0] = (_rope_halves(qb, cos, sin) * scale).astype(BF16)
    o += D_B
    kb = _head_rms(feat[o:o + D_KV_B].reshape(N_KV_B, HEAD_DIM, tm), gkb_ref[...])
    kb = _rope_halves(kb, cos, sin)
    zeros = jnp.zeros((HEAD_DIM, tm), F32)
    for kv in range(N_KV_B):
        kb_ref[0, kv] = jnp.concatenate([kb[kv], zeros], axis=0).T.astype(BF16)
    o += D_KV_B
    vb = feat[o:o + D_KV_B].reshape(N_KV_B, HEAD_DIM, tm)
    ones = jnp.ones((N_KV_B, ONES_ROWS, tm), F32)
    vbt_ref[0, :, 0] = jnp.concatenate([vb, ones], axis=1).astype(BF16)


def _proj_call(x, ln_g, wtok, wfeat, gqa, gka, gqb, gkb, cos_t, sin_t):
    B, S, D = x.shape
    tm = ROW_TILE
    nt = S // tm
    full = lambda shape: pl.BlockSpec(shape, lambda b, i: (0,) * len(shape))
    tok_spec = pl.BlockSpec((1, tm, D_A), lambda b, i: (b, i, 0))
    lay_shapes, lay_specs = [], []
    for _ in range(3):
        for dil in DILATIONS:
            lay_shapes.append(jax.ShapeDtypeStruct((B, S // dil, dil * D_A), BF16))
            lay_specs.append(pl.BlockSpec((1, tm // dil, dil * D_A), lambda b, i: (b, i, 0)))
    out_shape = (
        jax.ShapeDtypeStruct((B, S, D_A), BF16),
        jax.ShapeDtypeStruct((B, S, D_B), BF16),
        *lay_shapes,
        jax.ShapeDtypeStruct((B, N_HEADS_B, HEAD_DIM, S), BF16),
        jax.ShapeDtypeStruct((B, N_KV_B, S, 2 * HEAD_DIM), BF16),
        jax.ShapeDtypeStruct((B, N_KV_B, nt, HEAD_DIM + ONES_ROWS, tm), BF16),
    )
    out_specs = (
        tok_spec, tok_spec,
        *lay_specs,
        pl.BlockSpec((1, N_HEADS_B, HEAD_DIM, tm), lambda b, i: (b, 0, 0, i)),
        pl.BlockSpec((1, N_KV_B, tm, 2 * HEAD_DIM), lambda b, i: (b, 0, i, 0)),
        pl.BlockSpec((1, N_KV_B, 1, HEAD_DIM + ONES_ROWS, tm), lambda b, i: (b, 0, i, 0, 0)),
    )
    in_specs = [
        pl.BlockSpec((1, tm, D), lambda b, i: (b, i, 0)),
        full((1, D)),
        full(wtok.shape),
        full(wfeat.shape),
        full((HEAD_DIM, tm)), full((HEAD_DIM, tm)), full((HEAD_DIM, tm)), full((HEAD_DIM, tm)),
        pl.BlockSpec((HEAD_DIM // 2, tm), lambda b, i: (0, i)),
        pl.BlockSpec((HEAD_DIM // 2, tm), lambda b, i: (0, i)),
    ]
    return pl.pallas_call(
        _proj_kernel,
        grid=(B, nt),
        in_specs=in_specs,
        out_specs=out_specs,
        out_shape=out_shape,
        scratch_shapes=[pltpu.VMEM((N_PAIRS, tm, LANES), F32)] * 3,
        compiler_params=pltpu.CompilerParams(
            dimension_semantics=("arbitrary", "arbitrary"), vmem_limit_bytes=VMEM_LIMIT),
        name="proj",
    )(x, ln_g, wtok, wfeat, gqa, gka, gqb, gkb, cos_t, sin_t)


def _gqa_kernel(qt_ref, k_ref, vt_ref, o_ref):
    tq = qt_ref.shape[3]
    n_chunks = vt_ref.shape[2]
    tk = vt_ref.shape[4]
    q = qt_ref[0, 0]
    qext = jnp.concatenate([q, jnp.zeros_like(q)], axis=0)
    acc = jnp.zeros((vt_ref.shape[3], tq), F32)
    per_chunk = tk // KEY_STEP
    n_steps = n_chunks * per_chunk

    def scores(t):
        return _dot(k_ref[0, 0, t * KEY_STEP:(t + 1) * KEY_STEP, :], qext)

    pending = [scores(t) for t in range(min(LOOKAHEAD, n_steps))]
    for t in range(n_steps):
        if t + LOOKAHEAD < n_steps:
            pending.append(scores(t + LOOKAHEAD))
        p = jnp.exp(pending[t]).astype(BF16)
        lo = (t % per_chunk) * KEY_STEP
        acc = acc + _dot(vt_ref[0, 0, t // per_chunk, :, lo:lo + KEY_STEP], p)
    o_ref[0, 0] = acc[:HEAD_DIM] / acc[HEAD_DIM:HEAD_DIM + 1]


def _gqa_call(qbt, kb, vbt):
    B, H, _, S = qbt.shape
    tq = Q_TILE
    nt, rows, tk = vbt.shape[2], vbt.shape[3], vbt.shape[4]
    return pl.pallas_call(
        _gqa_kernel,
        grid=(B, H, S // tq),
        in_specs=[
            pl.BlockSpec((1, 1, HEAD_DIM, tq), lambda b, h, i: (b, h, 0, i)),
            pl.BlockSpec((1, 1, S, 2 * HEAD_DIM), lambda b, h, i: (b, h // GQA_GROUP, 0, 0)),
            pl.BlockSpec((1, 1, nt, rows, tk), lambda b, h, i: (b, h // GQA_GROUP, 0, 0, 0)),
        ],
        out_specs=pl.BlockSpec((1, 1, HEAD_DIM, tq), lambda b, h, i: (b, h, 0, i)),
        out_shape=jax.ShapeDtypeStruct((B, H, HEAD_DIM, S), F32),
        compiler_params=pltpu.CompilerParams(
            dimension_semantics=("arbitrary", "arbitrary", "arbitrary"), vmem_limit_bytes=VMEM_LIMIT),
        name="gqa",
    )(qbt, kb, vbt)


def _band_bucket_index():
    nb = N_BUCKETS // 2
    max_exact = nb // 2
    rel = np.arange(BAND_SPAN)[None, :] - BAND_HALF - np.arange(BAND_BLOCK)[:, None]
    out = []
    for dilation in DILATIONS:
        dist = rel * dilation
        ret = np.where(dist > 0, nb, 0)
        n = np.abs(dist)
        nf = np.maximum(n, max_exact).astype(np.float32)
        large = max_exact + (np.log(nf / np.float32(max_exact)) / np.float32(math.log(MAX_DISTANCE / max_exact))
                             * np.float32(nb - max_exact)).astype(np.int32)
        large = np.minimum(large, nb - 1)
        bucket = ret + np.where(n < max_exact, n, large)
        out.append(np.where(np.abs(rel) <= BAND_HALF, bucket, -1))
    return np.stack(out).astype(np.int32)


def _bias_kernel(table_ref, idx_ref, o_ref):
    h = pl.program_id(1)
    idx = idx_ref[0]
    acc = jnp.full(idx.shape, NEG_INF, F32)
    for b in range(N_BUCKETS):
        acc = jnp.where(idx == b, table_ref[b, h], acc)
    o_ref[0, 0] = acc


def _bias_call(rel_bias):
    idx = jnp.asarray(_band_bucket_index())
    n_pat = len(DILATIONS)
    return pl.pallas_call(
        _bias_kernel,
        grid=(n_pat, N_HEADS_A),
        in_specs=[
            pl.BlockSpec(memory_space=pltpu.SMEM),
            pl.BlockSpec((1, BAND_BLOCK, BAND_SPAN), lambda p, h: (p, 0, 0)),
        ],
        out_specs=pl.BlockSpec((1, 1, BAND_BLOCK, BAND_SPAN), lambda p, h: (p, h, 0, 0)),
        out_shape=jax.ShapeDtypeStruct((n_pat, N_HEADS_A, BAND_BLOCK, BAND_SPAN), F32),
        name="bias_bands",
    )(rel_bias, idx)


def _band_kernel(*refs):
    n_in = 7 * len(DILATIONS)
    ins, bias_ref, o_ref, num_ref, den_ref = refs[:n_in], refs[n_in], refs[n_in + 1], refs[n_in + 2], refs[n_in + 3]
    n = pl.program_id(1)
    last = pl.num_programs(1) - 1
    col = lax.broadcasted_iota(jnp.int32, (1, BAND_SPAN), 1)
    edge_lo = jnp.where((n == 0) & (col < BAND_HALF), NEG_INF, 0.0).astype(F32)
    edge_hi = jnp.where((n == last) & (col >= BAND_BLOCK + BAND_HALF), NEG_INF, 0.0).astype(F32)
    lane = lax.broadcasted_iota(jnp.int32, (BAND_BLOCK, LANES), 1)
    first_half = lane < HEAD_DIM
    ones = jnp.ones((BAND_SPAN, LANES), BF16)

    def band(main, prev, nxt, blk, n_blk, ls):
        lo = blk * BAND_BLOCK - BAND_HALF
        parts = []
        if blk == 0:
            parts.append(prev[0, :, ls])
        start = max(lo, 0)
        stop = min(lo + BAND_SPAN, n_blk * BAND_BLOCK)
        parts.append(main[0, start:stop, ls])
        if blk == n_blk - 1:
            parts.append(nxt[0, :, ls])
        return parts[0] if len(parts) == 1 else jnp.concatenate(parts, axis=0)

    units = []
    for pat in reversed(range(len(DILATIONS))):
        dil = DILATIONS[pat]
        q_ref, k_ref, kp_ref, kn_ref, v_ref, vp_ref, vn_ref = ins[7 * pat:7 * pat + 7]
        n_blk = SUPER_BLOCK // (dil * BAND_BLOCK)
        for r in range(dil):
            for blk in range(n_blk):
                units.append((pat, dil, r, blk, n_blk, q_ref, k_ref, kp_ref, kn_ref, v_ref, vp_ref, vn_ref))

    def stage_scores(u):
        pat, dil, r, blk, n_blk, q_ref, k_ref, kp_ref, kn_ref = u[:9]
        ls = slice(r * LANES, (r + 1) * LANES)
        q = q_ref[0, blk * BAND_BLOCK:(blk + 1) * BAND_BLOCK, ls]
        kband = band(k_ref, kp_ref, kn_ref, blk, n_blk, ls)
        zero = jnp.zeros_like(q)
        return [_nt_dot(jnp.where(first_half, q, zero), kband), _nt_dot(jnp.where(first_half, zero, q), kband)]

    def stage_values(u, s_pair):
        pat, dil, r, blk, n_blk = u[:5]
        v_ref, vp_ref, vn_ref = u[9:12]
        ls = slice(r * LANES, (r + 1) * LANES)
        vext = jnp.concatenate([band(v_ref, vp_ref, vn_ref, blk, n_blk, ls), ones], axis=1)
        outs = []
        for half in range(2):
            s = s_pair[half] + bias_ref[pat, half]
            if blk == 0:
                s = s + edge_lo
            if blk == n_blk - 1:
                s = s + edge_hi
            outs.append(_dot(jnp.exp(s).astype(BF16), vext))
        num = jnp.where(first_half, outs[0][:, :LANES], outs[1][:, :LANES])
        den = jnp.where(first_half, outs[0][:, LANES:], outs[1][:, LANES:])
        if dil == DILATIONS[-1]:
            rows = pl.ds(r, BAND_BLOCK, stride=dil)
            num_ref[rows, :] = num
            den_ref[rows, :] = den
        elif dil > 1:
            rows = pl.ds(dil * blk * BAND_BLOCK + r, BAND_BLOCK, stride=dil)
            num_ref[rows, :] = num_ref[rows, :] + num
            den_ref[rows, :] = den_ref[rows, :] + den
        else:
            rows = slice(blk * BAND_BLOCK, (blk + 1) * BAND_BLOCK)
            o_ref[0, rows, :] = (num_ref[rows, :] + num) / (den_ref[rows, :] + den)

    pending = [stage_scores(u) for u in units[:LOOKAHEAD]]
    for t, u in enumerate(units):
        if t + LOOKAHEAD < len(units):
            pending.append(stage_scores(units[t + LOOKAHEAD]))
        stage_values(u, pending[t])
        pending[t] = None


def _band_call(layouts, bias):
    B, S, _ = layouts[0][0].shape
    n_sb = S // SUPER_BLOCK
    in_specs, args = [], []
    for (q, k, v), dil in zip(layouts, DILATIONS):
        rows = SUPER_BLOCK // dil
        width = dil * LANES
        n_halo = (S // dil) // BAND_HALF
        per_sb = rows // BAND_HALF
        main = pl.BlockSpec((1, rows, width), lambda b, n, pr: (b, n, pr))
        prev = pl.BlockSpec((1, BAND_HALF, width),
                            lambda b, n, pr, per_sb=per_sb: (b, jnp.maximum(n * per_sb - 1, 0), pr))
        nxt = pl.BlockSpec((1, BAND_HALF, width),
                           lambda b, n, pr, per_sb=per_sb, n_halo=n_halo: (b, jnp.minimum((n + 1) * per_sb, n_halo - 1), pr))
        in_specs += [main, main, prev, nxt, main, prev, nxt]
        args += [q, k, k, k, v, v, v]
    in_specs.append(pl.BlockSpec((len(DILATIONS), 2, BAND_BLOCK, BAND_SPAN), lambda b, n, pr: (0, pr, 0, 0)))
    return pl.pallas_call(
        _band_kernel,
        grid=(B, n_sb, N_PAIRS),
        in_specs=in_specs,
        out_specs=pl.BlockSpec((1, SUPER_BLOCK, LANES), lambda b, n, pr: (b, n, pr)),
        out_shape=jax.ShapeDtypeStruct((B, S, D_A), F32),
        scratch_shapes=[pltpu.VMEM((SUPER_BLOCK, LANES), F32)] * 2,
        compiler_params=pltpu.CompilerParams(
            dimension_semantics=("arbitrary", "arbitrary", "arbitrary"), vmem_limit_bytes=VMEM_LIMIT),
        name="band",
    )(*args, bias)


def _silu(g):
    return g * jax.nn.sigmoid(g)


def _out_kernel(x_ref, p_ref, ya_ref, ga_ref, gb_ref, ybt_ref, wo_ref, wple_ref, wgate_ref, o_ref):
    tm = x_ref.shape[1]
    ya = ya_ref[0] * _silu(ga_ref[0].astype(F32))
    yb = ybt_ref[0].reshape(D_B, tm).T * _silu(gb_ref[0].astype(F32))
    x1 = x_ref[0] + _dot(ya.astype(BF16), wo_ref[:D_A, :]) + _dot(yb.astype(BF16), wo_ref[D_A:, :])
    gate = jax.nn.sigmoid(_dot(x1.astype(BF16), wgate_ref[...]))
    ple = _dot(p_ref[0].astype(BF16), wple_ref[...])
    o_ref[0] = x1 + ple * gate


def _out_call(x, p, ya, ga, gb, ybt, wo, wple, wgate):
    B, S, D = x.shape
    tm = ROW_TILE
    full = lambda shape: pl.BlockSpec(shape, lambda b, i: (0,) * len(shape))
    tok = lambda c: pl.BlockSpec((1, tm, c), lambda b, i: (b, i, 0))
    return pl.pallas_call(
        _out_kernel,
        grid=(B, S // tm),
        in_specs=[tok(D), tok(p.shape[-1]), tok(D_A), tok(D_A), tok(D_B),
                  pl.BlockSpec((1, N_HEADS_B, HEAD_DIM, tm), lambda b, i: (b, 0, 0, i)),
                  full(wo.shape), full(wple.shape), full(wgate.shape)],
        out_specs=tok(D),
        out_shape=jax.ShapeDtypeStruct((B, S, D), F32),
        compiler_params=pltpu.CompilerParams(
            dimension_semantics=("arbitrary", "arbitrary"), vmem_limit_bytes=VMEM_LIMIT),
        name="out_proj",
    )(x, p, ya, ga, gb, ybt, wo, wple, wgate)


def _rope_tables(S):
    n = HEAD_DIM // 4
    inv = ROPE_THETA ** (-jnp.arange(n, dtype=F32) / n)
    pos = jnp.arange(S, dtype=jnp.int32)
    row_id = (pos // GRID_W).astype(F32)
    col_id = (pos % GRID_W).astype(F32)
    ang = jnp.concatenate([inv[:, None] * row_id[None, :], inv[:, None] * col_id[None, :]], axis=0)
    return jnp.cos(ang), jnp.sin(ang)


def kernel(x, p, ln_g, w_in, qn_a, kn_a, qn_b, kn_b, w_out, w_ple, w_pgate, rel_bias):
    B, S, D = x.shape
    depth = w_in.shape[0]
    cos_t, sin_t = _rope_tables(S)
    bias = _bias_call(rel_bias)
    perm = np.concatenate([np.arange(0, HEAD_DIM, 2), np.arange(1, HEAD_DIM, 2)])
    bcast = lambda g: jnp.broadcast_to(g.astype(F32)[:, None], (HEAD_DIM, ROW_TILE))
    for i in range(depth):
        w = w_in[i]
        c = 0
        wqa = w[:, c:c + D_A]; c += D_A
        wka = w[:, c:c + D_A]; c += D_A
        wva = w[:, c:c + D_A]; c += D_A
        wga = w[:, c:c + D_A]; c += D_A
        wqb = w[:, c:c + D_B]; c += D_B
        wkb = w[:, c:c + D_KV_B]; c += D_KV_B
        wvb = w[:, c:c + D_KV_B]; c += D_KV_B
        wgb = w[:, c:c + D_B]
        wqb = wqb.reshape(D, N_HEADS_B, HEAD_DIM)[:, :, perm].reshape(D, D_B)
        wkb = wkb.reshape(D, N_KV_B, HEAD_DIM)[:, :, perm].reshape(D, D_KV_B)
        wtok = jnp.concatenate([wva, wga, wgb], axis=1).astype(BF16)
        wfeat = jnp.concatenate([wqa, wka, wqb, wkb, wvb], axis=1).T.astype(BF16)

        outs = _proj_call(
            x, ln_g[i][None, :], wtok, wfeat, bcast(qn_a[i]), bcast(kn_a[i]),
            bcast(qn_b[i][perm]), bcast(kn_b[i][perm]), cos_t, sin_t)
        ga, gb = outs[0], outs[1]
        qs, ks, vs = outs[2:5], outs[5:8], outs[8:11]
        qbt, kb, vbt = outs[11:14]

        ybt = _gqa_call(qbt, kb, vbt)
        ya = _band_call(list(zip(qs, ks, vs)), bias)
        x = _out_call(x, p[i], ya, ga, gb, ybt, w_out[i].astype(BF16), w_ple[i].astype(BF16),
                      w_pgate[i].astype(BF16))
    return x
```

```python
import functools
import math

import numpy as np
import jax
import jax.numpy as jnp
from jax import lax
from jax.experimental import pallas as pl
from jax.experimental.pallas import tpu as pltpu

HEAD_DIM = 64
N_HEADS_A = 8
N_HEADS_B = 8
N_KV_B = 2
GQA_GROUP = N_HEADS_B // N_KV_B
D_A = N_HEADS_A * HEAD_DIM
D_B = N_HEADS_B * HEAD_DIM
D_KV_B = N_KV_B * HEAD_DIM
DILATIONS = (1, 4, 16)
BAND_BLOCK = 128
BAND_HALF = 64
BAND_SPAN = BAND_BLOCK + 2 * BAND_HALF
SUPER_BLOCK = BAND_BLOCK * max(DILATIONS)
LANES = 128
N_PAIRS = D_A // LANES
GRID_W = 64
ROPE_THETA = 10000.0
N_BUCKETS = 32
MAX_DISTANCE = 1024
EPS = 1e-6
NEG_INF = -1e30

ROW_TILE = 512
Q_TILE = 512
KEY_STEP = 256
LOOKAHEAD = 2
VMEM_LIMIT = 56 * 1024 * 1024

F32 = jnp.float32
BF16 = jnp.bfloat16


def _nt_dot(a, b):
    return lax.dot_general(a, b, (((1,), (1,)), ((), ())), preferred_element_type=F32)


def _dot(a, b):
    return jnp.dot(a, b, preferred_element_type=F32)


def _head_rms(t, gain):
    ms = jnp.mean(t * t, axis=1, keepdims=True)
    return t * lax.rsqrt(ms + EPS) * gain[None]


def _rope_halves(t, cos, sin):
    half = HEAD_DIM // 2
    x1 = t[:, :half, :]
    x2 = t[:, half:, :]
    c = cos[None]
    s = sin[None]
    return jnp.concatenate([x1 * c - x2 * s, x1 * s + x2 * c], axis=1)


def _emit_layouts(y, scr, o1, o4, o16):
    tm = y.shape[0]
    o1[0] = y.astype(BF16)
    for pr in range(N_PAIRS):
        scr[pr] = y[:, pr * LANES:(pr + 1) * LANES]
    for pr in range(N_PAIRS):
        for dil, out in ((4, o4), (16, o16)):
            for r in range(dil):
                lo = (pr * dil + r) * LANES
                out[0, :, lo:lo + LANES] = scr[pr, pl.ds(r, tm // dil, stride=dil), :].astype(BF16)


def _proj_kernel(x_ref, g_ref, wtok_ref, wfeat_ref, gqa_ref, gka_ref, gqb_ref, gkb_ref, cos_ref, sin_ref,
                 ga_ref, gb_ref, q1_ref, q4_ref, q16_ref, k1_ref, k4_ref, k16_ref, v1_ref, v4_ref, v16_ref,
                 qbt_ref, kb_ref, vbt_ref, sq_ref, sk_ref, sv_ref):
    tm = x_ref.shape[1]
    scale = HEAD_DIM ** -0.5
    x = x_ref[0]
    ms = jnp.mean(x * x, axis=-1, keepdims=True)
    h = (x * lax.rsqrt(ms + EPS) * g_ref[...]).astype(BF16)

    qa = _nt_dot(wfeat_ref[:D_A, :], h).reshape(N_HEADS_A, HEAD_DIM, tm)
    qa = _head_rms(qa, gqa_ref[...]) * scale
    _emit_layouts(qa.reshape(D_A, tm).T, sq_ref, q1_ref, q4_ref, q16_ref)
    ka = _nt_dot(wfeat_ref[D_A:2 * D_A, :], h).reshape(N_HEADS_A, HEAD_DIM, tm)
    ka = _head_rms(ka, gka_ref[...])
    _emit_layouts(ka.reshape(D_A, tm).T, sk_ref, k1_ref, k4_ref, k16_ref)
    _emit_layouts(_dot(h, wtok_ref[:, :D_A]), sv_ref, v1_ref, v4_ref, v16_ref)
    gates = _dot(h, wtok_ref[:, D_A:])
    ga_ref[0] = gates[:, :D_A].astype(BF16)
    gb_ref[0] = gates[:, D_A:].astype(BF16)

    o = 2 * D_A
    featb = _nt_dot(wfeat_ref[o:, :], h)
    cos = cos_ref[...]
    sin = sin_ref[...]
    qb = _head_rms(featb[:D_B].reshape(N_HEADS_B, HEAD_DIM, tm), gqb_ref[...])
    qbt_ref[0] = (_rope_halves(qb, cos, sin) * scale).astype(BF16)
    kb = _head_rms(featb[D_B:D_B + D_KV_B].reshape(N_KV_B, HEAD_DIM, tm), gkb_ref[...])
    kb = _rope_halves(kb, cos, sin)
    zeros = jnp.zeros((HEAD_DIM, tm), F32)
    for kv in range(N_KV_B):
        kb_ref[0, kv] = jnp.concatenate([kb[kv], zeros], axis=0).T.astype(BF16)
    vbt_ref[0, :, 0] = featb[D_B + D_KV_B:].reshape(N_KV_B, HEAD_DIM, tm).astype(BF16)


def _proj_call(x, ln_g, wtok, wfeat, gqa, gka, gqb, gkb, cos_t, sin_t):
    B, S, D = x.shape
    tm = ROW_TILE
    nt = S // tm
    full = lambda shape: pl.BlockSpec(shape, lambda b, i: (0,) * len(shape))
    tok_spec = pl.BlockSpec((1, tm, D_A), lambda b, i: (b, i, 0))
    lay_shapes, lay_specs = [], []
    for _ in range(3):
        for dil in DILATIONS:
            lay_shapes.append(jax.ShapeDtypeStruct((B, S // dil, dil * D_A), BF16))
            lay_specs.append(pl.BlockSpec((1, tm // dil, dil * D_A), lambda b, i: (b, i, 0)))
    out_shape = (
        jax.ShapeDtypeStruct((B, S, D_A), BF16),
        jax.ShapeDtypeStruct((B, S, D_B), BF16),
        *lay_shapes,
        jax.ShapeDtypeStruct((B, N_HEADS_B, HEAD_DIM, S), BF16),
        jax.ShapeDtypeStruct((B, N_KV_B, S, 2 * HEAD_DIM), BF16),
        jax.ShapeDtypeStruct((B, N_KV_B, nt, HEAD_DIM, tm), BF16),
    )
    out_specs = (
        tok_spec, tok_spec,
        *lay_specs,
        pl.BlockSpec((1, N_HEADS_B, HEAD_DIM, tm), lambda b, i: (b, 0, 0, i)),
        pl.BlockSpec((1, N_KV_B, tm, 2 * HEAD_DIM), lambda b, i: (b, 0, i, 0)),
        pl.BlockSpec((1, N_KV_B, 1, HEAD_DIM, tm), lambda b, i: (b, 0, i, 0, 0)),
    )
    in_specs = [
        pl.BlockSpec((1, tm, D), lambda b, i: (b, i, 0)),
        full((1, D)),
        full(wtok.shape),
        full(wfeat.shape),
        full((HEAD_DIM, tm)), full((HEAD_DIM, tm)), full((HEAD_DIM, tm)), full((HEAD_DIM, tm)),
        pl.BlockSpec((HEAD_DIM // 2, tm), lambda b, i: (0, i)),
        pl.BlockSpec((HEAD_DIM // 2, tm), lambda b, i: (0, i)),
    ]
    return pl.pallas_call(
        _proj_kernel,
        grid=(B, nt),
        in_specs=in_specs,
        out_specs=out_specs,
        out_shape=out_shape,
        scratch_shapes=[pltpu.VMEM((N_PAIRS, tm, LANES), F32)] * 3,
        compiler_params=pltpu.CompilerParams(
            dimension_semantics=("arbitrary", "arbitrary"), vmem_limit_bytes=VMEM_LIMIT),
        name="proj",
    )(x, ln_g, wtok, wfeat, gqa, gka, gqb, gkb, cos_t, sin_t)


def _gqa_kernel(qt_ref, k_ref, vt_ref, o_ref):
    tq = qt_ref.shape[3]
    n_chunks = vt_ref.shape[2]
    tk = vt_ref.shape[4]
    per_chunk = tk // KEY_STEP
    n_steps = n_chunks * per_chunk
    sub = 8
    qext = []
    for g in range(GQA_GROUP):
        q = qt_ref[0, g]
        qext.append(jnp.concatenate([q, jnp.zeros_like(q)], axis=0))
    units = [(g, t) for g in range(GQA_GROUP) for t in range(n_steps)]

    def scores(u):
        g, t = u
        return _dot(k_ref[0, 0, t * KEY_STEP:(t + 1) * KEY_STEP, :], qext[g])

    pending = [scores(u) for u in units[:LOOKAHEAD]]
    for i, (g, t) in enumerate(units):
        if i + LOOKAHEAD < len(units):
            pending.append(scores(units[i + LOOKAHEAD]))
        if t == 0:
            acc = jnp.zeros((HEAD_DIM, tq), F32)
            den = jnp.zeros((sub, tq), F32)
        p = jnp.exp(pending[i])
        pending[i] = None
        den = den + p.reshape(KEY_STEP // sub, sub, tq).sum(axis=0)
        lo = (t % per_chunk) * KEY_STEP
        acc = acc + _dot(vt_ref[0, 0, t // per_chunk, :HEAD_DIM, lo:lo + KEY_STEP], p.astype(BF16))
        if t == n_steps - 1:
            o_ref[0, g] = acc / den.sum(axis=0, keepdims=True)


def _gqa_call(qbt, kb, vbt):
    B, H, _, S = qbt.shape
    tq = Q_TILE
    nt, rows, tk = vbt.shape[2], vbt.shape[3], vbt.shape[4]
    return pl.pallas_call(
        _gqa_kernel,
        grid=(B, N_KV_B, S // tq),
        in_specs=[
            pl.BlockSpec((1, GQA_GROUP, HEAD_DIM, tq), lambda b, kv, i: (b, kv, 0, i)),
            pl.BlockSpec((1, 1, S, 2 * HEAD_DIM), lambda b, kv, i: (b, kv, 0, 0)),
            pl.BlockSpec((1, 1, nt, rows, tk), lambda b, kv, i: (b, kv, 0, 0, 0)),
        ],
        out_specs=pl.BlockSpec((1, GQA_GROUP, HEAD_DIM, tq), lambda b, kv, i: (b, kv, 0, i)),
        out_shape=jax.ShapeDtypeStruct((B, H, HEAD_DIM, S), F32),
        compiler_params=pltpu.CompilerParams(
            dimension_semantics=("arbitrary", "arbitrary", "arbitrary"), vmem_limit_bytes=VMEM_LIMIT),
        name="gqa",
    )(qbt, kb, vbt)


def _band_bucket_index():
    nb = N_BUCKETS // 2
    max_exact = nb // 2
    rel = np.arange(BAND_SPAN)[None, :] - BAND_HALF - np.arange(BAND_BLOCK)[:, None]
    out = []
    for dilation in DILATIONS:
        dist = rel * dilation
        ret = np.where(dist > 0, nb, 0)
        n = np.abs(dist)
        nf = np.maximum(n, max_exact).astype(np.float32)
        large = max_exact + (np.log(nf / np.float32(max_exact)) / np.float32(math.log(MAX_DISTANCE / max_exact))
                             * np.float32(nb - max_exact)).astype(np.int32)
        large = np.minimum(large, nb - 1)
        bucket = ret + np.where(n < max_exact, n, large)
        out.append(np.where(np.abs(rel) <= BAND_HALF, bucket, -1))
    return np.stack(out).astype(np.int32)


def _bias_kernel(table_ref, idx_ref, o_ref):
    h = pl.program_id(1)
    idx = idx_ref[0]
    acc = jnp.full(idx.shape, NEG_INF, F32)
    for b in range(N_BUCKETS):
        acc = jnp.where(idx == b, table_ref[b, h], acc)
    o_ref[0, 0] = acc


def _bias_call(rel_bias):
    idx = jnp.asarray(_band_bucket_index())
    n_pat = len(DILATIONS)
    return pl.pallas_call(
        _bias_kernel,
        grid=(n_pat, N_HEADS_A),
        in_specs=[
            pl.BlockSpec(memory_space=pltpu.SMEM),
            pl.BlockSpec((1, BAND_BLOCK, BAND_SPAN), lambda p, h: (p, 0, 0)),
        ],
        out_specs=pl.BlockSpec((1, 1, BAND_BLOCK, BAND_SPAN), lambda p, h: (p, h, 0, 0)),
        out_shape=jax.ShapeDtypeStruct((n_pat, N_HEADS_A, BAND_BLOCK, BAND_SPAN), F32),
        name="bias_bands",
    )(rel_bias, idx)


def _band_kernel(*refs):
    n_in = 7 * len(DILATIONS)
    ins, bias_ref, o_ref, num_ref, den_ref = refs[:n_in], refs[n_in], refs[n_in + 1], refs[n_in + 2], refs[n_in + 3]
    n = pl.program_id(1)
    last = pl.num_programs(1) - 1
    col = lax.broadcasted_iota(jnp.int32, (1, BAND_SPAN), 1)
    edge_lo = jnp.where((n == 0) & (col < BAND_HALF), NEG_INF, 0.0).astype(F32)
    edge_hi = jnp.where((n == last) & (col >= BAND_BLOCK + BAND_HALF), NEG_INF, 0.0).astype(F32)
    lane = lax.broadcasted_iota(jnp.int32, (BAND_BLOCK, LANES), 1)
    first_half = lane < HEAD_DIM
    ones = jnp.ones((BAND_SPAN, LANES), BF16)

    def band(main, prev, nxt, blk, n_blk, ls):
        lo = blk * BAND_BLOCK - BAND_HALF
        parts = []
        if blk == 0:
            parts.append(prev[0, :, ls])
        start = max(lo, 0)
        stop = min(lo + BAND_SPAN, n_blk * BAND_BLOCK)
        parts.append(main[0, start:stop, ls])
        if blk == n_blk - 1:
            parts.append(nxt[0, :, ls])
        return parts[0] if len(parts) == 1 else jnp.concatenate(parts, axis=0)

    units = []
    for pat in reversed(range(len(DILATIONS))):
        dil = DILATIONS[pat]
        q_ref, k_ref, kp_ref, kn_ref, v_ref, vp_ref, vn_ref = ins[7 * pat:7 * pat + 7]
        n_blk = SUPER_BLOCK // (dil * BAND_BLOCK)
        for r in range(dil):
            for blk in range(n_blk):
                units.append((pat, dil, r, blk, n_blk, q_ref, k_ref, kp_ref, kn_ref, v_ref, vp_ref, vn_ref))

    def stage_scores(u):
        pat, dil, r, blk, n_blk, q_ref, k_ref, kp_ref, kn_ref = u[:9]
        ls = slice(r * LANES, (r + 1) * LANES)
        q = q_ref[0, blk * BAND_BLOCK:(blk + 1) * BAND_BLOCK, ls]
        kband = band(k_ref, kp_ref, kn_ref, blk, n_blk, ls)
        zero = jnp.zeros_like(q)
        return [_nt_dot(jnp.where(first_half, q, zero), kband), _nt_dot(jnp.where(first_half, zero, q), kband)]

    def stage_values(u, s_pair):
        pat, dil, r, blk, n_blk = u[:5]
        v_ref, vp_ref, vn_ref = u[9:12]
        ls = slice(r * LANES, (r + 1) * LANES)
        vext = jnp.concatenate([band(v_ref, vp_ref, vn_ref, blk, n_blk, ls), ones], axis=1)
        outs = []
        for half in range(2):
            s = s_pair[half] + bias_ref[pat, half]
            if blk == 0:
                s = s + edge_lo
            if blk == n_blk - 1:
                s = s + edge_hi
            outs.append(_dot(jnp.exp(s).astype(BF16), vext))
        num = jnp.where(first_half, outs[0][:, :LANES], outs[1][:, :LANES])
        den = jnp.where(first_half, outs[0][:, LANES:], outs[1][:, LANES:])
        if dil == DILATIONS[-1]:
            rows = pl.ds(r, BAND_BLOCK, stride=dil)
            num_ref[rows, :] = num
            den_ref[rows, :] = den
        elif dil > 1:
            rows = pl.ds(dil * blk * BAND_BLOCK + r, BAND_BLOCK, stride=dil)
            num_ref[rows, :] = num_ref[rows, :] + num
            den_ref[rows, :] = den_ref[rows, :] + den
        else:
            rows = slice(blk * BAND_BLOCK, (blk + 1) * BAND_BLOCK)
            o_ref[0, rows, :] = (num_ref[rows, :] + num) / (den_ref[rows, :] + den)

    pending = [stage_scores(u) for u in units[:LOOKAHEAD]]
    for t, u in enumerate(units):
        if t + LOOKAHEAD < len(units):
            pending.append(stage_scores(units[t + LOOKAHEAD]))
        stage_values(u, pending[t])
        pending[t] = None


def _band_call(layouts, bias):
    B, S, _ = layouts[0][0].shape
    n_sb = S // SUPER_BLOCK
    in_specs, args = [], []
    for (q, k, v), dil in zip(layouts, DILATIONS):
        rows = SUPER_BLOCK // dil
        width = dil * LANES
        n_halo = (S // dil) // BAND_HALF
        per_sb = rows // BAND_HALF
        main = pl.BlockSpec((1, rows, width), lambda b, n, pr: (b, n, pr))
        prev = pl.BlockSpec((1, BAND_HALF, width),
                            lambda b, n, pr, per_sb=per_sb: (b, jnp.maximum(n * per_sb - 1, 0), pr))
        nxt = pl.BlockSpec((1, BAND_HALF, width),
                           lambda b, n, pr, per_sb=per_sb, n_halo=n_halo: (b, jnp.minimum((n + 1) * per_sb, n_halo - 1), pr))
        in_specs += [main, main, prev, nxt, main, prev, nxt]
        args += [q, k, k, k, v, v, v]
    in_specs.append(pl.BlockSpec((len(DILATIONS), 2, BAND_BLOCK, BAND_SPAN), lambda b, n, pr: (0, pr, 0, 0)))
    return pl.pallas_call(
        _band_kernel,
        grid=(B, n_sb, N_PAIRS),
        in_specs=in_specs,
        out_specs=pl.BlockSpec((1, SUPER_BLOCK, LANES), lambda b, n, pr: (b, n, pr)),
        out_shape=jax.ShapeDtypeStruct((B, S, D_A), F32),
        scratch_shapes=[pltpu.VMEM((SUPER_BLOCK, LANES), F32)] * 2,
        compiler_params=pltpu.CompilerParams(
            dimension_semantics=("arbitrary", "arbitrary", "arbitrary"), vmem_limit_bytes=VMEM_LIMIT),
        name="band",
    )(*args, bias)


def _silu(g):
    return g * jax.nn.sigmoid(g)


def _out_kernel(x_ref, p_ref, ya_ref, ga_ref, gb_ref, ybt_ref, wo_ref, wple_ref, wgate_ref, o_ref):
    tm = x_ref.shape[1]
    ya = ya_ref[0] * _silu(ga_ref[0].astype(F32))
    yb = ybt_ref[0].reshape(D_B, tm).T * _silu(gb_ref[0].astype(F32))
    x1 = x_ref[0] + _dot(ya.astype(BF16), wo_ref[:D_A, :]) + _dot(yb.astype(BF16), wo_ref[D_A:, :])
    gate = jax.nn.sigmoid(_dot(x1.astype(BF16), wgate_ref[...]))
    ple = _dot(p_ref[0].astype(BF16), wple_ref[...])
    o_ref[0] = x1 + ple * gate


def _out_call(x, p, ya, ga, gb, ybt, wo, wple, wgate):
    B, S, D = x.shape
    tm = ROW_TILE
    full = lambda shape: pl.BlockSpec(shape, lambda b, i: (0,) * len(shape))
    tok = lambda c: pl.BlockSpec((1, tm, c), lambda b, i: (b, i, 0))
    return pl.pallas_call(
        _out_kernel,
        grid=(B, S // tm),
        in_specs=[tok(D), tok(p.shape[-1]), tok(D_A), tok(D_A), tok(D_B),
                  pl.BlockSpec((1, N_HEADS_B, HEAD_DIM, tm), lambda b, i: (b, 0, 0, i)),
                  full(wo.shape), full(wple.shape), full(wgate.shape)],
        out_specs=tok(D),
        out_shape=jax.ShapeDtypeStruct((B, S, D), F32),
        compiler_params=pltpu.CompilerParams(
            dimension_semantics=("arbitrary", "arbitrary"), vmem_limit_bytes=VMEM_LIMIT),
        name="out_proj",
    )(x, p, ya, ga, gb, ybt, wo, wple, wgate)


def _rope_tables(S):
    n = HEAD_DIM // 4
    inv = ROPE_THETA ** (-jnp.arange(n, dtype=F32) / n)
    pos = jnp.arange(S, dtype=jnp.int32)
    row_id = (pos // GRID_W).astype(F32)
    col_id = (pos % GRID_W).astype(F32)
    ang = jnp.concatenate([inv[:, None] * row_id[None, :], inv[:, None] * col_id[None, :]], axis=0)
    return jnp.cos(ang), jnp.sin(ang)


def kernel(x, p, ln_g, w_in, qn_a, kn_a, qn_b, kn_b, w_out, w_ple, w_pgate, rel_bias):
    B, S, D = x.shape
    depth = w_in.shape[0]
    cos_t, sin_t = _rope_tables(S)
    bias = _bias_call(rel_bias)
    perm = np.concatenate([np.arange(0, HEAD_DIM, 2), np.arange(1, HEAD_DIM, 2)])
    bcast = lambda g: jnp.broadcast_to(g.astype(F32)[:, None], (HEAD_DIM, ROW_TILE))
    for i in range(depth):
        w = w_in[i]
        c = 0
        wqa = w[:, c:c + D_A]; c += D_A
        wka = w[:, c:c + D_A]; c += D_A
        wva = w[:, c:c + D_A]; c += D_A
        wga = w[:, c:c + D_A]; c += D_A
        wqb = w[:, c:c + D_B]; c += D_B
        wkb = w[:, c:c + D_KV_B]; c += D_KV_B
        wvb = w[:, c:c + D_KV_B]; c += D_KV_B
        wgb = w[:, c:c + D_B]
        wqb = wqb.reshape(D, N_HEADS_B, HEAD_DIM)[:, :, perm].reshape(D, D_B)
        wkb = wkb.reshape(D, N_KV_B, HEAD_DIM)[:, :, perm].reshape(D, D_KV_B)
        wtok = jnp.concatenate([wva, wga, wgb], axis=1).astype(BF16)
        wfeat = jnp.concatenate([wqa, wka, wqb, wkb, wvb], axis=1).T.astype(BF16)

        outs = _proj_call(
            x, ln_g[i][None, :], wtok, wfeat, bcast(qn_a[i]), bcast(kn_a[i]),
            bcast(qn_b[i][perm]), bcast(kn_b[i][perm]), cos_t, sin_t)
        ga, gb = outs[0], outs[1]
        qs, ks, vs = outs[2:5], outs[5:8], outs[8:11]
        qbt, kb, vbt = outs[11:14]

        ybt = _gqa_call(qbt, kb, vbt)
        ya = _band_call(list(zip(qs, ks, vs)), bias)
        x = _out_call(x, p[i], ya, ga, gb, ybt, w_out[i].astype(BF16), w_ple[i].astype(BF16),
                      w_pgate[i].astype(BF16))
    return x
```

```python
import functools
import math

import numpy as np
import jax
import jax.numpy as jnp
from jax import lax
from jax.experimental import pallas as pl
from jax.experimental.pallas import tpu as pltpu

HEAD_DIM = 64
N_HEADS_A = 8
N_HEADS_B = 8
N_KV_B = 2
GQA_GROUP = N_HEADS_B // N_KV_B
D_A = N_HEADS_A * HEAD_DIM
D_B = N_HEADS_B * HEAD_DIM
D_KV_B = N_KV_B * HEAD_DIM
DILATIONS = (1, 4, 16)
BAND_BLOCK = 128
BAND_HALF = 64
BAND_SPAN = BAND_BLOCK + 2 * BAND_HALF
SUPER_BLOCK = BAND_BLOCK * max(DILATIONS)
LANES = 128
N_PAIRS = D_A // LANES
GRID_W = 64
ROPE_THETA = 10000.0
N_BUCKETS = 32
MAX_DISTANCE = 1024
EPS = 1e-6
NEG_INF = -1e30

ROW_TILE = 512
PROJ_SUBTILES = 2
Q_TILE = 512
KEY_STEP = 256
LOOKAHEAD = 2
VMEM_LIMIT = 56 * 1024 * 1024
SAFE_EXPONENT = 60.0
SHIFT_TERMS = 3
SHIFT_ROWS = 16

F32 = jnp.float32
BF16 = jnp.bfloat16


def _nt_dot(a, b):
    return lax.dot_general(a, b, (((1,), (1,)), ((), ())), preferred_element_type=F32)


def _dot(a, b):
    return jnp.dot(a, b, preferred_element_type=F32)


def _silu(g):
    return g * jax.nn.sigmoid(g)


def _head_rms(t, gain):
    ms = jnp.mean(t * t, axis=1, keepdims=True)
    return t * lax.rsqrt(ms + EPS) * gain[None]


def _rope_halves(t, cos, sin):
    half = HEAD_DIM // 2
    x1 = t[:, :half, :]
    x2 = t[:, half:, :]
    c = cos[None]
    s = sin[None]
    return jnp.concatenate([x1 * c - x2 * s, x1 * s + x2 * c], axis=1)


def _emit_layouts(y, row0, scr, o1, o4, o16):
    t = y.shape[0]
    o1[0, row0:row0 + t, :] = y.astype(BF16)
    for pr in range(N_PAIRS):
        scr[pr] = y[:, pr * LANES:(pr + 1) * LANES]
    for pr in range(N_PAIRS):
        for dil, out in ((4, o4), (16, o16)):
            for r in range(dil):
                lo = (pr * dil + r) * LANES
                rows = slice(row0 // dil, (row0 + t) // dil)
                out[0, rows, lo:lo + LANES] = scr[pr, pl.ds(r, t // dil, stride=dil), :].astype(BF16)


def _proj_kernel(x_ref, g_ref, wtok_ref, wfeat_ref, gqa_ref, gka_ref, gqb_ref, gkb_ref, cos_ref, sin_ref,
                 ga_ref, gbt_ref, q1_ref, q4_ref, q16_ref, k1_ref, k4_ref, k16_ref, v1_ref, v4_ref, v16_ref,
                 qbt_ref, kb_ref, vbt_ref, *scratch):
    for sub in range(PROJ_SUBTILES):
        _proj_subtile(sub, x_ref, g_ref, wtok_ref, wfeat_ref, gqa_ref, gka_ref, gqb_ref, gkb_ref, cos_ref, sin_ref,
                      ga_ref, gbt_ref, (q1_ref, q4_ref, q16_ref), (k1_ref, k4_ref, k16_ref), (v1_ref, v4_ref, v16_ref),
                      qbt_ref, kb_ref, vbt_ref, scratch[3 * sub:3 * sub + 3])


def _proj_subtile(sub, x_ref, g_ref, wtok_ref, wfeat_ref, gqa_ref, gka_ref, gqb_ref, gkb_ref, cos_ref, sin_ref,
                  ga_ref, gbt_ref, q_refs, k_refs, v_refs, qbt_ref, kb_ref, vbt_ref, scratch):
    sq_ref, sk_ref, sv_ref = scratch
    t = x_ref.shape[1] // PROJ_SUBTILES
    row0 = sub * t
    rows = slice(row0, row0 + t)
    scale = HEAD_DIM ** -0.5
    x = x_ref[0, rows, :]
    ms = jnp.mean(x * x, axis=-1, keepdims=True)
    h = (x * lax.rsqrt(ms + EPS) * g_ref[...]).astype(BF16)

    qa = _nt_dot(wfeat_ref[:D_A, :], h).reshape(N_HEADS_A, HEAD_DIM, t)
    qa = _head_rms(qa, gqa_ref[...]) * scale
    _emit_layouts(qa.reshape(D_A, t).T, row0, sq_ref, *q_refs)
    ka = _nt_dot(wfeat_ref[D_A:2 * D_A, :], h).reshape(N_HEADS_A, HEAD_DIM, t)
    ka = _head_rms(ka, gka_ref[...])
    _emit_layouts(ka.reshape(D_A, t).T, row0, sk_ref, *k_refs)
    _emit_layouts(_dot(h, wtok_ref[:, :D_A]), row0, sv_ref, *v_refs)
    ga_ref[0, rows, :] = _dot(h, wtok_ref[:, D_A:]).astype(BF16)

    o = 2 * D_A
    featb = _nt_dot(wfeat_ref[o:, :], h)
    gbt_ref[0, :, :, rows] = featb[D_B + 2 * D_KV_B:].reshape(N_HEADS_B, HEAD_DIM, t).astype(BF16)
    cos = cos_ref[:, rows]
    sin = sin_ref[:, rows]
    qb = _head_rms(featb[:D_B].reshape(N_HEADS_B, HEAD_DIM, t), gqb_ref[...])
    qbt_ref[0, :, :, rows] = (_rope_halves(qb, cos, sin) * scale).astype(BF16)
    kb = _head_rms(featb[D_B:D_B + D_KV_B].reshape(N_KV_B, HEAD_DIM, t), gkb_ref[...])
    kb = _rope_halves(kb, cos, sin)
    ext = (lax.broadcasted_iota(jnp.int32, (HEAD_DIM, t), 0) < SHIFT_TERMS).astype(F32)
    for kv in range(N_KV_B):
        kb_ref[0, kv, rows, :] = jnp.concatenate([kb[kv], ext], axis=0).T.astype(BF16)
    vbt_ref[0, :, 0, :, rows] = featb[D_B + D_KV_B:D_B + 2 * D_KV_B].reshape(N_KV_B, HEAD_DIM, t).astype(BF16)


def _proj_call(x, ln_g, wtok, wfeat, gqa, gka, gqb, gkb, cos_t, sin_t):
    B, S, D = x.shape
    tm = ROW_TILE
    nt = S // tm
    full = lambda shape: pl.BlockSpec(shape, lambda b, i: (0,) * len(shape))
    tok_spec = pl.BlockSpec((1, tm, D_A), lambda b, i: (b, i, 0))
    lay_shapes, lay_specs = [], []
    for _ in range(3):
        for dil in DILATIONS:
            lay_shapes.append(jax.ShapeDtypeStruct((B, S // dil, dil * D_A), BF16))
            lay_specs.append(pl.BlockSpec((1, tm // dil, dil * D_A), lambda b, i: (b, i, 0)))
    out_shape = (
        jax.ShapeDtypeStruct((B, S, D_A), BF16),
        jax.ShapeDtypeStruct((B, N_HEADS_B, HEAD_DIM, S), BF16),
        *lay_shapes,
        jax.ShapeDtypeStruct((B, N_HEADS_B, HEAD_DIM, S), BF16),
        jax.ShapeDtypeStruct((B, N_KV_B, S, 2 * HEAD_DIM), BF16),
        jax.ShapeDtypeStruct((B, N_KV_B, nt, HEAD_DIM, tm), BF16),
    )
    out_specs = (
        tok_spec, pl.BlockSpec((1, N_HEADS_B, HEAD_DIM, tm), lambda b, i: (b, 0, 0, i)),
        *lay_specs,
        pl.BlockSpec((1, N_HEADS_B, HEAD_DIM, tm), lambda b, i: (b, 0, 0, i)),
        pl.BlockSpec((1, N_KV_B, tm, 2 * HEAD_DIM), lambda b, i: (b, 0, i, 0)),
        pl.BlockSpec((1, N_KV_B, 1, HEAD_DIM, tm), lambda b, i: (b, 0, i, 0, 0)),
    )
    in_specs = [
        pl.BlockSpec((1, tm, D), lambda b, i: (b, i, 0)),
        full((1, D)),
        full(wtok.shape),
        full(wfeat.shape),
        full(gqa.shape), full(gka.shape), full(gqb.shape), full(gkb.shape),
        pl.BlockSpec((HEAD_DIM // 2, tm), lambda b, i: (0, i)),
        pl.BlockSpec((HEAD_DIM // 2, tm), lambda b, i: (0, i)),
    ]
    return pl.pallas_call(
        _proj_kernel,
        grid=(B, nt),
        in_specs=in_specs,
        out_specs=out_specs,
        out_shape=out_shape,
        scratch_shapes=[pltpu.VMEM((N_PAIRS, tm // PROJ_SUBTILES, LANES), F32)] * (3 * PROJ_SUBTILES),
        compiler_params=pltpu.CompilerParams(
            dimension_semantics=("arbitrary", "arbitrary"), vmem_limit_bytes=VMEM_LIMIT),
        name="proj",
    )(x, ln_g, wtok, wfeat, gqa, gka, gqb, gkb, cos_t, sin_t)


def _gqa_kernel(*refs, mode):
    if mode == "attn":
        qt_ref, k_ref, vt_ref, gt_ref, o_ref = refs
    elif mode == "attn_shift":
        qt_ref, k_ref, vt_ref, gt_ref, c_ref, o_ref = refs
    else:
        qt_ref, k_ref, o_ref = refs
    tq = qt_ref.shape[3]
    n_steps = k_ref.shape[2] // KEY_STEP
    sub = 8
    qext = []
    for g in range(GQA_GROUP):
        q = qt_ref[0, g]
        if mode == "attn_shift":
            pad = jnp.zeros((HEAD_DIM - SHIFT_ROWS, tq), BF16)
            qext.append(jnp.concatenate([q, c_ref[0, g], pad], axis=0))
        else:
            qext.append(jnp.concatenate([q, jnp.zeros_like(q)], axis=0))
    units = [(g, t) for g in range(GQA_GROUP) for t in range(n_steps)]

    def scores(u):
        g, t = u
        return _dot(k_ref[0, 0, t * KEY_STEP:(t + 1) * KEY_STEP, :], qext[g])

    pending = [scores(u) for u in units[:LOOKAHEAD]]
    for i, (g, t) in enumerate(units):
        if i + LOOKAHEAD < len(units):
            pending.append(scores(units[i + LOOKAHEAD]))
        s = pending[i]
        pending[i] = None
        if mode == "max":
            part = s.reshape(KEY_STEP // sub, sub, tq).max(axis=0)
            mx = part if t == 0 else jnp.maximum(mx, part)
            if t == n_steps - 1:
                o_ref[0, g] = jnp.broadcast_to(mx.max(axis=0, keepdims=True), (sub, tq))
            continue
        if t == 0:
            acc = jnp.zeros((HEAD_DIM, tq), F32)
            den = jnp.zeros((sub, tq), F32)
        p = jnp.exp(s)
        den = den + p.reshape(KEY_STEP // sub, sub, tq).sum(axis=0)
        tk = vt_ref.shape[4]
        lo = (t * KEY_STEP) % tk
        acc = acc + _dot(vt_ref[0, 0, (t * KEY_STEP) // tk, :, lo:lo + KEY_STEP], p.astype(BF16))
        if t == n_steps - 1:
            y = acc / den.sum(axis=0, keepdims=True)
            o_ref[0, g] = (y * _silu(gt_ref[0, g].astype(F32))).astype(BF16)


def _gqa_call(qbt, kb, vbt, gbt, shift=None, mode="attn"):
    B, H, _, S = qbt.shape
    tq = Q_TILE
    nt, rows, tk = vbt.shape[2], vbt.shape[3], vbt.shape[4]
    head_blk = lambda r: pl.BlockSpec((1, GQA_GROUP, r, tq), lambda b, kv, i: (b, kv, 0, i))
    in_specs = [head_blk(HEAD_DIM), pl.BlockSpec((1, 1, S, 2 * HEAD_DIM), lambda b, kv, i: (b, kv, 0, 0))]
    args = [qbt, kb]
    if mode != "max":
        in_specs += [pl.BlockSpec((1, 1, nt, rows, tk), lambda b, kv, i: (b, kv, 0, 0, 0)), head_blk(HEAD_DIM)]
        args += [vbt, gbt]
    if mode == "attn_shift":
        in_specs.append(head_blk(SHIFT_ROWS))
        args.append(shift)
    out_rows, out_dtype = (8, F32) if mode == "max" else (HEAD_DIM, BF16)
    return pl.pallas_call(
        functools.partial(_gqa_kernel, mode=mode),
        grid=(B, N_KV_B, S // tq),
        in_specs=in_specs,
        out_specs=head_blk(out_rows),
        out_shape=jax.ShapeDtypeStruct((B, H, out_rows, S), out_dtype),
        compiler_params=pltpu.CompilerParams(
            dimension_semantics=("arbitrary", "arbitrary", "arbitrary"), vmem_limit_bytes=VMEM_LIMIT),
        name="gqa_" + mode,
    )(*args)


def _shift_rows(m):
    rest = m[:, :, :1, :]
    terms = []
    for _ in range(SHIFT_TERMS):
        t = rest.astype(BF16)
        terms.append(-t)
        rest = rest - t.astype(F32)
    pad = jnp.zeros(m.shape[:2] + (SHIFT_ROWS - SHIFT_TERMS, m.shape[3]), BF16)
    return jnp.concatenate(terms + [pad], axis=2)


def _band_bucket_index():
    nb = N_BUCKETS // 2
    max_exact = nb // 2
    rel = np.arange(BAND_SPAN)[None, :] - BAND_HALF - np.arange(BAND_BLOCK)[:, None]
    out = []
    for dilation in DILATIONS:
        dist = rel * dilation
        ret = np.where(dist > 0, nb, 0)
        n = np.abs(dist)
        nf = np.maximum(n, max_exact).astype(np.float32)
        large = max_exact + (np.log(nf / np.float32(max_exact)) / np.float32(math.log(MAX_DISTANCE / max_exact))
                             * np.float32(nb - max_exact)).astype(np.int32)
        large = np.minimum(large, nb - 1)
        bucket = ret + np.where(n < max_exact, n, large)
        out.append(np.where(np.abs(rel) <= BAND_HALF, bucket, -1))
    return np.stack(out).astype(np.int32)


def _bias_kernel(table_ref, idx_ref, o_ref):
    chunk = 16
    for pat in range(len(DILATIONS)):
        for lo in range(0, BAND_BLOCK, chunk):
            idx = idx_ref[pat, lo:lo + chunk, :]
            acc = [jnp.full(idx.shape, NEG_INF, F32) for _ in range(N_HEADS_A)]
            for b in range(N_BUCKETS):
                hit = idx == b
                for h in range(N_HEADS_A):
                    acc[h] = jnp.where(hit, table_ref[b, h], acc[h])
            for h in range(N_HEADS_A):
                o_ref[pat, h, lo:lo + chunk, :] = acc[h]


def _bias_call(rel_bias):
    idx = jnp.asarray(_band_bucket_index())
    n_pat = len(DILATIONS)
    return pl.pallas_call(
        _bias_kernel,
        in_specs=[pl.BlockSpec(memory_space=pltpu.SMEM), pl.BlockSpec(memory_space=pltpu.VMEM)],
        out_specs=pl.BlockSpec(memory_space=pltpu.VMEM),
        out_shape=jax.ShapeDtypeStruct((n_pat, N_HEADS_A, BAND_BLOCK, BAND_SPAN), F32),
        name="bias_bands",
    )(rel_bias, idx)


def _band_kernel(*refs, mode):
    n_in = 7 * len(DILATIONS)
    ins, bias_ref = refs[:n_in], refs[n_in]
    g_ref = refs[n_in + 1] if mode != "max" else None
    c_ref = refs[n_in + 2] if mode == "attn_shift" else None
    o_ref, num_ref, den_ref = refs[-3:]
    n = pl.program_id(1)
    last = pl.num_programs(1) - 1
    col = lax.broadcasted_iota(jnp.int32, (1, BAND_SPAN), 1)
    edge_lo = jnp.where((n == 0) & (col < BAND_HALF), NEG_INF, 0.0).astype(F32)
    edge_hi = jnp.where((n == last) & (col >= BAND_BLOCK + BAND_HALF), NEG_INF, 0.0).astype(F32)
    lane = lax.broadcasted_iota(jnp.int32, (BAND_BLOCK, LANES), 1)
    first_half = lane < HEAD_DIM
    ones = jnp.ones((BAND_SPAN, LANES), BF16)

    def band(main, prev, nxt, blk, n_blk, ls):
        lo = blk * BAND_BLOCK - BAND_HALF
        parts = []
        if blk == 0:
            parts.append(prev[0, :, ls])
        start = max(lo, 0)
        stop = min(lo + BAND_SPAN, n_blk * BAND_BLOCK)
        parts.append(main[0, start:stop, ls])
        if blk == n_blk - 1:
            parts.append(nxt[0, :, ls])
        return parts[0] if len(parts) == 1 else jnp.concatenate(parts, axis=0)

    units = []
    for pat in reversed(range(len(DILATIONS))):
        dil = DILATIONS[pat]
        q_ref, k_ref, kp_ref, kn_ref, v_ref, vp_ref, vn_ref = ins[7 * pat:7 * pat + 7]
        n_blk = SUPER_BLOCK // (dil * BAND_BLOCK)
        for r in range(dil):
            for blk in range(n_blk):
                units.append((pat, dil, r, blk, n_blk, q_ref, k_ref, kp_ref, kn_ref, v_ref, vp_ref, vn_ref))

    def stage_scores(u):
        pat, dil, r, blk, n_blk, q_ref, k_ref, kp_ref, kn_ref = u[:9]
        ls = slice(r * LANES, (r + 1) * LANES)
        q = q_ref[0, blk * BAND_BLOCK:(blk + 1) * BAND_BLOCK, ls]
        kband = band(k_ref, kp_ref, kn_ref, blk, n_blk, ls)
        zero = jnp.zeros_like(q)
        return [_nt_dot(jnp.where(first_half, q, zero), kband), _nt_dot(jnp.where(first_half, zero, q), kband)]

    def stage_values(u, s_pair):
        pat, dil, r, blk, n_blk = u[:5]
        v_ref, vp_ref, vn_ref = u[9:12]
        ls = slice(r * LANES, (r + 1) * LANES)
        if dil == 1:
            rows = slice(blk * BAND_BLOCK, (blk + 1) * BAND_BLOCK)
        else:
            rows = pl.ds(dil * blk * BAND_BLOCK + r, BAND_BLOCK, stride=dil)
        def biased(half):
            s = s_pair[half] + bias_ref[pat, half]
            if blk == 0:
                s = s + edge_lo
            if blk == n_blk - 1:
                s = s + edge_hi
            return s

        if mode == "max":
            mx = jnp.where(first_half, biased(0).max(axis=-1, keepdims=True), biased(1).max(axis=-1, keepdims=True))
            if dil == DILATIONS[-1]:
                num_ref[rows, :] = mx
            elif dil > 1:
                num_ref[rows, :] = jnp.maximum(num_ref[rows, :], mx)
            else:
                o_ref[0, rows, :] = jnp.maximum(num_ref[rows, :], mx)
            return
        vext = jnp.concatenate([band(v_ref, vp_ref, vn_ref, blk, n_blk, ls), ones], axis=1)
        outs = []
        for half in range(2):
            s = biased(half)
            if mode == "attn_shift":
                s = s - c_ref[0, rows, :][:, half * HEAD_DIM:half * HEAD_DIM + 1]
            outs.append(_dot(jnp.exp(s).astype(BF16), vext))
        num = jnp.where(first_half, outs[0][:, :LANES], outs[1][:, :LANES])
        den = jnp.where(first_half, outs[0][:, LANES:], outs[1][:, LANES:])
        if dil == DILATIONS[-1]:
            num_ref[rows, :] = num
            den_ref[rows, :] = den
        elif dil > 1:
            num_ref[rows, :] = num_ref[rows, :] + num
            den_ref[rows, :] = den_ref[rows, :] + den
        else:
            y = (num_ref[rows, :] + num) / (den_ref[rows, :] + den)
            o_ref[0, rows, :] = (y * _silu(g_ref[0, rows, :].astype(F32))).astype(BF16)

    pending = [stage_scores(u) for u in units[:LOOKAHEAD]]
    for t, u in enumerate(units):
        if t + LOOKAHEAD < len(units):
            pending.append(stage_scores(units[t + LOOKAHEAD]))
        stage_values(u, pending[t])
        pending[t] = None


def _band_call(layouts, bias, gate, shift=None, mode="attn"):
    B, S, _ = layouts[0][0].shape
    n_sb = S // SUPER_BLOCK
    in_specs, args = [], []
    for (q, k, v), dil in zip(layouts, DILATIONS):
        rows = SUPER_BLOCK // dil
        width = dil * LANES
        n_halo = (S // dil) // BAND_HALF
        per_sb = rows // BAND_HALF
        main = pl.BlockSpec((1, rows, width), lambda b, n, pr: (b, n, pr))
        prev = pl.BlockSpec((1, BAND_HALF, width),
                            lambda b, n, pr, per_sb=per_sb: (b, jnp.maximum(n * per_sb - 1, 0), pr))
        nxt = pl.BlockSpec((1, BAND_HALF, width),
                           lambda b, n, pr, per_sb=per_sb, n_halo=n_halo: (b, jnp.minimum((n + 1) * per_sb, n_halo - 1), pr))
        in_specs += [main, main, prev, nxt, main, prev, nxt]
        args += [q, k, k, k, v, v, v]
    in_specs.append(pl.BlockSpec((len(DILATIONS), 2, BAND_BLOCK, BAND_SPAN), lambda b, n, pr: (0, pr, 0, 0)))
    args.append(bias)
    tok_blk = pl.BlockSpec((1, SUPER_BLOCK, LANES), lambda b, n, pr: (b, n, pr))
    if mode != "max":
        in_specs.append(tok_blk)
        args.append(gate)
    if mode == "attn_shift":
        in_specs.append(tok_blk)
        args.append(shift)
    return pl.pallas_call(
        functools.partial(_band_kernel, mode=mode),
        grid=(B, n_sb, N_PAIRS),
        in_specs=in_specs,
        out_specs=tok_blk,
        out_shape=jax.ShapeDtypeStruct((B, S, D_A), F32 if mode == "max" else BF16),
        scratch_shapes=[pltpu.VMEM((SUPER_BLOCK, LANES), F32)] * 2,
        compiler_params=pltpu.CompilerParams(
            dimension_semantics=("arbitrary", "arbitrary", "arbitrary"), vmem_limit_bytes=VMEM_LIMIT),
        name="band_" + mode,
    )(*args)


def _out_kernel(x_ref, p_ref, ya_ref, ybt_ref, wo_ref, wple_ref, wgate_ref, o_ref):
    tm = x_ref.shape[1]
    yb = ybt_ref[0].reshape(D_B, tm).astype(F32).T.astype(BF16)
    x1 = x_ref[0] + _dot(ya_ref[0], wo_ref[:D_A, :]) + _dot(yb, wo_ref[D_A:, :])
    gate = jax.nn.sigmoid(_dot(x1.astype(BF16), wgate_ref[...]))
    ple = _dot(p_ref[0].astype(BF16), wple_ref[...])
    o_ref[0] = x1 + ple * gate


def _out_call(x, p, ya, ybt, wo, wple, wgate):
    B, S, D = x.shape
    tm = ROW_TILE
    full = lambda shape: pl.BlockSpec(shape, lambda b, i: (0,) * len(shape))
    tok = lambda c: pl.BlockSpec((1, tm, c), lambda b, i: (b, i, 0))
    return pl.pallas_call(
        _out_kernel,
        grid=(B, S // tm),
        in_specs=[tok(D), tok(p.shape[-1]), tok(D_A),
                  pl.BlockSpec((1, N_HEADS_B, HEAD_DIM, tm), lambda b, i: (b, 0, 0, i)),
                  full(wo.shape), full(wple.shape), full(wgate.shape)],
        out_specs=tok(D),
        out_shape=jax.ShapeDtypeStruct((B, S, D), F32),
        compiler_params=pltpu.CompilerParams(
            dimension_semantics=("arbitrary", "arbitrary"), vmem_limit_bytes=VMEM_LIMIT),
        name="out_proj",
    )(x, p, ya, ybt, wo, wple, wgate)


def _rope_tables(S):
    n = HEAD_DIM // 4
    inv = ROPE_THETA ** (-jnp.arange(n, dtype=F32) / n)
    pos = jnp.arange(S, dtype=jnp.int32)
    row_id = (pos // GRID_W).astype(F32)
    col_id = (pos % GRID_W).astype(F32)
    ang = jnp.concatenate([inv[:, None] * row_id[None, :], inv[:, None] * col_id[None, :]], axis=0)
    return jnp.cos(ang), jnp.sin(ang)


def kernel(x, p, ln_g, w_in, qn_a, kn_a, qn_b, kn_b, w_out, w_ple, w_pgate, rel_bias):
    B, S, D = x.shape
    depth = w_in.shape[0]
    cos_t, sin_t = _rope_tables(S)
    bias = _bias_call(rel_bias)
    perm = np.concatenate([np.arange(0, HEAD_DIM, 2), np.arange(1, HEAD_DIM, 2)])
    bcast = lambda g: jnp.broadcast_to(g.astype(F32)[:, None], (HEAD_DIM, ROW_TILE // PROJ_SUBTILES))
    for i in range(depth):
        w = w_in[i]
        c = 0
        wqa = w[:, c:c + D_A]; c += D_A
        wka = w[:, c:c + D_A]; c += D_A
        wva = w[:, c:c + D_A]; c += D_A
        wga = w[:, c:c + D_A]; c += D_A
        wqb = w[:, c:c + D_B]; c += D_B
        wkb = w[:, c:c + D_KV_B]; c += D_KV_B
        wvb = w[:, c:c + D_KV_B]; c += D_KV_B
        wgb = w[:, c:c + D_B]
        wqb = wqb.reshape(D, N_HEADS_B, HEAD_DIM)[:, :, perm].reshape(D, D_B)
        wkb = wkb.reshape(D, N_KV_B, HEAD_DIM)[:, :, perm].reshape(D, D_KV_B)
        wtok = jnp.concatenate([wva, wga], axis=1).astype(BF16)
        wfeat = jnp.concatenate([wqa, wka, wqb, wkb, wvb, wgb], axis=1).T.astype(BF16)

        outs = _proj_call(
            x, ln_g[i][None, :], wtok, wfeat, bcast(qn_a[i]), bcast(kn_a[i]),
            bcast(qn_b[i][perm]), bcast(kn_b[i][perm]), cos_t, sin_t)
        ga, gbt = outs[0], outs[1]
        qs, ks, vs = outs[2:5], outs[5:8], outs[8:11]
        qbt, kb, vbt = outs[11:14]

        amax = lambda g: jnp.max(jnp.abs(g.astype(F32)))
        root = HEAD_DIM ** 0.5
        bound_b = root * amax(qn_b[i]) * amax(kn_b[i])
        bound_a = root * amax(qn_a[i]) * amax(kn_a[i]) + amax(rel_bias)
        safe = jnp.maximum(bound_a, bound_b) <= SAFE_EXPONENT
        layouts = tuple(zip(qs, ks, vs))

        def unshifted(ops):
            qbt, kb, vbt, gbt, layouts, bias, ga = ops
            return _gqa_call(qbt, kb, vbt, gbt), _band_call(layouts, bias, ga)

        def shifted(ops):
            qbt, kb, vbt, gbt, layouts, bias, ga = ops
            shift_b = _shift_rows(_gqa_call(qbt, kb, vbt, gbt, mode="max"))
            shift_a = _band_call(layouts, bias, ga, mode="max")
            return (_gqa_call(qbt, kb, vbt, gbt, shift_b, mode="attn_shift"),
                    _band_call(layouts, bias, ga, shift_a, mode="attn_shift"))

        ybt, ya = lax.cond(safe, unshifted, shifted, (qbt, kb, vbt, gbt, layouts, bias, ga))
        x = _out_call(x, p[i], ya, ybt, w_out[i].astype(BF16), w_ple[i].astype(BF16), w_pgate[i].astype(BF16))
    return x
```

```python
import functools
import math

import numpy as np
import jax
import jax.numpy as jnp
from jax import lax
from jax.experimental import pallas as pl
from jax.experimental.pallas import tpu as pltpu

HEAD_DIM = 64
N_HEADS_A = 8
N_HEADS_B = 8
N_KV_B = 2
GQA_GROUP = N_HEADS_B // N_KV_B
D_A = N_HEADS_A * HEAD_DIM
D_B = N_HEADS_B * HEAD_DIM
D_KV_B = N_KV_B * HEAD_DIM
DILATIONS = (1, 4, 16)
BAND_BLOCK = 128
BAND_HALF = 64
BAND_SPAN = BAND_BLOCK + 2 * BAND_HALF
SUPER_BLOCK = BAND_BLOCK * max(DILATIONS)
LANES = 128
N_PAIRS = D_A // LANES
GRID_W = 64
ROPE_THETA = 10000.0
N_BUCKETS = 32
MAX_DISTANCE = 1024
EPS = 1e-6
NEG_INF = -1e30

ROW_TILE = 512
PROJ_SUBTILES = 2
Q_TILE = 512
Q_TILES_PER_STEP = 2
KEY_STEP = 256
LOOKAHEAD = 2
VMEM_LIMIT = 56 * 1024 * 1024
SAFE_EXPONENT = 60.0
SHIFT_TERMS = 3
SHIFT_ROWS = 16

F32 = jnp.float32
BF16 = jnp.bfloat16


def _nt_dot(a, b):
    return lax.dot_general(a, b, (((1,), (1,)), ((), ())), preferred_element_type=F32)


def _dot(a, b):
    return jnp.dot(a, b, preferred_element_type=F32)


def _silu(g):
    return g * jax.nn.sigmoid(g)


def _head_rms(t, gain):
    ms = jnp.mean(t * t, axis=1, keepdims=True)
    return t * lax.rsqrt(ms + EPS) * gain[None]


def _rope_halves(t, cos, sin):
    half = HEAD_DIM // 2
    x1 = t[:, :half, :]
    x2 = t[:, half:, :]
    c = cos[None]
    s = sin[None]
    return jnp.concatenate([x1 * c - x2 * s, x1 * s + x2 * c], axis=1)


def _zero_from(tile, width):
    bits = pltpu.bitcast(tile, jnp.uint32)
    zero = lax.shift_right_logical(lax.shift_right_logical(bits, jnp.uint32(16)), jnp.uint32(16))
    return jnp.tile(pltpu.bitcast(zero, F32), (1, width // LANES))


def _residue_layout_jobs(src, scr, o4, o16):
    t = src.shape[1]

    def fill(pr):
        scr[pr] = src[0, :, pr * LANES:(pr + 1) * LANES].astype(F32)
        return None

    def gather(pr, dil, out, r):
        lo = (pr * dil + r) * LANES
        out[0, :, lo:lo + LANES] = scr[pr, pl.ds(r, t // dil, stride=dil), :].astype(BF16)
        row = pl.multiple_of(jnp.minimum(pl.program_id(0), 0), 16)
        return out[0, pl.ds(row, 16), lo:lo + LANES]

    jobs = [functools.partial(fill, pr) for pr in range(N_PAIRS)]
    for pr in range(N_PAIRS):
        for dil, out in ((4, o4), (16, o16)):
            jobs += [functools.partial(gather, pr, dil, out, r) for r in range(dil)]
    return jobs


def _proj_kernel(x_ref, g_ref, wtok_ref, wfeat_ref, gqa_ref, gka_ref, gqb_ref, gkb_ref, cos_ref, sin_ref,
                 ga_ref, gbt_ref, q1_ref, k1_ref, v1_ref, qbt_ref, kb_ref, vbt_ref):
    for sub in range(PROJ_SUBTILES):
        _proj_subtile(sub, x_ref, g_ref, wtok_ref, wfeat_ref, gqa_ref, gka_ref, gqb_ref, gkb_ref, cos_ref, sin_ref,
                      ga_ref, gbt_ref, q1_ref, k1_ref, v1_ref, qbt_ref, kb_ref, vbt_ref)


def _proj_subtile(sub, x_ref, g_ref, wtok_ref, wfeat_ref, gqa_ref, gka_ref, gqb_ref, gkb_ref, cos_ref, sin_ref,
                  ga_ref, gbt_ref, q1_ref, k1_ref, v1_ref, qbt_ref, kb_ref, vbt_ref):
    t = x_ref.shape[1] // PROJ_SUBTILES
    row0 = sub * t
    rows = slice(row0, row0 + t)
    scale = HEAD_DIM ** -0.5
    x = x_ref[0, rows, :]
    ms = jnp.mean(x * x, axis=-1, keepdims=True)
    h = (x * lax.rsqrt(ms + EPS) * g_ref[...]).astype(BF16)

    qa = _nt_dot(wfeat_ref[:D_A, :], h).reshape(N_HEADS_A, HEAD_DIM, t)
    qa = _head_rms(qa, gqa_ref[...]) * scale
    q1_ref[0, rows, :] = qa.reshape(D_A, t).T.astype(BF16)
    ka = _nt_dot(wfeat_ref[D_A:2 * D_A, :], h).reshape(N_HEADS_A, HEAD_DIM, t)
    ka = _head_rms(ka, gka_ref[...])
    k1_ref[0, rows, :] = ka.reshape(D_A, t).T.astype(BF16)
    v1_ref[0, rows, :] = _dot(h, wtok_ref[:, :D_A]).astype(BF16)
    ga_ref[0, rows, :] = _dot(h, wtok_ref[:, D_A:]).astype(BF16)

    o = 2 * D_A
    featb = _nt_dot(wfeat_ref[o:, :], h)
    gbt_ref[0, :, :, rows] = featb[D_B + 2 * D_KV_B:].reshape(N_HEADS_B, HEAD_DIM, t).astype(BF16)
    cos = cos_ref[:, rows]
    sin = sin_ref[:, rows]
    qb = _head_rms(featb[:D_B].reshape(N_HEADS_B, HEAD_DIM, t), gqb_ref[...])
    qbt_ref[0, :, :, rows] = (_rope_halves(qb, cos, sin) * scale).astype(BF16)
    kb = _head_rms(featb[D_B:D_B + D_KV_B].reshape(N_KV_B, HEAD_DIM, t), gkb_ref[...])
    kb = _rope_halves(kb, cos, sin)
    ext = (lax.broadcasted_iota(jnp.int32, (HEAD_DIM, t), 0) < SHIFT_TERMS).astype(F32)
    for kv in range(N_KV_B):
        kb_ref[0, kv, rows, :] = jnp.concatenate([kb[kv], ext], axis=0).T.astype(BF16)
    vbt_ref[0, :, 0, :, rows] = featb[D_B + D_KV_B:D_B + 2 * D_KV_B].reshape(N_KV_B, HEAD_DIM, t).astype(BF16)


def _proj_call(x, ln_g, wtok, wfeat, gqa, gka, gqb, gkb, cos_t, sin_t):
    B, S, D = x.shape
    tm = ROW_TILE
    nt = S // tm
    full = lambda shape: pl.BlockSpec(shape, lambda b, i: (0,) * len(shape))
    tok_spec = pl.BlockSpec((1, tm, D_A), lambda b, i: (b, i, 0))
    tok_shape = jax.ShapeDtypeStruct((B, S, D_A), BF16)
    out_shape = (
        tok_shape,
        jax.ShapeDtypeStruct((B, N_HEADS_B, HEAD_DIM, S), BF16),
        tok_shape, tok_shape, tok_shape,
        jax.ShapeDtypeStruct((B, N_HEADS_B, HEAD_DIM, S), BF16),
        jax.ShapeDtypeStruct((B, N_KV_B, S, 2 * HEAD_DIM), BF16),
        jax.ShapeDtypeStruct((B, N_KV_B, nt, HEAD_DIM, tm), BF16),
    )
    out_specs = (
        tok_spec, pl.BlockSpec((1, N_HEADS_B, HEAD_DIM, tm), lambda b, i: (b, 0, 0, i)),
        tok_spec, tok_spec, tok_spec,
        pl.BlockSpec((1, N_HEADS_B, HEAD_DIM, tm), lambda b, i: (b, 0, 0, i)),
        pl.BlockSpec((1, N_KV_B, tm, 2 * HEAD_DIM), lambda b, i: (b, 0, i, 0)),
        pl.BlockSpec((1, N_KV_B, 1, HEAD_DIM, tm), lambda b, i: (b, 0, i, 0, 0)),
    )
    in_specs = [
        pl.BlockSpec((1, tm, D), lambda b, i: (b, i, 0)),
        full((1, D)),
        full(wtok.shape),
        full(wfeat.shape),
        full(gqa.shape), full(gka.shape), full(gqb.shape), full(gkb.shape),
        pl.BlockSpec((HEAD_DIM // 2, tm), lambda b, i: (0, i)),
        pl.BlockSpec((HEAD_DIM // 2, tm), lambda b, i: (0, i)),
    ]
    return pl.pallas_call(
        _proj_kernel,
        grid=(B, nt),
        in_specs=in_specs,
        out_specs=out_specs,
        out_shape=out_shape,
        compiler_params=pltpu.CompilerParams(
            dimension_semantics=("arbitrary", "arbitrary"), vmem_limit_bytes=VMEM_LIMIT),
        name="proj",
    )(x, ln_g, wtok, wfeat, gqa, gka, gqb, gkb, cos_t, sin_t)


def _gqa_kernel(*refs, mode):
    if mode == "max":
        qt_ref, k_ref, o_ref = refs
    else:
        n_in = 5 if mode == "attn_shift" else 4
        qt_ref, k_ref, vt_ref, gt_ref = refs[:4]
        c_ref = refs[4] if mode == "attn_shift" else None
        nat_refs, o_ref = refs[n_in:n_in + 3], refs[n_in + 3]
        lay_refs, scr_refs = refs[n_in + 4:n_in + 10], refs[n_in + 10:]
        side_jobs = []
        for j in range(3):
            side_jobs += _residue_layout_jobs(nat_refs[j], scr_refs[j], lay_refs[2 * j], lay_refs[2 * j + 1])
    tq = Q_TILE
    n_steps = k_ref.shape[2] // KEY_STEP
    sub = 8
    heads = [(g, slice(j * tq, (j + 1) * tq)) for j in range(qt_ref.shape[3] // tq) for g in range(GQA_GROUP)]
    qext = {}
    for g, cols in heads:
        q = qt_ref[0, g, :, cols]
        if mode == "attn_shift":
            pad = jnp.zeros((HEAD_DIM - SHIFT_ROWS, tq), BF16)
            qext[g, cols.start] = jnp.concatenate([q, c_ref[0, g, :, cols], pad], axis=0)
        else:
            qext[g, cols.start] = jnp.concatenate([q, jnp.zeros_like(q)], axis=0)
    units = [(g, cols, t) for g, cols in heads for t in range(n_steps)]

    def scores(u):
        g, cols, t = u
        return _dot(k_ref[0, 0, t * KEY_STEP:(t + 1) * KEY_STEP, :], qext[g, cols.start])

    pending = [scores(u) for u in units[:LOOKAHEAD]]
    for i, (g, cols, t) in enumerate(units):
        if i + LOOKAHEAD < len(units):
            pending.append(scores(units[i + LOOKAHEAD]))
        s = pending[i]
        pending[i] = None
        if mode == "max":
            part = s.reshape(KEY_STEP // sub, sub, tq).max(axis=0)
            mx = part if t == 0 else jnp.maximum(mx, part)
            if t == n_steps - 1:
                o_ref[0, g, :, cols] = jnp.broadcast_to(mx.max(axis=0, keepdims=True), (sub, tq))
            continue
        if t == 0:
            acc = jnp.zeros((HEAD_DIM, tq), F32)
            den = jnp.zeros((sub, tq), F32)
        p = jnp.exp(s)
        den = den + p.reshape(KEY_STEP // sub, sub, tq).sum(axis=0)
        if side_jobs:
            back = side_jobs.pop(0)()
            if back is not None:
                den = den + _zero_from(back, tq)
        tk = vt_ref.shape[4]
        lo = (t * KEY_STEP) % tk
        acc = acc + _dot(vt_ref[0, 0, (t * KEY_STEP) // tk, :, lo:lo + KEY_STEP], p.astype(BF16))
        if t == n_steps - 1:
            y = acc / den.sum(axis=0, keepdims=True)
            o_ref[0, g, :, cols] = (y * _silu(gt_ref[0, g, :, cols].astype(F32))).astype(BF16)
    if mode != "max":
        for job in side_jobs:
            job()


def _gqa_call(qbt, kb, vbt, gbt, window_qkv, shift=None, mode="attn"):
    B, H, _, S = qbt.shape
    tq = Q_TILE * min(Q_TILES_PER_STEP, S // Q_TILE)
    n_q = S // tq
    nt, rows, tk = vbt.shape[2], vbt.shape[3], vbt.shape[4]
    head_blk = lambda r: pl.BlockSpec((1, GQA_GROUP, r, tq), lambda b, kv, i: (b, kv, 0, i))
    in_specs = [head_blk(HEAD_DIM), pl.BlockSpec((1, 1, S, 2 * HEAD_DIM), lambda b, kv, i: (b, kv, 0, 0))]
    args = [qbt, kb]
    grid = (B, N_KV_B, n_q)
    params = pltpu.CompilerParams(
        dimension_semantics=("arbitrary", "arbitrary", "arbitrary"), vmem_limit_bytes=VMEM_LIMIT)
    if mode == "max":
        return pl.pallas_call(
            functools.partial(_gqa_kernel, mode=mode), grid=grid, in_specs=in_specs, out_specs=head_blk(8),
            out_shape=jax.ShapeDtypeStruct((B, H, 8, S), F32), compiler_params=params, name="gqa_" + mode,
        )(*args)
    in_specs += [pl.BlockSpec((1, 1, nt, rows, tk), lambda b, kv, i: (b, kv, 0, 0, 0)), head_blk(HEAD_DIM)]
    args += [vbt, gbt]
    if mode == "attn_shift":
        in_specs.append(head_blk(SHIFT_ROWS))
        args.append(shift)
    t = S // (N_KV_B * n_q)
    tile = lambda b, kv, i: (b, kv * n_q + i, 0)
    in_specs += [pl.BlockSpec((1, t, D_A), tile)] * 3
    args += list(window_qkv)
    out_specs, out_shape = [head_blk(HEAD_DIM)], [jax.ShapeDtypeStruct((B, H, HEAD_DIM, S), BF16)]
    for _ in range(3):
        for dil in DILATIONS[1:]:
            out_specs.append(pl.BlockSpec((1, t // dil, dil * D_A), tile))
            out_shape.append(jax.ShapeDtypeStruct((B, S // dil, dil * D_A), BF16))
    outs = pl.pallas_call(
        functools.partial(_gqa_kernel, mode=mode), grid=grid, in_specs=in_specs, out_specs=out_specs,
        out_shape=out_shape, scratch_shapes=[pltpu.VMEM((N_PAIRS, t, LANES), F32)] * 3,
        compiler_params=params, name="gqa_" + mode,
    )(*args)
    return outs[0], outs[1:]


def _shift_rows(m):
    rest = m[:, :, :1, :]
    terms = []
    for _ in range(SHIFT_TERMS):
        t = rest.astype(BF16)
        terms.append(-t)
        rest = rest - t.astype(F32)
    pad = jnp.zeros(m.shape[:2] + (SHIFT_ROWS - SHIFT_TERMS, m.shape[3]), BF16)
    return jnp.concatenate(terms + [pad], axis=2)


def _band_bucket_index():
    nb = N_BUCKETS // 2
    max_exact = nb // 2
    rel = np.arange(BAND_SPAN)[None, :] - BAND_HALF - np.arange(BAND_BLOCK)[:, None]
    out = []
    for dilation in DILATIONS:
        dist = rel * dilation
        ret = np.where(dist > 0, nb, 0)
        n = np.abs(dist)
        nf = np.maximum(n, max_exact).astype(np.float32)
        large = max_exact + (np.log(nf / np.float32(max_exact)) / np.float32(math.log(MAX_DISTANCE / max_exact))
                             * np.float32(nb - max_exact)).astype(np.int32)
        large = np.minimum(large, nb - 1)
        bucket = ret + np.where(n < max_exact, n, large)
        out.append(np.where(np.abs(rel) <= BAND_HALF, bucket, -1))
    return np.stack(out).astype(np.int32)


def _bias_kernel(table_ref, idx_ref, o_ref):
    chunk = 16
    for pat in range(len(DILATIONS)):
        for lo in range(0, BAND_BLOCK, chunk):
            idx = idx_ref[pat, lo:lo + chunk, :]
            acc = [jnp.full(idx.shape, NEG_INF, F32) for _ in range(N_HEADS_A)]
            for b in range(N_BUCKETS):
                hit = idx == b
                for h in range(N_HEADS_A):
                    acc[h] = jnp.where(hit, table_ref[b, h], acc[h])
            for h in range(N_HEADS_A):
                o_ref[pat, h, lo:lo + chunk, :] = acc[h]


def _bias_call(rel_bias):
    idx = jnp.asarray(_band_bucket_index())
    n_pat = len(DILATIONS)
    return pl.pallas_call(
        _bias_kernel,
        in_specs=[pl.BlockSpec(memory_space=pltpu.SMEM), pl.BlockSpec(memory_space=pltpu.VMEM)],
        out_specs=pl.BlockSpec(memory_space=pltpu.VMEM),
        out_shape=jax.ShapeDtypeStruct((n_pat, N_HEADS_A, BAND_BLOCK, BAND_SPAN), F32),
        name="bias_bands",
    )(rel_bias, idx)


def _band_kernel(*refs, mode):
    n_in = 7 * len(DILATIONS)
    ins, bias_ref = refs[:n_in], refs[n_in]
    g_ref = refs[n_in + 1] if mode != "max" else None
    c_ref = refs[n_in + 2] if mode == "attn_shift" else None
    o_ref, num_ref, den_ref = refs[-3:]
    n = pl.program_id(1)
    last = pl.num_programs(1) - 1
    col = lax.broadcasted_iota(jnp.int32, (1, BAND_SPAN), 1)
    edge_lo = jnp.where((n == 0) & (col < BAND_HALF), NEG_INF, 0.0).astype(F32)
    edge_hi = jnp.where((n == last) & (col >= BAND_BLOCK + BAND_HALF), NEG_INF, 0.0).astype(F32)
    lane = lax.broadcasted_iota(jnp.int32, (BAND_BLOCK, LANES), 1)
    first_half = lane < HEAD_DIM
    ones = jnp.ones((BAND_SPAN, LANES), BF16)

    def band(main, prev, nxt, blk, n_blk, ls):
        lo = blk * BAND_BLOCK - BAND_HALF
        parts = []
        if blk == 0:
            parts.append(prev[0, :, ls])
        start = max(lo, 0)
        stop = min(lo + BAND_SPAN, n_blk * BAND_BLOCK)
        parts.append(main[0, start:stop, ls])
        if blk == n_blk - 1:
            parts.append(nxt[0, :, ls])
        return parts[0] if len(parts) == 1 else jnp.concatenate(parts, axis=0)

    units = []
    for pat in reversed(range(len(DILATIONS))):
        dil = DILATIONS[pat]
        q_ref, k_ref, kp_ref, kn_ref, v_ref, vp_ref, vn_ref = ins[7 * pat:7 * pat + 7]
        n_blk = SUPER_BLOCK // (dil * BAND_BLOCK)
        for r in range(dil):
            for blk in range(n_blk):
                units.append((pat, dil, r, blk, n_blk, q_ref, k_ref, kp_ref, kn_ref, v_ref, vp_ref, vn_ref))

    def stage_scores(u):
        pat, dil, r, blk, n_blk, q_ref, k_ref, kp_ref, kn_ref = u[:9]
        ls = slice(r * LANES, (r + 1) * LANES)
        q = q_ref[0, blk * BAND_BLOCK:(blk + 1) * BAND_BLOCK, ls]
        kband = band(k_ref, kp_ref, kn_ref, blk, n_blk, ls)
        zero = jnp.zeros_like(q)
        return [_nt_dot(jnp.where(first_half, q, zero), kband), _nt_dot(jnp.where(first_half, zero, q), kband)]

    def stage_values(u, s_pair):
        pat, dil, r, blk, n_blk = u[:5]
        v_ref, vp_ref, vn_ref = u[9:12]
        ls = slice(r * LANES, (r + 1) * LANES)
        if dil == 1:
            rows = slice(blk * BAND_BLOCK, (blk + 1) * BAND_BLOCK)
        else:
            rows = pl.ds(dil * blk * BAND_BLOCK + r, BAND_BLOCK, stride=dil)
        def biased(half):
            s = s_pair[half] + bias_ref[pat, half]
            if blk == 0:
                s = s + edge_lo
            if blk == n_blk - 1:
                s = s + edge_hi
            return s

        if mode == "max":
            mx = jnp.where(first_half, biased(0).max(axis=-1, keepdims=True), biased(1).max(axis=-1, keepdims=True))
            if dil == DILATIONS[-1]:
                num_ref[rows, :] = mx
            elif dil > 1:
                num_ref[rows, :] = jnp.maximum(num_ref[rows, :], mx)
            else:
                o_ref[0, rows, :] = jnp.maximum(num_ref[rows, :], mx)
            return
        vext = jnp.concatenate([band(v_ref, vp_ref, vn_ref, blk, n_blk, ls), ones], axis=1)
        outs = []
        for half in range(2):
            s = biased(half)
            if mode == "attn_shift":
                s = s - c_ref[0, rows, :][:, half * HEAD_DIM:half * HEAD_DIM + 1]
            outs.append(_dot(jnp.exp(s).astype(BF16), vext))
        num = jnp.where(first_half, outs[0][:, :LANES], outs[1][:, :LANES])
        den = jnp.where(first_half, outs[0][:, LANES:], outs[1][:, LANES:])
        if dil == DILATIONS[-1]:
            num_ref[rows, :] = num
            den_ref[rows, :] = den
        elif dil > 1:
            num_ref[rows, :] = num_ref[rows, :] + num
            den_ref[rows, :] = den_ref[rows, :] + den
        else:
            y = (num_ref[rows, :] + num) / (den_ref[rows, :] + den)
            o_ref[0, rows, :] = (y * _silu(g_ref[0, rows, :].astype(F32))).astype(BF16)

    pending = [stage_scores(u) for u in units[:LOOKAHEAD]]
    for t, u in enumerate(units):
        if t + LOOKAHEAD < len(units):
            pending.append(stage_scores(units[t + LOOKAHEAD]))
        stage_values(u, pending[t])
        pending[t] = None


def _band_call(layouts, bias, gate, shift=None, mode="attn"):
    B, S, _ = layouts[0][0].shape
    n_sb = S // SUPER_BLOCK
    in_specs, args = [], []
    for (q, k, v), dil in zip(layouts, DILATIONS):
        rows = SUPER_BLOCK // dil
        width = dil * LANES
        n_halo = (S // dil) // BAND_HALF
        per_sb = rows // BAND_HALF
        main = pl.BlockSpec((1, rows, width), lambda b, n, pr: (b, n, pr))
        prev = pl.BlockSpec((1, BAND_HALF, width),
                            lambda b, n, pr, per_sb=per_sb: (b, jnp.maximum(n * per_sb - 1, 0), pr))
        nxt = pl.BlockSpec((1, BAND_HALF, width),
                           lambda b, n, pr, per_sb=per_sb, n_halo=n_halo: (b, jnp.minimum((n + 1) * per_sb, n_halo - 1), pr))
        in_specs += [main, main, prev, nxt, main, prev, nxt]
        args += [q, k, k, k, v, v, v]
    in_specs.append(pl.BlockSpec((len(DILATIONS), 2, BAND_BLOCK, BAND_SPAN), lambda b, n, pr: (0, pr, 0, 0)))
    args.append(bias)
    tok_blk = pl.BlockSpec((1, SUPER_BLOCK, LANES), lambda b, n, pr: (b, n, pr))
    if mode != "max":
        in_specs.append(tok_blk)
        args.append(gate)
    if mode == "attn_shift":
        in_specs.append(tok_blk)
        args.append(shift)
    return pl.pallas_call(
        functools.partial(_band_kernel, mode=mode),
        grid=(B, n_sb, N_PAIRS),
        in_specs=in_specs,
        out_specs=tok_blk,
        out_shape=jax.ShapeDtypeStruct((B, S, D_A), F32 if mode == "max" else BF16),
        scratch_shapes=[pltpu.VMEM((SUPER_BLOCK, LANES), F32)] * 2,
        compiler_params=pltpu.CompilerParams(
            dimension_semantics=("arbitrary", "arbitrary", "arbitrary"), vmem_limit_bytes=VMEM_LIMIT),
        name="band_" + mode,
    )(*args)


def _out_kernel(x_ref, p_ref, ya_ref, ybt_ref, wo_ref, wple_ref, wgate_ref, o_ref):
    tm = x_ref.shape[1]
    yb = ybt_ref[0].reshape(D_B, tm).astype(F32).T.astype(BF16)
    x1 = x_ref[0] + _dot(ya_ref[0], wo_ref[:D_A, :]) + _dot(yb, wo_ref[D_A:, :])
    gate = jax.nn.sigmoid(_dot(x1.astype(BF16), wgate_ref[...]))
    ple = _dot(p_ref[0].astype(BF16), wple_ref[...])
    o_ref[0] = x1 + ple * gate


def _out_call(x, p, ya, ybt, wo, wple, wgate):
    B, S, D = x.shape
    tm = ROW_TILE
    full = lambda shape: pl.BlockSpec(shape, lambda b, i: (0,) * len(shape))
    tok = lambda c: pl.BlockSpec((1, tm, c), lambda b, i: (b, i, 0))
    return pl.pallas_call(
        _out_kernel,
        grid=(B, S // tm),
        in_specs=[tok(D), tok(p.shape[-1]), tok(D_A),
                  pl.BlockSpec((1, N_HEADS_B, HEAD_DIM, tm), lambda b, i: (b, 0, 0, i)),
                  full(wo.shape), full(wple.shape), full(wgate.shape)],
        out_specs=tok(D),
        out_shape=jax.ShapeDtypeStruct((B, S, D), F32),
        compiler_params=pltpu.CompilerParams(
            dimension_semantics=("arbitrary", "arbitrary"), vmem_limit_bytes=VMEM_LIMIT),
        name="out_proj",
    )(x, p, ya, ybt, wo, wple, wgate)


def _rope_tables(S):
    n = HEAD_DIM // 4
    inv = ROPE_THETA ** (-jnp.arange(n, dtype=F32) / n)
    pos = jnp.arange(S, dtype=jnp.int32)
    row_id = (pos // GRID_W).astype(F32)
    col_id = (pos % GRID_W).astype(F32)
    ang = jnp.concatenate([inv[:, None] * row_id[None, :], inv[:, None] * col_id[None, :]], axis=0)
    return jnp.cos(ang), jnp.sin(ang)


def kernel(x, p, ln_g, w_in, qn_a, kn_a, qn_b, kn_b, w_out, w_ple, w_pgate, rel_bias):
    B, S, D = x.shape
    depth = w_in.shape[0]
    cos_t, sin_t = _rope_tables(S)
    bias = _bias_call(rel_bias)
    perm = np.concatenate([np.arange(0, HEAD_DIM, 2), np.arange(1, HEAD_DIM, 2)])
    bcast = lambda g: jnp.broadcast_to(g.astype(F32)[:, None], (HEAD_DIM, ROW_TILE // PROJ_SUBTILES))
    for i in range(depth):
        w = w_in[i]
        c = 0
        wqa = w[:, c:c + D_A]; c += D_A
        wka = w[:, c:c + D_A]; c += D_A
        wva = w[:, c:c + D_A]; c += D_A
        wga = w[:, c:c + D_A]; c += D_A
        wqb = w[:, c:c + D_B]; c += D_B
        wkb = w[:, c:c + D_KV_B]; c += D_KV_B
        wvb = w[:, c:c + D_KV_B]; c += D_KV_B
        wgb = w[:, c:c + D_B]
        wqb = wqb.reshape(D, N_HEADS_B, HEAD_DIM)[:, :, perm].reshape(D, D_B)
        wkb = wkb.reshape(D, N_KV_B, HEAD_DIM)[:, :, perm].reshape(D, D_KV_B)
        wtok = jnp.concatenate([wva, wga], axis=1).astype(BF16)
        wfeat = jnp.concatenate([wqa, wka, wqb, wkb, wvb, wgb], axis=1).T.astype(BF16)

        outs = _proj_call(
            x, ln_g[i][None, :], wtok, wfeat, bcast(qn_a[i]), bcast(kn_a[i]),
            bcast(qn_b[i][perm]), bcast(kn_b[i][perm]), cos_t, sin_t)
        ga, gbt, q1, k1, v1, qbt, kb, vbt = outs

        amax = lambda g: jnp.max(jnp.abs(g.astype(F32)))
        root = HEAD_DIM ** 0.5
        bound_b = root * amax(qn_b[i]) * amax(kn_b[i])
        bound_a = root * amax(qn_a[i]) * amax(kn_a[i]) + amax(rel_bias)
        safe = jnp.maximum(bound_a, bound_b) <= SAFE_EXPONENT
        def band_layouts(qkv, lay):
            (q1, k1, v1), (q4, q16, k4, k16, v4, v16) = qkv, lay
            return ((q1, k1, v1), (q4, k4, v4), (q16, k16, v16))

        def unshifted(ops):
            qbt, kb, vbt, gbt, qkv, bias, ga = ops
            ybt, lay = _gqa_call(qbt, kb, vbt, gbt, qkv)
            return ybt, _band_call(band_layouts(qkv, lay), bias, ga)

        def shifted(ops):
            qbt, kb, vbt, gbt, qkv, bias, ga = ops
            shift_b = _shift_rows(_gqa_call(qbt, kb, vbt, gbt, qkv, mode="max"))
            ybt, lay = _gqa_call(qbt, kb, vbt, gbt, qkv, shift_b, mode="attn_shift")
            layouts = band_layouts(qkv, lay)
            shift_a = _band_call(layouts, bias, ga, mode="max")
            return ybt, _band_call(layouts, bias, ga, shift_a, mode="attn_shift")

        ybt, ya = lax.cond(safe, unshifted, shifted, (qbt, kb, vbt, gbt, (q1, k1, v1), bias, ga))
        x = _out_call(x, p[i], ya, ybt, w_out[i].astype(BF16), w_ple[i].astype(BF16), w_pgate[i].astype(BF16))
    return x
```

```python
import functools
import math

import numpy as np
import jax
import jax.numpy as jnp
from jax import lax
from jax.experimental import pallas as pl
from jax.experimental.pallas import tpu as pltpu

HEAD_DIM = 64
N_HEADS_A = 8
N_HEADS_B = 8
N_KV_B = 2
GQA_GROUP = N_HEADS_B // N_KV_B
D_A = N_HEADS_A * HEAD_DIM
D_B = N_HEADS_B * HEAD_DIM
D_KV_B = N_KV_B * HEAD_DIM
DILATIONS = (1, 4, 16)
BAND_BLOCK = 128
BAND_HALF = 64
BAND_SPAN = BAND_BLOCK + 2 * BAND_HALF
SUPER_BLOCK = BAND_BLOCK * max(DILATIONS)
LANES = 128
N_PAIRS = D_A // LANES
GRID_W = 64
ROPE_THETA = 10000.0
N_BUCKETS = 32
MAX_DISTANCE = 1024
EPS = 1e-6
NEG_INF = -1e30

ROW_TILE = 512
PROJ_SUBTILES = 2
Q_TILE = 512
Q_TILES_PER_STEP = 2
KEY_STEP = 256
LOOKAHEAD = 2
VMEM_LIMIT = 56 * 1024 * 1024
SAFE_EXPONENT = 60.0
SHIFT_TERMS = 3
SHIFT_ROWS = 16

F32 = jnp.float32
BF16 = jnp.bfloat16


def _nt_dot(a, b):
    return lax.dot_general(a, b, (((1,), (1,)), ((), ())), preferred_element_type=F32)


def _dot(a, b):
    return jnp.dot(a, b, preferred_element_type=F32)


def _silu(g):
    return g * jax.nn.sigmoid(g)


def _head_rms(t, gain):
    ms = jnp.mean(t * t, axis=1, keepdims=True)
    return t * lax.rsqrt(ms + EPS) * gain[None]


def _rope_halves(t, cos, sin):
    half = HEAD_DIM // 2
    x1 = t[:, :half, :]
    x2 = t[:, half:, :]
    c = cos[None]
    s = sin[None]
    return jnp.concatenate([x1 * c - x2 * s, x1 * s + x2 * c], axis=1)


def _zero_from(tile, width):
    bits = pltpu.bitcast(tile, jnp.uint32)
    zero = lax.shift_right_logical(lax.shift_right_logical(bits, jnp.uint32(16)), jnp.uint32(16))
    return jnp.tile(pltpu.bitcast(zero, F32), (1, width // LANES))


def _residue_layout_jobs(src, scr, o4, o16):
    t = src.shape[1]

    def fill(pr):
        scr[pr] = src[0, :, pr * LANES:(pr + 1) * LANES].astype(F32)
        return None

    def gather(pr, dil, out, r):
        lo = (pr * dil + r) * LANES
        out[0, :, lo:lo + LANES] = scr[pr, pl.ds(r, t // dil, stride=dil), :].astype(BF16)
        row = pl.multiple_of(jnp.minimum(pl.program_id(0), 0), 16)
        return out[0, pl.ds(row, 16), lo:lo + LANES]

    jobs = [functools.partial(fill, pr) for pr in range(N_PAIRS)]
    for pr in range(N_PAIRS):
        for dil, out in ((4, o4), (16, o16)):
            jobs += [functools.partial(gather, pr, dil, out, r) for r in range(dil)]
    return jobs


def _proj_kernel(x_ref, g_ref, wtok_ref, wfeat_ref, gains_ref, cos_ref, sin_ref,
                 ga_ref, gbt_ref, q1_ref, k1_ref, v1_ref, qbt_ref, kb_ref, vbt_ref):
    for sub in range(PROJ_SUBTILES):
        _proj_subtile(sub, x_ref, g_ref, wtok_ref, wfeat_ref, gains_ref, cos_ref, sin_ref,
                      ga_ref, gbt_ref, q1_ref, k1_ref, v1_ref, qbt_ref, kb_ref, vbt_ref)


def _proj_subtile(sub, x_ref, g_ref, wtok_ref, wfeat_ref, gains_ref, cos_ref, sin_ref,
                  ga_ref, gbt_ref, q1_ref, k1_ref, v1_ref, qbt_ref, kb_ref, vbt_ref):
    t = x_ref.shape[1] // PROJ_SUBTILES
    row0 = sub * t
    rows = slice(row0, row0 + t)
    scale = HEAD_DIM ** -0.5
    x = x_ref[0, rows, :]
    ms = jnp.mean(x * x, axis=-1, keepdims=True)
    h = (x * lax.rsqrt(ms + EPS) * g_ref[...]).astype(BF16)

    qa = _nt_dot(wfeat_ref[:D_A, :], h).reshape(N_HEADS_A, HEAD_DIM, t)
    qa = _head_rms(qa, gains_ref[0]) * scale
    q1_ref[0, rows, :] = qa.reshape(D_A, t).T.astype(BF16)
    ka = _nt_dot(wfeat_ref[D_A:2 * D_A, :], h).reshape(N_HEADS_A, HEAD_DIM, t)
    ka = _head_rms(ka, gains_ref[1])
    k1_ref[0, rows, :] = ka.reshape(D_A, t).T.astype(BF16)
    v1_ref[0, rows, :] = _dot(h, wtok_ref[:, :D_A]).astype(BF16)
    ga_ref[0, rows, :] = _dot(h, wtok_ref[:, D_A:]).astype(BF16)

    o = 2 * D_A
    featb = _nt_dot(wfeat_ref[o:, :], h)
    gbt_ref[0, :, :, rows] = featb[D_B + 2 * D_KV_B:].reshape(N_HEADS_B, HEAD_DIM, t).astype(BF16)
    cos = cos_ref[:, rows]
    sin = sin_ref[:, rows]
    qb = _head_rms(featb[:D_B].reshape(N_HEADS_B, HEAD_DIM, t), gains_ref[2])
    qbt_ref[0, :, :, rows] = (_rope_halves(qb, cos, sin) * scale).astype(BF16)
    kb = _head_rms(featb[D_B:D_B + D_KV_B].reshape(N_KV_B, HEAD_DIM, t), gains_ref[3])
    kb = _rope_halves(kb, cos, sin)
    ext = (lax.broadcasted_iota(jnp.int32, (HEAD_DIM, t), 0) < SHIFT_TERMS).astype(F32)
    for kv in range(N_KV_B):
        kb_ref[0, kv, rows, :] = jnp.concatenate([kb[kv], ext], axis=0).T.astype(BF16)
    vbt_ref[0, :, 0, :, rows] = featb[D_B + D_KV_B:D_B + 2 * D_KV_B].reshape(N_KV_B, HEAD_DIM, t).astype(BF16)


def _proj_call(x, ln_g, wtok, wfeat, gains, cos_t, sin_t):
    B, S, D = x.shape
    tm = ROW_TILE
    nt = S // tm
    full = lambda shape: pl.BlockSpec(shape, lambda b, i: (0,) * len(shape))
    tok_spec = pl.BlockSpec((1, tm, D_A), lambda b, i: (b, i, 0))
    tok_shape = jax.ShapeDtypeStruct((B, S, D_A), BF16)
    out_shape = (
        tok_shape,
        jax.ShapeDtypeStruct((B, N_HEADS_B, HEAD_DIM, S), BF16),
        tok_shape, tok_shape, tok_shape,
        jax.ShapeDtypeStruct((B, N_HEADS_B, HEAD_DIM, S), BF16),
        jax.ShapeDtypeStruct((B, N_KV_B, S, 2 * HEAD_DIM), BF16),
        jax.ShapeDtypeStruct((B, N_KV_B, nt, HEAD_DIM, tm), BF16),
    )
    out_specs = (
        tok_spec, pl.BlockSpec((1, N_HEADS_B, HEAD_DIM, tm), lambda b, i: (b, 0, 0, i)),
        tok_spec, tok_spec, tok_spec,
        pl.BlockSpec((1, N_HEADS_B, HEAD_DIM, tm), lambda b, i: (b, 0, 0, i)),
        pl.BlockSpec((1, N_KV_B, tm, 2 * HEAD_DIM), lambda b, i: (b, 0, i, 0)),
        pl.BlockSpec((1, N_KV_B, 1, HEAD_DIM, tm), lambda b, i: (b, 0, i, 0, 0)),
    )
    in_specs = [
        pl.BlockSpec((1, tm, D), lambda b, i: (b, i, 0)),
        full((1, D)),
        full(wtok.shape),
        full(wfeat.shape),
        full(gains.shape),
        pl.BlockSpec((HEAD_DIM // 2, tm), lambda b, i: (0, i)),
        pl.BlockSpec((HEAD_DIM // 2, tm), lambda b, i: (0, i)),
    ]
    return pl.pallas_call(
        _proj_kernel,
        grid=(B, nt),
        in_specs=in_specs,
        out_specs=out_specs,
        out_shape=out_shape,
        compiler_params=pltpu.CompilerParams(
            dimension_semantics=("arbitrary", "arbitrary"), vmem_limit_bytes=VMEM_LIMIT),
        name="proj",
    )(x, ln_g, wtok, wfeat, gains, cos_t, sin_t)


def _gqa_kernel(*refs, mode):
    if mode == "max":
        qt_ref, k_ref, o_ref = refs
    else:
        n_in = 5 if mode == "attn_shift" else 4
        qt_ref, k_ref, vt_ref, gt_ref = refs[:4]
        c_ref = refs[4] if mode == "attn_shift" else None
        nat_refs, o_ref = refs[n_in:n_in + 3], refs[n_in + 3]
        lay_refs, scr_refs = refs[n_in + 4:n_in + 10], refs[n_in + 10:]
        side_jobs = []
        for j in range(3):
            side_jobs += _residue_layout_jobs(nat_refs[j], scr_refs[j], lay_refs[2 * j], lay_refs[2 * j + 1])
    tq = Q_TILE
    n_steps = k_ref.shape[2] // KEY_STEP
    sub = 8
    heads = [(g, slice(j * tq, (j + 1) * tq)) for j in range(qt_ref.shape[3] // tq) for g in range(GQA_GROUP)]
    qext = {}
    for g, cols in heads:
        q = qt_ref[0, g, :, cols]
        if mode == "attn_shift":
            pad = jnp.zeros((HEAD_DIM - SHIFT_ROWS, tq), BF16)
            qext[g, cols.start] = jnp.concatenate([q, c_ref[0, g, :, cols], pad], axis=0)
        else:
            qext[g, cols.start] = jnp.concatenate([q, jnp.zeros_like(q)], axis=0)
    units = [(g, cols, t) for g, cols in heads for t in range(n_steps)]

    def scores(u):
        g, cols, t = u
        return _dot(k_ref[0, 0, t * KEY_STEP:(t + 1) * KEY_STEP, :], qext[g, cols.start])

    pending = [scores(u) for u in units[:LOOKAHEAD]]
    for i, (g, cols, t) in enumerate(units):
        if i + LOOKAHEAD < len(units):
            pending.append(scores(units[i + LOOKAHEAD]))
        s = pending[i]
        pending[i] = None
        if mode == "max":
            part = s.reshape(KEY_STEP // sub, sub, tq).max(axis=0)
            mx = part if t == 0 else jnp.maximum(mx, part)
            if t == n_steps - 1:
                o_ref[0, g, :, cols] = jnp.broadcast_to(mx.max(axis=0, keepdims=True), (sub, tq))
            continue
        if t == 0:
            acc = jnp.zeros((HEAD_DIM, tq), F32)
            den = jnp.zeros((sub, tq), F32)
        p = jnp.exp(s)
        den = den + p.reshape(KEY_STEP // sub, sub, tq).sum(axis=0)
        if side_jobs:
            back = side_jobs.pop(0)()
            if back is not None:
                den = den + _zero_from(back, tq)
        tk = vt_ref.shape[4]
        lo = (t * KEY_STEP) % tk
        acc = acc + _dot(vt_ref[0, 0, (t * KEY_STEP) // tk, :, lo:lo + KEY_STEP], p.astype(BF16))
        if t == n_steps - 1:
            y = acc / den.sum(axis=0, keepdims=True)
            o_ref[0, g, :, cols] = (y * _silu(gt_ref[0, g, :, cols].astype(F32))).astype(BF16)
    if mode != "max":
        for job in side_jobs:
            job()


def _gqa_call(qbt, kb, vbt, gbt, window_qkv, shift=None, mode="attn"):
    B, H, _, S = qbt.shape
    tq = Q_TILE * min(Q_TILES_PER_STEP, S // Q_TILE)
    n_q = S // tq
    nt, rows, tk = vbt.shape[2], vbt.shape[3], vbt.shape[4]
    head_blk = lambda r: pl.BlockSpec((1, GQA_GROUP, r, tq), lambda b, kv, i: (b, kv, 0, i))
    in_specs = [head_blk(HEAD_DIM), pl.BlockSpec((1, 1, S, 2 * HEAD_DIM), lambda b, kv, i: (b, kv, 0, 0))]
    args = [qbt, kb]
    grid = (B, N_KV_B, n_q)
    params = pltpu.CompilerParams(
        dimension_semantics=("arbitrary", "arbitrary", "arbitrary"), vmem_limit_bytes=VMEM_LIMIT)
    if mode == "max":
        return pl.pallas_call(
            functools.partial(_gqa_kernel, mode=mode), grid=grid, in_specs=in_specs, out_specs=head_blk(8),
            out_shape=jax.ShapeDtypeStruct((B, H, 8, S), F32), compiler_params=params, name="gqa_" + mode,
        )(*args)
    in_specs += [pl.BlockSpec((1, 1, nt, rows, tk), lambda b, kv, i: (b, kv, 0, 0, 0)), head_blk(HEAD_DIM)]
    args += [vbt, gbt]
    if mode == "attn_shift":
        in_specs.append(head_blk(SHIFT_ROWS))
        args.append(shift)
    t = S // (N_KV_B * n_q)
    tile = lambda b, kv, i: (b, kv * n_q + i, 0)
    in_specs += [pl.BlockSpec((1, t, D_A), tile)] * 3
    args += list(window_qkv)
    out_specs, out_shape = [head_blk(HEAD_DIM)], [jax.ShapeDtypeStruct((B, H, HEAD_DIM, S), BF16)]
    for _ in range(3):
        for dil in DILATIONS[1:]:
            out_specs.append(pl.BlockSpec((1, t // dil, dil * D_A), tile))
            out_shape.append(jax.ShapeDtypeStruct((B, S // dil, dil * D_A), BF16))
    outs = pl.pallas_call(
        functools.partial(_gqa_kernel, mode=mode), grid=grid, in_specs=in_specs, out_specs=out_specs,
        out_shape=out_shape, scratch_shapes=[pltpu.VMEM((N_PAIRS, t, LANES), F32)] * 3,
        compiler_params=params, name="gqa_" + mode,
    )(*args)
    return outs[0], outs[1:]


def _shift_rows(m):
    rest = m[:, :, :1, :]
    terms = []
    for _ in range(SHIFT_TERMS):
        t = rest.astype(BF16)
        terms.append(-t)
        rest = rest - t.astype(F32)
    pad = jnp.zeros(m.shape[:2] + (SHIFT_ROWS - SHIFT_TERMS, m.shape[3]), BF16)
    return jnp.concatenate(terms + [pad], axis=2)


def _band_bucket_index():
    nb = N_BUCKETS // 2
    max_exact = nb // 2
    rel = np.arange(BAND_SPAN)[None, :] - BAND_HALF - np.arange(BAND_BLOCK)[:, None]
    out = []
    for dilation in DILATIONS:
        dist = rel * dilation
        ret = np.where(dist > 0, nb, 0)
        n = np.abs(dist)
        nf = np.maximum(n, max_exact).astype(np.float32)
        large = max_exact + (np.log(nf / np.float32(max_exact)) / np.float32(math.log(MAX_DISTANCE / max_exact))
                             * np.float32(nb - max_exact)).astype(np.int32)
        large = np.minimum(large, nb - 1)
        bucket = ret + np.where(n < max_exact, n, large)
        out.append(np.where(np.abs(rel) <= BAND_HALF, bucket, -1))
    return np.stack(out).astype(np.int32)


def _bias_kernel(table_ref, idx_ref, o_ref):
    chunk = 16
    index = _band_bucket_index()
    for pat in range(len(DILATIONS)):
        present = [int(b) for b in np.unique(index[pat]) if b >= 0]
        for lo in range(0, BAND_BLOCK, chunk):
            idx = idx_ref[pat, lo:lo + chunk, :]
            acc = [jnp.full(idx.shape, NEG_INF, F32) for _ in range(N_HEADS_A)]
            for b in present:
                hit = idx == b
                for h in range(N_HEADS_A):
                    acc[h] = jnp.where(hit, table_ref[b, h], acc[h])
            for h in range(N_HEADS_A):
                o_ref[pat, h, lo:lo + chunk, :] = acc[h]


def _bias_call(rel_bias):
    idx = jnp.asarray(_band_bucket_index())
    n_pat = len(DILATIONS)
    return pl.pallas_call(
        _bias_kernel,
        in_specs=[pl.BlockSpec(memory_space=pltpu.SMEM), pl.BlockSpec(memory_space=pltpu.VMEM)],
        out_specs=pl.BlockSpec(memory_space=pltpu.VMEM),
        out_shape=jax.ShapeDtypeStruct((n_pat, N_HEADS_A, BAND_BLOCK, BAND_SPAN), F32),
        name="bias_bands",
    )(rel_bias, idx)


def _band_kernel(*refs, mode):
    n_in = 7 * len(DILATIONS)
    ins, bias_ref = refs[:n_in], refs[n_in]
    g_ref = refs[n_in + 1] if mode != "max" else None
    c_ref = refs[n_in + 2] if mode == "attn_shift" else None
    o_ref, num_ref, den_ref = refs[-3:]
    n = pl.program_id(1)
    last = pl.num_programs(1) - 1
    col = lax.broadcasted_iota(jnp.int32, (1, BAND_SPAN), 1)
    edge_lo = jnp.where((n == 0) & (col < BAND_HALF), NEG_INF, 0.0).astype(F32)
    edge_hi = jnp.where((n == last) & (col >= BAND_BLOCK + BAND_HALF), NEG_INF, 0.0).astype(F32)
    lane = lax.broadcasted_iota(jnp.int32, (BAND_BLOCK, LANES), 1)
    first_half = lane < HEAD_DIM
    ones = jnp.ones((BAND_SPAN, LANES), BF16)

    def band(main, prev, nxt, blk, n_blk, ls):
        lo = blk * BAND_BLOCK - BAND_HALF
        parts = []
        if blk == 0:
            parts.append(prev[0, :, ls])
        start = max(lo, 0)
        stop = min(lo + BAND_SPAN, n_blk * BAND_BLOCK)
        parts.append(main[0, start:stop, ls])
        if blk == n_blk - 1:
            parts.append(nxt[0, :, ls])
        return parts[0] if len(parts) == 1 else jnp.concatenate(parts, axis=0)

    units = []
    for pat in reversed(range(len(DILATIONS))):
        dil = DILATIONS[pat]
        q_ref, k_ref, kp_ref, kn_ref, v_ref, vp_ref, vn_ref = ins[7 * pat:7 * pat + 7]
        n_blk = SUPER_BLOCK // (dil * BAND_BLOCK)
        for r in range(dil):
            for blk in range(n_blk):
                units.append((pat, dil, r, blk, n_blk, q_ref, k_ref, kp_ref, kn_ref, v_ref, vp_ref, vn_ref))

    def stage_scores(u):
        pat, dil, r, blk, n_blk, q_ref, k_ref, kp_ref, kn_ref = u[:9]
        ls = slice(r * LANES, (r + 1) * LANES)
        q = q_ref[0, blk * BAND_BLOCK:(blk + 1) * BAND_BLOCK, ls]
        kband = band(k_ref, kp_ref, kn_ref, blk, n_blk, ls)
        zero = jnp.zeros_like(q)
        return [_nt_dot(jnp.where(first_half, q, zero), kband), _nt_dot(jnp.where(first_half, zero, q), kband)]

    def stage_values(u, s_pair):
        pat, dil, r, blk, n_blk = u[:5]
        v_ref, vp_ref, vn_ref = u[9:12]
        ls = slice(r * LANES, (r + 1) * LANES)
        if dil == 1:
            rows = slice(blk * BAND_BLOCK, (blk + 1) * BAND_BLOCK)
        else:
            rows = pl.ds(dil * blk * BAND_BLOCK + r, BAND_BLOCK, stride=dil)
        def biased(half):
            s = s_pair[half] + bias_ref[pat, half]
            if blk == 0:
                s = s + edge_lo
            if blk == n_blk - 1:
                s = s + edge_hi
            return s

        if mode == "max":
            mx = jnp.where(first_half, biased(0).max(axis=-1, keepdims=True), biased(1).max(axis=-1, keepdims=True))
            if dil == DILATIONS[-1]:
                num_ref[rows, :] = mx
            elif dil > 1:
                num_ref[rows, :] = jnp.maximum(num_ref[rows, :], mx)
            else:
                o_ref[0, rows, :] = jnp.maximum(num_ref[rows, :], mx)
            return
        vext = jnp.concatenate([band(v_ref, vp_ref, vn_ref, blk, n_blk, ls), ones], axis=1)
        outs = []
        for half in range(2):
            s = biased(half)
            if mode == "attn_shift":
                s = s - c_ref[0, rows, :][:, half * HEAD_DIM:half * HEAD_DIM + 1]
            outs.append(_dot(jnp.exp(s).astype(BF16), vext))
        num = jnp.where(first_half, outs[0][:, :LANES], outs[1][:, :LANES])
        den = jnp.where(first_half, outs[0][:, LANES:], outs[1][:, LANES:])
        if dil == DILATIONS[-1]:
            num_ref[rows, :] = num
            den_ref[rows, :] = den
        elif dil > 1:
            num_ref[rows, :] = num_ref[rows, :] + num
            den_ref[rows, :] = den_ref[rows, :] + den
        else:
            y = (num_ref[rows, :] + num) / (den_ref[rows, :] + den)
            o_ref[0, rows, :] = (y * _silu(g_ref[0, rows, :].astype(F32))).astype(BF16)

    pending = [stage_scores(u) for u in units[:LOOKAHEAD]]
    for t, u in enumerate(units):
        if t + LOOKAHEAD < len(units):
            pending.append(stage_scores(units[t + LOOKAHEAD]))
        stage_values(u, pending[t])
        pending[t] = None


def _band_call(layouts, bias, gate, shift=None, mode="attn"):
    B, S, _ = layouts[0][0].shape
    n_sb = S // SUPER_BLOCK
    in_specs, args = [], []
    for (q, k, v), dil in zip(layouts, DILATIONS):
        rows = SUPER_BLOCK // dil
        width = dil * LANES
        n_halo = (S // dil) // BAND_HALF
        per_sb = rows // BAND_HALF
        main = pl.BlockSpec((1, rows, width), lambda b, n, pr: (b, n, pr))
        prev = pl.BlockSpec((1, BAND_HALF, width),
                            lambda b, n, pr, per_sb=per_sb: (b, jnp.maximum(n * per_sb - 1, 0), pr))
        nxt = pl.BlockSpec((1, BAND_HALF, width),
                           lambda b, n, pr, per_sb=per_sb, n_halo=n_halo: (b, jnp.minimum((n + 1) * per_sb, n_halo - 1), pr))
        in_specs += [main, main, prev, nxt, main, prev, nxt]
        args += [q, k, k, k, v, v, v]
    in_specs.append(pl.BlockSpec((len(DILATIONS), 2, BAND_BLOCK, BAND_SPAN), lambda b, n, pr: (0, pr, 0, 0)))
    args.append(bias)
    tok_blk = pl.BlockSpec((1, SUPER_BLOCK, LANES), lambda b, n, pr: (b, n, pr))
    if mode != "max":
        in_specs.append(tok_blk)
        args.append(gate)
    if mode == "attn_shift":
        in_specs.append(tok_blk)
        args.append(shift)
    return pl.pallas_call(
        functools.partial(_band_kernel, mode=mode),
        grid=(B, n_sb, N_PAIRS),
        in_specs=in_specs,
        out_specs=tok_blk,
        out_shape=jax.ShapeDtypeStruct((B, S, D_A), F32 if mode == "max" else BF16),
        scratch_shapes=[pltpu.VMEM((SUPER_BLOCK, LANES), F32)] * 2,
        compiler_params=pltpu.CompilerParams(
            dimension_semantics=("arbitrary", "arbitrary", "arbitrary"), vmem_limit_bytes=VMEM_LIMIT),
        name="band_" + mode,
    )(*args)


def _out_kernel(x_ref, p_ref, ya_ref, ybt_ref, wo_ref, wple_ref, wgate_ref, o_ref):
    tm = x_ref.shape[1]
    yb = ybt_ref[0].reshape(D_B, tm).astype(F32).T.astype(BF16)
    x1 = x_ref[0] + _dot(ya_ref[0], wo_ref[:D_A, :]) + _dot(yb, wo_ref[D_A:, :])
    gate = jax.nn.sigmoid(_dot(x1.astype(BF16), wgate_ref[...]))
    ple = _dot(p_ref[0].astype(BF16), wple_ref[...])
    o_ref[0] = x1 + ple * gate


def _out_call(x, p, ya, ybt, wo, wple, wgate):
    B, S, D = x.shape
    tm = ROW_TILE
    full = lambda shape: pl.BlockSpec(shape, lambda b, i: (0,) * len(shape))
    tok = lambda c: pl.BlockSpec((1, tm, c), lambda b, i: (b, i, 0))
    return pl.pallas_call(
        _out_kernel,
        grid=(B, S // tm),
        in_specs=[tok(D), tok(p.shape[-1]), tok(D_A),
                  pl.BlockSpec((1, N_HEADS_B, HEAD_DIM, tm), lambda b, i: (b, 0, 0, i)),
                  full(wo.shape), full(wple.shape), full(wgate.shape)],
        out_specs=tok(D),
        out_shape=jax.ShapeDtypeStruct((B, S, D), F32),
        compiler_params=pltpu.CompilerParams(
            dimension_semantics=("arbitrary", "arbitrary"), vmem_limit_bytes=VMEM_LIMIT),
        name="out_proj",
    )(x, p, ya, ybt, wo, wple, wgate)


def _rope_tables(S):
    n = HEAD_DIM // 4
    inv = ROPE_THETA ** (-jnp.arange(n, dtype=F32) / n)
    rows = S // GRID_W
    ang_row = inv[:, None] * jnp.arange(rows, dtype=jnp.int32).astype(F32)[None, :]
    ang_col = inv[:, None] * jnp.arange(GRID_W, dtype=jnp.int32).astype(F32)[None, :]
    expand = lambda fn: jnp.concatenate(
        [jnp.repeat(fn(ang_row), GRID_W, axis=1), jnp.tile(fn(ang_col), (1, rows))], axis=0)
    return expand(jnp.cos), expand(jnp.sin)


_ROPE_PERM = np.concatenate([np.arange(0, HEAD_DIM, 2), np.arange(1, HEAD_DIM, 2)])


def _feature_major_columns():
    o_qb = 4 * D_A
    o_kb = o_qb + D_B
    heads = lambda start, n: (start + HEAD_DIM * np.arange(n)[:, None] + _ROPE_PERM[None, :]).reshape(-1)
    return np.concatenate([np.arange(2 * D_A), heads(o_qb, N_HEADS_B), heads(o_kb, N_KV_B),
                           np.arange(o_kb + D_KV_B, o_kb + 2 * D_KV_B + D_B)]).astype(np.int32)


def kernel(x, p, ln_g, w_in, qn_a, kn_a, qn_b, kn_b, w_out, w_ple, w_pgate, rel_bias):
    B, S, D = x.shape
    depth = w_in.shape[0]
    cos_t, sin_t = _rope_tables(S)
    bias = _bias_call(rel_bias)
    feat_cols = _feature_major_columns()
    for i in range(depth):
        wt = w_in[i].T.astype(BF16)
        wfeat = jnp.take(wt, feat_cols, axis=0)
        wtok = w_in[i][:, 2 * D_A:4 * D_A].astype(BF16)
        gains = jnp.stack([qn_a[i], kn_a[i], qn_b[i][_ROPE_PERM], kn_b[i][_ROPE_PERM]]).astype(F32)
        gains = jnp.broadcast_to(gains[:, :, None], (4, HEAD_DIM, ROW_TILE // PROJ_SUBTILES))

        outs = _proj_call(
            x, ln_g[i][None, :], wtok, wfeat, gains, cos_t, sin_t)
        ga, gbt, q1, k1, v1, qbt, kb, vbt = outs

        gmax = jnp.max(jnp.abs(jnp.stack([qn_a[i], kn_a[i], qn_b[i], kn_b[i]]).astype(F32)), axis=1)
        root = HEAD_DIM ** 0.5
        bound_a = root * gmax[0] * gmax[1] + jnp.max(jnp.abs(rel_bias.astype(F32)))
        bound_b = root * gmax[2] * gmax[3]
        safe = jnp.maximum(bound_a, bound_b) <= SAFE_EXPONENT
        def band_layouts(qkv, lay):
            (q1, k1, v1), (q4, q16, k4, k16, v4, v16) = qkv, lay
            return ((q1, k1, v1), (q4, k4, v4), (q16, k16, v16))

        def unshifted(ops):
            qbt, kb, vbt, gbt, qkv, bias, ga = ops
            ybt, lay = _gqa_call(qbt, kb, vbt, gbt, qkv)
            return ybt, _band_call(band_layouts(qkv, lay), bias, ga)

        def shifted(ops):
            qbt, kb, vbt, gbt, qkv, bias, ga = ops
            shift_b = _shift_rows(_gqa_call(qbt, kb, vbt, gbt, qkv, mode="max"))
            ybt, lay = _gqa_call(qbt, kb, vbt, gbt, qkv, shift_b, mode="attn_shift")
            layouts = band_layouts(qkv, lay)
            shift_a = _band_call(layouts, bias, ga, mode="max")
            return ybt, _band_call(layouts, bias, ga, shift_a, mode="attn_shift")

        ybt, ya = lax.cond(safe, unshifted, shifted, (qbt, kb, vbt, gbt, (q1, k1, v1), bias, ga))
        x = _out_call(x, p[i], ya, ybt, w_out[i].astype(BF16), w_ple[i].astype(BF16), w_pgate[i].astype(BF16))
    return x
```

```python
import functools
import math

import numpy as np
import jax
import jax.numpy as jnp
from jax import lax
from jax.experimental import pallas as pl
from jax.experimental.pallas import tpu as pltpu

HEAD_DIM = 64
N_HEADS_A = 8
N_HEADS_B = 8
N_KV_B = 2
GQA_GROUP = N_HEADS_B // N_KV_B
D_A = N_HEADS_A * HEAD_DIM
D_B = N_HEADS_B * HEAD_DIM
D_KV_B = N_KV_B * HEAD_DIM
DILATIONS = (1, 4, 16)
BAND_BLOCK = 128
BAND_HALF = 64
BAND_SPAN = BAND_BLOCK + 2 * BAND_HALF
SUPER_BLOCK = BAND_BLOCK * max(DILATIONS)
LANES = 128
N_PAIRS = D_A // LANES
GRID_W = 64
ROPE_THETA = 10000.0
N_BUCKETS = 32
MAX_DISTANCE = 1024
EPS = 1e-6
NEG_INF = -1e30

ROW_TILE = 1024
PROJ_SUBTILES = 4
Q_TILE = 512
Q_TILES_PER_STEP = 2
KEY_STEP = 256
LOOKAHEAD = 2
VMEM_LIMIT = 56 * 1024 * 1024
SAFE_EXPONENT = 60.0
SHIFT_TERMS = 3
SHIFT_ROWS = 16

F32 = jnp.float32
BF16 = jnp.bfloat16


def _nt_dot(a, b):
    return lax.dot_general(a, b, (((1,), (1,)), ((), ())), preferred_element_type=F32)


def _dot(a, b):
    return jnp.dot(a, b, preferred_element_type=F32)


def _silu(g):
    return g * jax.nn.sigmoid(g)


def _head_rms(t, gain):
    ms = jnp.mean(t * t, axis=1, keepdims=True)
    return t * lax.rsqrt(ms + EPS) * gain[None]


def _rope_halves(t, cos, sin):
    half = HEAD_DIM // 2
    x1 = t[:, :half, :]
    x2 = t[:, half:, :]
    c = cos[None]
    s = sin[None]
    return jnp.concatenate([x1 * c - x2 * s, x1 * s + x2 * c], axis=1)


def _zero_from(tile, width):
    bits = pltpu.bitcast(tile, jnp.uint32)
    zero = lax.shift_right_logical(lax.shift_right_logical(bits, jnp.uint32(16)), jnp.uint32(16))
    return jnp.tile(pltpu.bitcast(zero, F32), (1, width // LANES))


def _residue_layout_jobs(src, scr, o4, o16):
    t = src.shape[1]

    def fill(pr):
        scr[pr] = src[0, :, pr * LANES:(pr + 1) * LANES].astype(F32)
        return None

    def gather(pr, dil, out, r):
        lo = (pr * dil + r) * LANES
        out[0, :, lo:lo + LANES] = scr[pr, pl.ds(r, t // dil, stride=dil), :].astype(BF16)
        row = pl.multiple_of(jnp.minimum(pl.program_id(0), 0), 16)
        return out[0, pl.ds(row, 16), lo:lo + LANES]

    jobs = [functools.partial(fill, pr) for pr in range(N_PAIRS)]
    for pr in range(N_PAIRS):
        for dil, out in ((4, o4), (16, o16)):
            jobs += [functools.partial(gather, pr, dil, out, r) for r in range(dil)]
    return jobs


def _proj_kernel(x_ref, g_ref, wtok_ref, wfeat_ref, wrope_ref, gains_ref, cos_ref, sin_ref,
                 ga_ref, gbt_ref, q1_ref, k1_ref, v1_ref, qbt_ref, kb_ref, vbt_ref):
    for sub in range(PROJ_SUBTILES):
        _proj_subtile(sub, x_ref, g_ref, wtok_ref, wfeat_ref, wrope_ref, gains_ref, cos_ref, sin_ref,
                      ga_ref, gbt_ref, q1_ref, k1_ref, v1_ref, qbt_ref, kb_ref, vbt_ref)


def _proj_subtile(sub, x_ref, g_ref, wtok_ref, wfeat_ref, wrope_ref, gains_ref, cos_ref, sin_ref,
                  ga_ref, gbt_ref, q1_ref, k1_ref, v1_ref, qbt_ref, kb_ref, vbt_ref):
    t = x_ref.shape[1] // PROJ_SUBTILES
    row0 = sub * t
    rows = slice(row0, row0 + t)
    scale = HEAD_DIM ** -0.5
    x = x_ref[0, rows, :]
    ms = jnp.mean(x * x, axis=-1, keepdims=True)
    h = (x * lax.rsqrt(ms + EPS) * g_ref[...]).astype(BF16)

    qa = _nt_dot(wfeat_ref[:D_A, :], h).reshape(N_HEADS_A, HEAD_DIM, t)
    qa = _head_rms(qa, gains_ref[0]) * scale
    q1_ref[0, rows, :] = qa.reshape(D_A, t).T.astype(BF16)
    ka = _nt_dot(wfeat_ref[D_A:2 * D_A, :], h).reshape(N_HEADS_A, HEAD_DIM, t)
    ka = _head_rms(ka, gains_ref[1])
    k1_ref[0, rows, :] = ka.reshape(D_A, t).T.astype(BF16)
    v1_ref[0, rows, :] = _dot(h, wtok_ref[:, :D_A]).astype(BF16)
    ga_ref[0, rows, :] = _dot(h, wtok_ref[:, D_A:]).astype(BF16)

    rope = _nt_dot(wrope_ref[...], h)
    o = 4 * D_A + D_B + D_KV_B
    vg = _nt_dot(wfeat_ref[o:, :], h)
    gbt_ref[0, :, :, rows] = vg[D_KV_B:].reshape(N_HEADS_B, HEAD_DIM, t).astype(BF16)
    cos = cos_ref[:, rows]
    sin = sin_ref[:, rows]
    qb = _head_rms(rope[:D_B].reshape(N_HEADS_B, HEAD_DIM, t), gains_ref[2])
    qbt_ref[0, :, :, rows] = (_rope_halves(qb, cos, sin) * scale).astype(BF16)
    kb = _head_rms(rope[D_B:].reshape(N_KV_B, HEAD_DIM, t), gains_ref[3])
    kb = _rope_halves(kb, cos, sin)
    ext = (lax.broadcasted_iota(jnp.int32, (HEAD_DIM, t), 0) < SHIFT_TERMS).astype(F32)
    for kv in range(N_KV_B):
        kb_ref[0, kv, rows, :] = jnp.concatenate([kb[kv], ext], axis=0).T.astype(BF16)
    vbt_ref[0, :, 0, :, rows] = vg[:D_KV_B].reshape(N_KV_B, HEAD_DIM, t).astype(BF16)


def _proj_call(x, ln_g, wtok, wfeat, wrope, gains, cos_t, sin_t):
    B, S, D = x.shape
    tm = ROW_TILE
    nt = S // tm
    full = lambda shape: pl.BlockSpec(shape, lambda b, i: (0,) * len(shape))
    tok_spec = pl.BlockSpec((1, tm, D_A), lambda b, i: (b, i, 0))
    tok_shape = jax.ShapeDtypeStruct((B, S, D_A), BF16)
    out_shape = (
        tok_shape,
        jax.ShapeDtypeStruct((B, N_HEADS_B, HEAD_DIM, S), BF16),
        tok_shape, tok_shape, tok_shape,
        jax.ShapeDtypeStruct((B, N_HEADS_B, HEAD_DIM, S), BF16),
        jax.ShapeDtypeStruct((B, N_KV_B, S, 2 * HEAD_DIM), BF16),
        jax.ShapeDtypeStruct((B, N_KV_B, nt, HEAD_DIM, tm), BF16),
    )
    out_specs = (
        tok_spec, pl.BlockSpec((1, N_HEADS_B, HEAD_DIM, tm), lambda b, i: (b, 0, 0, i)),
        tok_spec, tok_spec, tok_spec,
        pl.BlockSpec((1, N_HEADS_B, HEAD_DIM, tm), lambda b, i: (b, 0, 0, i)),
        pl.BlockSpec((1, N_KV_B, tm, 2 * HEAD_DIM), lambda b, i: (b, 0, i, 0)),
        pl.BlockSpec((1, N_KV_B, 1, HEAD_DIM, tm), lambda b, i: (b, 0, i, 0, 0)),
    )
    in_specs = [
        pl.BlockSpec((1, tm, D), lambda b, i: (b, i, 0)),
        full((1, D)),
        full(wtok.shape),
        full(wfeat.shape),
        full(wrope.shape),
        full(gains.shape),
        pl.BlockSpec((HEAD_DIM // 2, tm), lambda b, i: (0, i)),
        pl.BlockSpec((HEAD_DIM // 2, tm), lambda b, i: (0, i)),
    ]
    return pl.pallas_call(
        _proj_kernel,
        grid=(B, nt),
        in_specs=in_specs,
        out_specs=out_specs,
        out_shape=out_shape,
        compiler_params=pltpu.CompilerParams(
            dimension_semantics=("arbitrary", "arbitrary"), vmem_limit_bytes=VMEM_LIMIT),
        name="proj",
    )(x, ln_g, wtok, wfeat, wrope, gains, cos_t, sin_t)


def _gqa_kernel(*refs, mode):
    if mode == "max":
        qt_ref, k_ref, o_ref = refs
    else:
        n_in = 5 if mode == "attn_shift" else 4
        qt_ref, k_ref, vt_ref, gt_ref = refs[:4]
        c_ref = refs[4] if mode == "attn_shift" else None
        nat_refs, o_ref = refs[n_in:n_in + 3], refs[n_in + 3]
        lay_refs, scr_refs = refs[n_in + 4:n_in + 10], refs[n_in + 10:]
        side_jobs = []
        for j in range(3):
            side_jobs += _residue_layout_jobs(nat_refs[j], scr_refs[j], lay_refs[2 * j], lay_refs[2 * j + 1])
    tq = Q_TILE
    n_steps = k_ref.shape[2] // KEY_STEP
    sub = 8
    heads = [(g, slice(j * tq, (j + 1) * tq)) for j in range(qt_ref.shape[3] // tq) for g in range(GQA_GROUP)]
    qext = {}
    for g, cols in heads:
        q = qt_ref[0, g, :, cols]
        if mode == "attn_shift":
            pad = jnp.zeros((HEAD_DIM - SHIFT_ROWS, tq), BF16)
            qext[g, cols.start] = jnp.concatenate([q, c_ref[0, g, :, cols], pad], axis=0)
        else:
            qext[g, cols.start] = jnp.concatenate([q, jnp.zeros_like(q)], axis=0)
    units = [(g, cols, t) for g, cols in heads for t in range(n_steps)]

    def scores(u):
        g, cols, t = u
        return _dot(k_ref[0, 0, t * KEY_STEP:(t + 1) * KEY_STEP, :], qext[g, cols.start])

    pending = [scores(u) for u in units[:LOOKAHEAD]]
    for i, (g, cols, t) in enumerate(units):
        if i + LOOKAHEAD < len(units):
            pending.append(scores(units[i + LOOKAHEAD]))
        s = pending[i]
        pending[i] = None
        if mode == "max":
            part = s.reshape(KEY_STEP // sub, sub, tq).max(axis=0)
            mx = part if t == 0 else jnp.maximum(mx, part)
            if t == n_steps - 1:
                o_ref[0, g, :, cols] = jnp.broadcast_to(mx.max(axis=0, keepdims=True), (sub, tq))
            continue
        if t == 0:
            acc = jnp.zeros((HEAD_DIM, tq), F32)
            den = jnp.zeros((sub, tq), F32)
        p = jnp.exp(s)
        den = den + p.reshape(KEY_STEP // sub, sub, tq).sum(axis=0)
        if side_jobs:
            back = side_jobs.pop(0)()
            if back is not None:
                den = den + _zero_from(back, tq)
        tk = vt_ref.shape[4]
        lo = (t * KEY_STEP) % tk
        acc = acc + _dot(vt_ref[0, 0, (t * KEY_STEP) // tk, :, lo:lo + KEY_STEP], p.astype(BF16))
        if t == n_steps - 1:
            y = acc / den.sum(axis=0, keepdims=True)
            o_ref[0, g, :, cols] = (y * _silu(gt_ref[0, g, :, cols].astype(F32))).astype(BF16)
    if mode != "max":
        for job in side_jobs:
            job()


def _gqa_call(qbt, kb, vbt, gbt, window_qkv, shift=None, mode="attn"):
    B, H, _, S = qbt.shape
    tq = Q_TILE * min(Q_TILES_PER_STEP, S // Q_TILE)
    n_q = S // tq
    nt, rows, tk = vbt.shape[2], vbt.shape[3], vbt.shape[4]
    head_blk = lambda r: pl.BlockSpec((1, GQA_GROUP, r, tq), lambda b, kv, i: (b, kv, 0, i))
    in_specs = [head_blk(HEAD_DIM), pl.BlockSpec((1, 1, S, 2 * HEAD_DIM), lambda b, kv, i: (b, kv, 0, 0))]
    args = [qbt, kb]
    grid = (B, N_KV_B, n_q)
    params = pltpu.CompilerParams(
        dimension_semantics=("arbitrary", "arbitrary", "arbitrary"), vmem_limit_bytes=VMEM_LIMIT)
    if mode == "max":
        return pl.pallas_call(
            functools.partial(_gqa_kernel, mode=mode), grid=grid, in_specs=in_specs, out_specs=head_blk(8),
            out_shape=jax.ShapeDtypeStruct((B, H, 8, S), F32), compiler_params=params, name="gqa_" + mode,
        )(*args)
    in_specs += [pl.BlockSpec((1, 1, nt, rows, tk), lambda b, kv, i: (b, kv, 0, 0, 0)), head_blk(HEAD_DIM)]
    args += [vbt, gbt]
    if mode == "attn_shift":
        in_specs.append(head_blk(SHIFT_ROWS))
        args.append(shift)
    t = S // (N_KV_B * n_q)
    tile = lambda b, kv, i: (b, kv * n_q + i, 0)
    in_specs += [pl.BlockSpec((1, t, D_A), tile)] * 3
    args += list(window_qkv)
    out_specs, out_shape = [head_blk(HEAD_DIM)], [jax.ShapeDtypeStruct((B, H, HEAD_DIM, S), BF16)]
    for _ in range(3):
        for dil in DILATIONS[1:]:
            out_specs.append(pl.BlockSpec((1, t // dil, dil * D_A), tile))
            out_shape.append(jax.ShapeDtypeStruct((B, S // dil, dil * D_A), BF16))
    outs = pl.pallas_call(
        functools.partial(_gqa_kernel, mode=mode), grid=grid, in_specs=in_specs, out_specs=out_specs,
        out_shape=out_shape, scratch_shapes=[pltpu.VMEM((N_PAIRS, t, LANES), F32)] * 3,
        compiler_params=params, name="gqa_" + mode,
    )(*args)
    return outs[0], outs[1:]


def _shift_rows(m):
    rest = m[:, :, :1, :]
    terms = []
    for _ in range(SHIFT_TERMS):
        t = rest.astype(BF16)
        terms.append(-t)
        rest = rest - t.astype(F32)
    pad = jnp.zeros(m.shape[:2] + (SHIFT_ROWS - SHIFT_TERMS, m.shape[3]), BF16)
    return jnp.concatenate(terms + [pad], axis=2)


def _band_bucket_index():
    nb = N_BUCKETS // 2
    max_exact = nb // 2
    rel = np.arange(BAND_SPAN)[None, :] - BAND_HALF - np.arange(BAND_BLOCK)[:, None]
    out = []
    for dilation in DILATIONS:
        dist = rel * dilation
        ret = np.where(dist > 0, nb, 0)
        n = np.abs(dist)
        nf = np.maximum(n, max_exact).astype(np.float32)
        large = max_exact + (np.log(nf / np.float32(max_exact)) / np.float32(math.log(MAX_DISTANCE / max_exact))
                             * np.float32(nb - max_exact)).astype(np.int32)
        large = np.minimum(large, nb - 1)
        bucket = ret + np.where(n < max_exact, n, large)
        out.append(np.where(np.abs(rel) <= BAND_HALF, bucket, -1))
    return np.stack(out).astype(np.int32)


def _bias_kernel(table_ref, idx_ref, o_ref):
    chunk = 16
    index = _band_bucket_index()
    for pat in range(len(DILATIONS)):
        present = [int(b) for b in np.unique(index[pat]) if b >= 0]
        for lo in range(0, BAND_BLOCK, chunk):
            idx = idx_ref[pat, lo:lo + chunk, :]
            acc = [jnp.full(idx.shape, NEG_INF, F32) for _ in range(N_HEADS_A)]
            for b in present:
                hit = idx == b
                for h in range(N_HEADS_A):
                    acc[h] = jnp.where(hit, table_ref[b, h], acc[h])
            for h in range(N_HEADS_A):
                o_ref[pat, h, lo:lo + chunk, :] = acc[h]


def _bias_call(rel_bias):
    idx = jnp.asarray(_band_bucket_index())
    n_pat = len(DILATIONS)
    return pl.pallas_call(
        _bias_kernel,
        in_specs=[pl.BlockSpec(memory_space=pltpu.SMEM), pl.BlockSpec(memory_space=pltpu.VMEM)],
        out_specs=pl.BlockSpec(memory_space=pltpu.VMEM),
        out_shape=jax.ShapeDtypeStruct((n_pat, N_HEADS_A, BAND_BLOCK, BAND_SPAN), F32),
        name="bias_bands",
    )(rel_bias, idx)


def _band_kernel(*refs, mode):
    n_in = 7 * len(DILATIONS)
    ins, bias_ref = refs[:n_in], refs[n_in]
    g_ref = refs[n_in + 1] if mode != "max" else None
    c_ref = refs[n_in + 2] if mode == "attn_shift" else None
    o_ref, num_ref, den_ref = refs[-3:]
    n = pl.program_id(1)
    last = pl.num_programs(1) - 1
    col = lax.broadcasted_iota(jnp.int32, (1, BAND_SPAN), 1)
    edge_lo = jnp.where((n == 0) & (col < BAND_HALF), NEG_INF, 0.0).astype(F32)
    edge_hi = jnp.where((n == last) & (col >= BAND_BLOCK + BAND_HALF), NEG_INF, 0.0).astype(F32)
    lane = lax.broadcasted_iota(jnp.int32, (BAND_BLOCK, LANES), 1)
    first_half = lane < HEAD_DIM
    ones = jnp.ones((BAND_SPAN, LANES), BF16)

    def band(main, prev, nxt, blk, n_blk, ls):
        lo = blk * BAND_BLOCK - BAND_HALF
        parts = []
        if blk == 0:
            parts.append(prev[0, :, ls])
        start = max(lo, 0)
        stop = min(lo + BAND_SPAN, n_blk * BAND_BLOCK)
        parts.append(main[0, start:stop, ls])
        if blk == n_blk - 1:
            parts.append(nxt[0, :, ls])
        return parts[0] if len(parts) == 1 else jnp.concatenate(parts, axis=0)

    units = []
    for pat in reversed(range(len(DILATIONS))):
        dil = DILATIONS[pat]
        q_ref, k_ref, kp_ref, kn_ref, v_ref, vp_ref, vn_ref = ins[7 * pat:7 * pat + 7]
        n_blk = SUPER_BLOCK // (dil * BAND_BLOCK)
        for r in range(dil):
            for blk in range(n_blk):
                units.append((pat, dil, r, blk, n_blk, q_ref, k_ref, kp_ref, kn_ref, v_ref, vp_ref, vn_ref))

    def stage_scores(u):
        pat, dil, r, blk, n_blk, q_ref, k_ref, kp_ref, kn_ref = u[:9]
        ls = slice(r * LANES, (r + 1) * LANES)
        q = q_ref[0, blk * BAND_BLOCK:(blk + 1) * BAND_BLOCK, ls]
        kband = band(k_ref, kp_ref, kn_ref, blk, n_blk, ls)
        zero = jnp.zeros_like(q)
        return [_nt_dot(jnp.where(first_half, q, zero), kband), _nt_dot(jnp.where(first_half, zero, q), kband)]

    def stage_values(u, s_pair):
        pat, dil, r, blk, n_blk = u[:5]
        v_ref, vp_ref, vn_ref = u[9:12]
        ls = slice(r * LANES, (r + 1) * LANES)
        if dil == 1:
            rows = slice(blk * BAND_BLOCK, (blk + 1) * BAND_BLOCK)
        else:
            rows = pl.ds(dil * blk * BAND_BLOCK + r, BAND_BLOCK, stride=dil)
        def biased(half):
            s = s_pair[half] + bias_ref[pat, half]
            if blk == 0:
                s = s + edge_lo
            if blk == n_blk - 1:
                s = s + edge_hi
            return s

        if mode == "max":
            mx = jnp.where(first_half, biased(0).max(axis=-1, keepdims=True), biased(1).max(axis=-1, keepdims=True))
            if dil == DILATIONS[-1]:
                num_ref[rows, :] = mx
            elif dil > 1:
                num_ref[rows, :] = jnp.maximum(num_ref[rows, :], mx)
            else:
                o_ref[0, rows, :] = jnp.maximum(num_ref[rows, :], mx)
            return
        vext = jnp.concatenate([band(v_ref, vp_ref, vn_ref, blk, n_blk, ls), ones], axis=1)
        outs = []
        for half in range(2):
            s = biased(half)
            if mode == "attn_shift":
                s = s - c_ref[0, rows, :][:, half * HEAD_DIM:half * HEAD_DIM + 1]
            outs.append(_dot(jnp.exp(s).astype(BF16), vext))
        num = jnp.where(first_half, outs[0][:, :LANES], outs[1][:, :LANES])
        den = jnp.where(first_half, outs[0][:, LANES:], outs[1][:, LANES:])
        if dil == DILATIONS[-1]:
            num_ref[rows, :] = num
            den_ref[rows, :] = den
        elif dil > 1:
            num_ref[rows, :] = num_ref[rows, :] + num
            den_ref[rows, :] = den_ref[rows, :] + den
        else:
            y = (num_ref[rows, :] + num) / (den_ref[rows, :] + den)
            o_ref[0, rows, :] = (y * _silu(g_ref[0, rows, :].astype(F32))).astype(BF16)

    pending = [stage_scores(u) for u in units[:LOOKAHEAD]]
    for t, u in enumerate(units):
        if t + LOOKAHEAD < len(units):
            pending.append(stage_scores(units[t + LOOKAHEAD]))
        stage_values(u, pending[t])
        pending[t] = None


def _band_call(layouts, bias, gate, shift=None, mode="attn"):
    B, S, _ = layouts[0][0].shape
    n_sb = S // SUPER_BLOCK
    in_specs, args = [], []
    for (q, k, v), dil in zip(layouts, DILATIONS):
        rows = SUPER_BLOCK // dil
        width = dil * LANES
        n_halo = (S // dil) // BAND_HALF
        per_sb = rows // BAND_HALF
        main = pl.BlockSpec((1, rows, width), lambda b, n, pr: (b, n, pr))
        prev = pl.BlockSpec((1, BAND_HALF, width),
                            lambda b, n, pr, per_sb=per_sb: (b, jnp.maximum(n * per_sb - 1, 0), pr))
        nxt = pl.BlockSpec((1, BAND_HALF, width),
                           lambda b, n, pr, per_sb=per_sb, n_halo=n_halo: (b, jnp.minimum((n + 1) * per_sb, n_halo - 1), pr))
        in_specs += [main, main, prev, nxt, main, prev, nxt]
        args += [q, k, k, k, v, v, v]
    in_specs.append(pl.BlockSpec((len(DILATIONS), 2, BAND_BLOCK, BAND_SPAN), lambda b, n, pr: (0, pr, 0, 0)))
    args.append(bias)
    tok_blk = pl.BlockSpec((1, SUPER_BLOCK, LANES), lambda b, n, pr: (b, n, pr))
    if mode != "max":
        in_specs.append(tok_blk)
        args.append(gate)
    if mode == "attn_shift":
        in_specs.append(tok_blk)
        args.append(shift)
    return pl.pallas_call(
        functools.partial(_band_kernel, mode=mode),
        grid=(B, n_sb, N_PAIRS),
        in_specs=in_specs,
        out_specs=tok_blk,
        out_shape=jax.ShapeDtypeStruct((B, S, D_A), F32 if mode == "max" else BF16),
        scratch_shapes=[pltpu.VMEM((SUPER_BLOCK, LANES), F32)] * 2,
        compiler_params=pltpu.CompilerParams(
            dimension_semantics=("arbitrary", "arbitrary", "arbitrary"), vmem_limit_bytes=VMEM_LIMIT),
        name="band_" + mode,
    )(*args)


def _out_kernel(x_ref, p_ref, ya_ref, ybt_ref, wo_ref, wple_ref, wgate_ref, o_ref):
    tm = x_ref.shape[1]
    yb = ybt_ref[0].reshape(D_B, tm).astype(F32).T.astype(BF16)
    x1 = x_ref[0] + _dot(ya_ref[0], wo_ref[:D_A, :]) + _dot(yb, wo_ref[D_A:, :])
    gate = jax.nn.sigmoid(_dot(x1.astype(BF16), wgate_ref[...]))
    ple = _dot(p_ref[0].astype(BF16), wple_ref[...])
    o_ref[0] = x1 + ple * gate


def _out_call(x, p, ya, ybt, wo, wple, wgate):
    B, S, D = x.shape
    tm = ROW_TILE
    full = lambda shape: pl.BlockSpec(shape, lambda b, i: (0,) * len(shape))
    tok = lambda c: pl.BlockSpec((1, tm, c), lambda b, i: (b, i, 0))
    return pl.pallas_call(
        _out_kernel,
        grid=(B, S // tm),
        in_specs=[tok(D), tok(p.shape[-1]), tok(D_A),
                  pl.BlockSpec((1, N_HEADS_B, HEAD_DIM, tm), lambda b, i: (b, 0, 0, i)),
                  full(wo.shape), full(wple.shape), full(wgate.shape)],
        out_specs=tok(D),
        out_shape=jax.ShapeDtypeStruct((B, S, D), F32),
        compiler_params=pltpu.CompilerParams(
            dimension_semantics=("arbitrary", "arbitrary"), vmem_limit_bytes=VMEM_LIMIT),
        name="out_proj",
    )(x, p, ya, ybt, wo, wple, wgate)


def _rope_tables(S):
    n = HEAD_DIM // 4
    inv = ROPE_THETA ** (-jnp.arange(n, dtype=F32) / n)
    rows = S // GRID_W
    ang_row = inv[:, None] * jnp.arange(rows, dtype=jnp.int32).astype(F32)[None, :]
    ang_col = inv[:, None] * jnp.arange(GRID_W, dtype=jnp.int32).astype(F32)[None, :]
    expand = lambda fn: jnp.concatenate(
        [jnp.repeat(fn(ang_row), GRID_W, axis=1), jnp.tile(fn(ang_col), (1, rows))], axis=0)
    return expand(jnp.cos), expand(jnp.sin)


def _pairs_apart(t):
    lead, tail = t.shape[:-2], t.shape[-1]
    heads = t.shape[-2] // HEAD_DIM
    t = t.reshape(*lead, heads, HEAD_DIM // 2, 2, tail)
    return jnp.swapaxes(t, -3, -2).reshape(*lead, heads * HEAD_DIM, tail)


def kernel(x, p, ln_g, w_in, qn_a, kn_a, qn_b, kn_b, w_out, w_ple, w_pgate, rel_bias):
    B, S, D = x.shape
    depth = w_in.shape[0]
    cos_t, sin_t = _rope_tables(S)
    bias = _bias_call(rel_bias)
    for i in range(depth):
        wfeat = w_in[i].T.astype(BF16)
        wrope = _pairs_apart(wfeat[4 * D_A:4 * D_A + D_B + D_KV_B])
        wtok = w_in[i][:, 2 * D_A:4 * D_A].astype(BF16)
        gains_b = _pairs_apart(jnp.stack([qn_b[i], kn_b[i]], axis=-1))
        gains = jnp.concatenate([jnp.stack([qn_a[i], kn_a[i]]), gains_b.T]).astype(F32)
        gains = jnp.broadcast_to(gains[:, :, None], (4, HEAD_DIM, ROW_TILE // PROJ_SUBTILES))

        outs = _proj_call(
            x, ln_g[i][None, :], wtok, wfeat, wrope, gains, cos_t, sin_t)
        ga, gbt, q1, k1, v1, qbt, kb, vbt = outs

        gmax = jnp.max(jnp.abs(jnp.stack([qn_a[i], kn_a[i], qn_b[i], kn_b[i]]).astype(F32)), axis=1)
        root = HEAD_DIM ** 0.5
        bound_a = root * gmax[0] * gmax[1] + jnp.max(jnp.abs(rel_bias.astype(F32)))
        bound_b = root * gmax[2] * gmax[3]
        safe = jnp.maximum(bound_a, bound_b) <= SAFE_EXPONENT
        def band_layouts(qkv, lay):
            (q1, k1, v1), (q4, q16, k4, k16, v4, v16) = qkv, lay
            return ((q1, k1, v1), (q4, k4, v4), (q16, k16, v16))

        def unshifted(ops):
            qbt, kb, vbt, gbt, qkv, bias, ga = ops
            ybt, lay = _gqa_call(qbt, kb, vbt, gbt, qkv)
            return ybt, _band_call(band_layouts(qkv, lay), bias, ga)

        def shifted(ops):
            qbt, kb, vbt, gbt, qkv, bias, ga = ops
            shift_b = _shift_rows(_gqa_call(qbt, kb, vbt, gbt, qkv, mode="max"))
            ybt, lay = _gqa_call(qbt, kb, vbt, gbt, qkv, shift_b, mode="attn_shift")
            layouts = band_layouts(qkv, lay)
            shift_a = _band_call(layouts, bias, ga, mode="max")
            return ybt, _band_call(layouts, bias, ga, shift_a, mode="attn_shift")

        ybt, ya = lax.cond(safe, unshifted, shifted, (qbt, kb, vbt, gbt, (q1, k1, v1), bias, ga))
        x = _out_call(x, p[i], ya, ybt, w_out[i].astype(BF16), w_ple[i].astype(BF16), w_pgate[i].astype(BF16))
    return x
```

```python
import functools
import math

import numpy as np
import jax
import jax.numpy as jnp
from jax import lax
from jax.experimental import pallas as pl
from jax.experimental.pallas import tpu as pltpu

HEAD_DIM = 64
N_HEADS_A = 8
N_HEADS_B = 8
N_KV_B = 2
GQA_GROUP = N_HEADS_B // N_KV_B
D_A = N_HEADS_A * HEAD_DIM
D_B = N_HEADS_B * HEAD_DIM
D_KV_B = N_KV_B * HEAD_DIM
DILATIONS = (1, 4, 16)
BAND_BLOCK = 128
BAND_HALF = 64
BAND_SPAN = BAND_BLOCK + 2 * BAND_HALF
SUPER_BLOCK = BAND_BLOCK * max(DILATIONS)
LANES = 128
N_PAIRS = D_A // LANES
PAIRS_PER_STEP = 2
GRID_W = 64
ROPE_THETA = 10000.0
N_BUCKETS = 32
MAX_DISTANCE = 1024
EPS = 1e-6
NEG_INF = -1e30

ROW_TILE = 1024
PROJ_SUBTILES = 4
Q_TILE = 512
Q_TILES_PER_STEP = 2
KEY_STEP = 256
LOOKAHEAD = 2
VMEM_LIMIT = 56 * 1024 * 1024
SAFE_EXPONENT = 60.0
SHIFT_TERMS = 3
SHIFT_ROWS = 16

F32 = jnp.float32
BF16 = jnp.bfloat16


def _nt_dot(a, b):
    return lax.dot_general(a, b, (((1,), (1,)), ((), ())), preferred_element_type=F32)


def _dot(a, b):
    return jnp.dot(a, b, preferred_element_type=F32)


def _silu(g):
    return g * jax.nn.sigmoid(g)


def _head_rms(t, gain):
    ms = jnp.mean(t * t, axis=1, keepdims=True)
    return t * lax.rsqrt(ms + EPS) * gain[None]


def _rope_halves(t, cos, sin):
    half = HEAD_DIM // 2
    x1 = t[:, :half, :]
    x2 = t[:, half:, :]
    c = cos[None]
    s = sin[None]
    return jnp.concatenate([x1 * c - x2 * s, x1 * s + x2 * c], axis=1)


def _zero_from(tile, width):
    bits = pltpu.bitcast(tile, jnp.uint32)
    zero = lax.shift_right_logical(lax.shift_right_logical(bits, jnp.uint32(16)), jnp.uint32(16))
    return jnp.tile(pltpu.bitcast(zero, F32), (1, width // LANES))


def _qkv_lanes(pr, dil, r, j):
    lo = ((pr * dil + r) * 3 + j) * LANES
    return slice(lo, lo + LANES)


def _residue_layout_jobs(src, j, scr, o4, o16):
    t = src.shape[1]

    def fill(pr):
        scr[pr] = src[0, :, _qkv_lanes(pr, 1, 0, j)].astype(F32)
        return None

    def gather(pr, dil, out, r):
        lanes = _qkv_lanes(pr, dil, r, j)
        lo = lanes.start
        out[0, :, lanes] = scr[pr, pl.ds(r, t // dil, stride=dil), :].astype(BF16)
        row = pl.multiple_of(jnp.minimum(pl.program_id(0), 0), 16)
        return out[0, pl.ds(row, 16), lo:lo + LANES]

    jobs = [functools.partial(fill, pr) for pr in range(N_PAIRS)]
    for pr in range(N_PAIRS):
        for dil, out in ((4, o4), (16, o16)):
            jobs += [functools.partial(gather, pr, dil, out, r) for r in range(dil)]
    return jobs


def _proj_kernel(x_ref, g_ref, wtok_ref, wfeat_ref, wrope_ref, gains_ref, cos_ref, sin_ref,
                 ga_ref, gbt_ref, qkv_ref, qbt_ref, kb_ref, vbt_ref):
    for sub in range(PROJ_SUBTILES):
        _proj_subtile(sub, x_ref, g_ref, wtok_ref, wfeat_ref, wrope_ref, gains_ref, cos_ref, sin_ref,
                      ga_ref, gbt_ref, qkv_ref, qbt_ref, kb_ref, vbt_ref)


def _proj_subtile(sub, x_ref, g_ref, wtok_ref, wfeat_ref, wrope_ref, gains_ref, cos_ref, sin_ref,
                  ga_ref, gbt_ref, qkv_ref, qbt_ref, kb_ref, vbt_ref):
    t = x_ref.shape[1] // PROJ_SUBTILES
    row0 = sub * t
    rows = slice(row0, row0 + t)
    scale = HEAD_DIM ** -0.5
    x = x_ref[0, rows, :]
    ms = jnp.mean(x * x, axis=-1, keepdims=True)
    h = (x * lax.rsqrt(ms + EPS) * g_ref[...]).astype(BF16)

    def store_pairs(j, y):
        for pr in range(N_PAIRS):
            qkv_ref[0, rows, _qkv_lanes(pr, 1, 0, j)] = y[:, pr * LANES:(pr + 1) * LANES].astype(BF16)

    qa = _nt_dot(wfeat_ref[:D_A, :], h).reshape(N_HEADS_A, HEAD_DIM, t)
    qa = _head_rms(qa, gains_ref[0]) * scale
    store_pairs(0, qa.reshape(D_A, t).T)
    ka = _nt_dot(wfeat_ref[D_A:2 * D_A, :], h).reshape(N_HEADS_A, HEAD_DIM, t)
    ka = _head_rms(ka, gains_ref[1])
    store_pairs(1, ka.reshape(D_A, t).T)
    store_pairs(2, _dot(h, wtok_ref[:, :D_A]))
    ga_ref[0, rows, :] = _dot(h, wtok_ref[:, D_A:]).astype(BF16)

    rope = _nt_dot(wrope_ref[...], h)
    o = 4 * D_A + D_B + D_KV_B
    vg = _nt_dot(wfeat_ref[o:, :], h)
    gbt_ref[0, :, :, rows] = vg[D_KV_B:].reshape(N_HEADS_B, HEAD_DIM, t).astype(BF16)
    cos = cos_ref[:, rows]
    sin = sin_ref[:, rows]
    qb = _head_rms(rope[:D_B].reshape(N_HEADS_B, HEAD_DIM, t), gains_ref[2])
    qbt_ref[0, :, :, rows] = (_rope_halves(qb, cos, sin) * scale).astype(BF16)
    kb = _head_rms(rope[D_B:].reshape(N_KV_B, HEAD_DIM, t), gains_ref[3])
    kb = _rope_halves(kb, cos, sin)
    ext = (lax.broadcasted_iota(jnp.int32, (HEAD_DIM, t), 0) < SHIFT_TERMS).astype(F32)
    for kv in range(N_KV_B):
        kb_ref[0, kv, rows, :] = jnp.concatenate([kb[kv], ext], axis=0).T.astype(BF16)
    vbt_ref[0, :, 0, :, rows] = vg[:D_KV_B].reshape(N_KV_B, HEAD_DIM, t).astype(BF16)


def _proj_call(x, ln_g, wtok, wfeat, wrope, gains, cos_t, sin_t):
    B, S, D = x.shape
    tm = ROW_TILE
    nt = S // tm
    full = lambda shape: pl.BlockSpec(shape, lambda b, i: (0,) * len(shape))
    tok_spec = pl.BlockSpec((1, tm, D_A), lambda b, i: (b, i, 0))
    tok_shape = jax.ShapeDtypeStruct((B, S, D_A), BF16)
    out_shape = (
        tok_shape,
        jax.ShapeDtypeStruct((B, N_HEADS_B, HEAD_DIM, S), BF16),
        jax.ShapeDtypeStruct((B, S, 3 * D_A), BF16),
        jax.ShapeDtypeStruct((B, N_HEADS_B, HEAD_DIM, S), BF16),
        jax.ShapeDtypeStruct((B, N_KV_B, S, 2 * HEAD_DIM), BF16),
        jax.ShapeDtypeStruct((B, N_KV_B, nt, HEAD_DIM, tm), BF16),
    )
    out_specs = (
        tok_spec, pl.BlockSpec((1, N_HEADS_B, HEAD_DIM, tm), lambda b, i: (b, 0, 0, i)),
        pl.BlockSpec((1, tm, 3 * D_A), lambda b, i: (b, i, 0)),
        pl.BlockSpec((1, N_HEADS_B, HEAD_DIM, tm), lambda b, i: (b, 0, 0, i)),
        pl.BlockSpec((1, N_KV_B, tm, 2 * HEAD_DIM), lambda b, i: (b, 0, i, 0)),
        pl.BlockSpec((1, N_KV_B, 1, HEAD_DIM, tm), lambda b, i: (b, 0, i, 0, 0)),
    )
    in_specs = [
        pl.BlockSpec((1, tm, D), lambda b, i: (b, i, 0)),
        full((1, D)),
        full(wtok.shape),
        full(wfeat.shape),
        full(wrope.shape),
        full(gains.shape),
        pl.BlockSpec((HEAD_DIM // 2, tm), lambda b, i: (0, i)),
        pl.BlockSpec((HEAD_DIM // 2, tm), lambda b, i: (0, i)),
    ]
    return pl.pallas_call(
        _proj_kernel,
        grid=(B, nt),
        in_specs=in_specs,
        out_specs=out_specs,
        out_shape=out_shape,
        compiler_params=pltpu.CompilerParams(
            dimension_semantics=("arbitrary", "arbitrary"), vmem_limit_bytes=VMEM_LIMIT),
        name="proj",
    )(x, ln_g, wtok, wfeat, wrope, gains, cos_t, sin_t)


def _gqa_kernel(*refs, mode):
    if mode == "max":
        qt_ref, k_ref, o_ref = refs
    else:
        n_in = 5 if mode == "attn_shift" else 4
        qt_ref, k_ref, vt_ref, gt_ref = refs[:4]
        c_ref = refs[4] if mode == "attn_shift" else None
        qkv_ref, o_ref, qkv4_ref, qkv16_ref = refs[n_in:n_in + 4]
        scr_refs = refs[n_in + 4:]
        side_jobs = []
        for j in range(3):
            side_jobs += _residue_layout_jobs(qkv_ref, j, scr_refs[j], qkv4_ref, qkv16_ref)
    tq = Q_TILE
    n_steps = k_ref.shape[2] // KEY_STEP
    sub = 8
    heads = [(g, slice(j * tq, (j + 1) * tq)) for j in range(qt_ref.shape[3] // tq) for g in range(GQA_GROUP)]
    qext = {}
    for g, cols in heads:
        q = qt_ref[0, g, :, cols]
        if mode == "attn_shift":
            pad = jnp.zeros((HEAD_DIM - SHIFT_ROWS, tq), BF16)
            qext[g, cols.start] = jnp.concatenate([q, c_ref[0, g, :, cols], pad], axis=0)
        else:
            qext[g, cols.start] = jnp.concatenate([q, jnp.zeros_like(q)], axis=0)
    units = [(g, cols, t) for g, cols in heads for t in range(n_steps)]

    def scores(u):
        g, cols, t = u
        return _dot(k_ref[0, 0, t * KEY_STEP:(t + 1) * KEY_STEP, :], qext[g, cols.start])

    pending = [scores(u) for u in units[:LOOKAHEAD]]
    for i, (g, cols, t) in enumerate(units):
        if i + LOOKAHEAD < len(units):
            pending.append(scores(units[i + LOOKAHEAD]))
        s = pending[i]
        pending[i] = None
        if mode == "max":
            part = s.reshape(KEY_STEP // sub, sub, tq).max(axis=0)
            mx = part if t == 0 else jnp.maximum(mx, part)
            if t == n_steps - 1:
                o_ref[0, g, :, cols] = jnp.broadcast_to(mx.max(axis=0, keepdims=True), (sub, tq))
            continue
        if t == 0:
            acc = jnp.zeros((HEAD_DIM, tq), F32)
            den = jnp.zeros((sub, tq), F32)
        p = jnp.exp(s)
        den = den + p.reshape(KEY_STEP // sub, sub, tq).sum(axis=0)
        if side_jobs:
            back = side_jobs.pop(0)()
            if back is not None:
                den = den + _zero_from(back, tq)
        tk = vt_ref.shape[4]
        lo = (t * KEY_STEP) % tk
        acc = acc + _dot(vt_ref[0, 0, (t * KEY_STEP) // tk, :, lo:lo + KEY_STEP], p.astype(BF16))
        if t == n_steps - 1:
            y = acc / den.sum(axis=0, keepdims=True)
            o_ref[0, g, :, cols] = (y * _silu(gt_ref[0, g, :, cols].astype(F32))).astype(BF16)
    if mode != "max":
        for job in side_jobs:
            job()


def _gqa_call(qbt, kb, vbt, gbt, window_qkv, shift=None, mode="attn"):
    B, H, _, S = qbt.shape
    tq = Q_TILE * min(Q_TILES_PER_STEP, S // Q_TILE)
    n_q = S // tq
    nt, rows, tk = vbt.shape[2], vbt.shape[3], vbt.shape[4]
    head_blk = lambda r: pl.BlockSpec((1, GQA_GROUP, r, tq), lambda b, kv, i: (b, kv, 0, i))
    in_specs = [head_blk(HEAD_DIM), pl.BlockSpec((1, 1, S, 2 * HEAD_DIM), lambda b, kv, i: (b, kv, 0, 0))]
    args = [qbt, kb]
    grid = (B, N_KV_B, n_q)
    params = pltpu.CompilerParams(
        dimension_semantics=("arbitrary", "arbitrary", "arbitrary"), vmem_limit_bytes=VMEM_LIMIT)
    if mode == "max":
        return pl.pallas_call(
            functools.partial(_gqa_kernel, mode=mode), grid=grid, in_specs=in_specs, out_specs=head_blk(8),
            out_shape=jax.ShapeDtypeStruct((B, H, 8, S), F32), compiler_params=params, name="gqa_" + mode,
        )(*args)
    in_specs += [pl.BlockSpec((1, 1, nt, rows, tk), lambda b, kv, i: (b, kv, 0, 0, 0)), head_blk(HEAD_DIM)]
    args += [vbt, gbt]
    if mode == "attn_shift":
        in_specs.append(head_blk(SHIFT_ROWS))
        args.append(shift)
    t = S // (N_KV_B * n_q)
    tile = lambda b, kv, i: (b, kv * n_q + i, 0)
    in_specs.append(pl.BlockSpec((1, t, 3 * D_A), tile))
    args.append(window_qkv)
    out_specs, out_shape = [head_blk(HEAD_DIM)], [jax.ShapeDtypeStruct((B, H, HEAD_DIM, S), BF16)]
    for dil in DILATIONS[1:]:
        out_specs.append(pl.BlockSpec((1, t // dil, dil * 3 * D_A), tile))
        out_shape.append(jax.ShapeDtypeStruct((B, S // dil, dil * 3 * D_A), BF16))
    outs = pl.pallas_call(
        functools.partial(_gqa_kernel, mode=mode), grid=grid, in_specs=in_specs, out_specs=out_specs,
        out_shape=out_shape, scratch_shapes=[pltpu.VMEM((N_PAIRS, t, LANES), F32)] * 3,
        compiler_params=params, name="gqa_" + mode,
    )(*args)
    return outs[0], outs[1:]


def _shift_rows(m):
    rest = m[:, :, :1, :]
    terms = []
    for _ in range(SHIFT_TERMS):
        t = rest.astype(BF16)
        terms.append(-t)
        rest = rest - t.astype(F32)
    pad = jnp.zeros(m.shape[:2] + (SHIFT_ROWS - SHIFT_TERMS, m.shape[3]), BF16)
    return jnp.concatenate(terms + [pad], axis=2)


def _band_bucket_index():
    nb = N_BUCKETS // 2
    max_exact = nb // 2
    rel = np.arange(BAND_SPAN)[None, :] - BAND_HALF - np.arange(BAND_BLOCK)[:, None]
    out = []
    for dilation in DILATIONS:
        dist = rel * dilation
        ret = np.where(dist > 0, nb, 0)
        n = np.abs(dist)
        nf = np.maximum(n, max_exact).astype(np.float32)
        large = max_exact + (np.log(nf / np.float32(max_exact)) / np.float32(math.log(MAX_DISTANCE / max_exact))
                             * np.float32(nb - max_exact)).astype(np.int32)
        large = np.minimum(large, nb - 1)
        bucket = ret + np.where(n < max_exact, n, large)
        out.append(np.where(np.abs(rel) <= BAND_HALF, bucket, -1))
    return np.stack(out).astype(np.int32)


def _bias_kernel(table_ref, idx_ref, o_ref):
    chunk = 16
    index = _band_bucket_index()
    for pat in range(len(DILATIONS)):
        present = [int(b) for b in np.unique(index[pat]) if b >= 0]
        for lo in range(0, BAND_BLOCK, chunk):
            idx = idx_ref[pat, lo:lo + chunk, :]
            acc = [jnp.full(idx.shape, NEG_INF, F32) for _ in range(N_HEADS_A)]
            for b in present:
                hit = idx == b
                for h in range(N_HEADS_A):
                    acc[h] = jnp.where(hit, table_ref[b, h], acc[h])
            for h in range(N_HEADS_A):
                o_ref[pat, h, lo:lo + chunk, :] = acc[h]


def _bias_call(rel_bias):
    idx = jnp.asarray(_band_bucket_index())
    n_pat = len(DILATIONS)
    return pl.pallas_call(
        _bias_kernel,
        in_specs=[pl.BlockSpec(memory_space=pltpu.SMEM), pl.BlockSpec(memory_space=pltpu.VMEM)],
        out_specs=pl.BlockSpec(memory_space=pltpu.VMEM),
        out_shape=jax.ShapeDtypeStruct((n_pat, N_HEADS_A, BAND_BLOCK, BAND_SPAN), F32),
        name="bias_bands",
    )(rel_bias, idx)


def _band_kernel(*refs, mode):
    n_in = 3 * len(DILATIONS)
    ins, bias_ref = refs[:n_in], refs[n_in]
    g_ref = refs[n_in + 1] if mode != "max" else None
    c_ref = refs[n_in + 2] if mode == "attn_shift" else None
    o_ref, nums_ref, dens_ref = refs[-3:]
    n = pl.program_id(1)
    last = pl.num_programs(1) - 1
    col = lax.broadcasted_iota(jnp.int32, (1, BAND_SPAN), 1)
    edge_lo = jnp.where((n == 0) & (col < BAND_HALF), NEG_INF, 0.0).astype(F32)
    edge_hi = jnp.where((n == last) & (col >= BAND_BLOCK + BAND_HALF), NEG_INF, 0.0).astype(F32)
    lane = lax.broadcasted_iota(jnp.int32, (BAND_BLOCK, LANES), 1)
    first_half = lane < HEAD_DIM
    ones = jnp.ones((BAND_SPAN, LANES), BF16)

    def band(main, prev, nxt, blk, n_blk, ls):
        lo = blk * BAND_BLOCK - BAND_HALF
        parts = []
        if blk == 0:
            parts.append(prev[0, :, ls])
        start = max(lo, 0)
        stop = min(lo + BAND_SPAN, n_blk * BAND_BLOCK)
        parts.append(main[0, start:stop, ls])
        if blk == n_blk - 1:
            parts.append(nxt[0, :, ls])
        return parts[0] if len(parts) == 1 else jnp.concatenate(parts, axis=0)

    units = []
    for pp in range(nums_ref.shape[0]):
        for pat in reversed(range(len(DILATIONS))):
            dil = DILATIONS[pat]
            main_ref, prev_ref, next_ref = ins[3 * pat:3 * pat + 3]
            n_blk = SUPER_BLOCK // (dil * BAND_BLOCK)
            for r in range(dil):
                for blk in range(n_blk):
                    units.append((pat, dil, r, blk, n_blk, main_ref, prev_ref, next_ref, pp))

    def stage_scores(u):
        pat, dil, r, blk, n_blk, main_ref, prev_ref, next_ref, pp = u
        q = main_ref[0, blk * BAND_BLOCK:(blk + 1) * BAND_BLOCK, _qkv_lanes(pp, dil, r, 0)]
        kband = band(main_ref, prev_ref, next_ref, blk, n_blk, _qkv_lanes(pp, dil, r, 1))
        zero = jnp.zeros_like(q)
        return [_nt_dot(jnp.where(first_half, q, zero), kband), _nt_dot(jnp.where(first_half, zero, q), kband)]

    def stage_values(u, s_pair):
        pat, dil, r, blk, n_blk, main_ref, prev_ref, next_ref, pp = u
        out_ls = slice(pp * LANES, (pp + 1) * LANES)
        num_ref, den_ref = nums_ref.at[pp], dens_ref.at[pp]
        if dil == 1:
            rows = slice(blk * BAND_BLOCK, (blk + 1) * BAND_BLOCK)
        else:
            rows = pl.ds(dil * blk * BAND_BLOCK + r, BAND_BLOCK, stride=dil)
        def biased(half):
            s = s_pair[half] + bias_ref[pat, 2 * pp + half]
            if blk == 0:
                s = s + edge_lo
            if blk == n_blk - 1:
                s = s + edge_hi
            return s

        if mode == "max":
            mx = jnp.where(first_half, biased(0).max(axis=-1, keepdims=True), biased(1).max(axis=-1, keepdims=True))
            if dil == DILATIONS[-1]:
                num_ref[rows, :] = mx
            elif dil > 1:
                num_ref[rows, :] = jnp.maximum(num_ref[rows, :], mx)
            else:
                o_ref[0, pp, rows, :] = jnp.maximum(num_ref[rows, :], mx)
            return
        vband = band(main_ref, prev_ref, next_ref, blk, n_blk, _qkv_lanes(pp, dil, r, 2))
        vext = jnp.concatenate([vband, ones], axis=1)
        outs = []
        for half in range(2):
            s = biased(half)
            if mode == "attn_shift":
                s = s - c_ref[0, pp, rows, :][:, half * HEAD_DIM:half * HEAD_DIM + 1]
            outs.append(_dot(jnp.exp(s).astype(BF16), vext))
        num = jnp.where(first_half, outs[0][:, :LANES], outs[1][:, :LANES])
        den = jnp.where(first_half, outs[0][:, LANES:], outs[1][:, LANES:])
        if dil == DILATIONS[-1]:
            num_ref[rows, :] = num
            den_ref[rows, :] = den
        elif dil > 1:
            num_ref[rows, :] = num_ref[rows, :] + num
            den_ref[rows, :] = den_ref[rows, :] + den
        else:
            y = (num_ref[rows, :] + num) / (den_ref[rows, :] + den)
            o_ref[0, rows, out_ls] = (y * _silu(g_ref[0, rows, out_ls].astype(F32))).astype(BF16)

    pending = [stage_scores(u) for u in units[:LOOKAHEAD]]
    for t, u in enumerate(units):
        if t + LOOKAHEAD < len(units):
            pending.append(stage_scores(units[t + LOOKAHEAD]))
        stage_values(u, pending[t])
        pending[t] = None


def _band_call(layouts, bias, gate, shift=None, mode="attn"):
    B, S, _ = layouts[0].shape
    n_sb = S // SUPER_BLOCK
    pp = PAIRS_PER_STEP
    in_specs, args = [], []
    for qkv, dil in zip(layouts, DILATIONS):
        rows = SUPER_BLOCK // dil
        width = pp * dil * 3 * LANES
        n_halo = (S // dil) // BAND_HALF
        per_sb = rows // BAND_HALF
        main = pl.BlockSpec((1, rows, width), lambda b, n, pr: (b, n, pr))
        prev = pl.BlockSpec((1, BAND_HALF, width),
                            lambda b, n, pr, per_sb=per_sb: (b, jnp.maximum(n * per_sb - 1, 0), pr))
        nxt = pl.BlockSpec((1, BAND_HALF, width),
                           lambda b, n, pr, per_sb=per_sb, n_halo=n_halo: (b, jnp.minimum((n + 1) * per_sb, n_halo - 1), pr))
        in_specs += [main, prev, nxt]
        args += [qkv, qkv, qkv]
    in_specs.append(pl.BlockSpec((len(DILATIONS), 2 * pp, BAND_BLOCK, BAND_SPAN), lambda b, n, pr: (0, pr, 0, 0)))
    args.append(bias)
    tok_blk = pl.BlockSpec((1, SUPER_BLOCK, pp * LANES), lambda b, n, pr: (b, n, pr))
    shift_blk = pl.BlockSpec((1, pp, SUPER_BLOCK, LANES), lambda b, n, pr: (b, pr, n, 0))
    if mode != "max":
        in_specs.append(tok_blk)
        args.append(gate)
    if mode == "attn_shift":
        in_specs.append(shift_blk)
        args.append(shift)
    if mode == "max":
        out_specs, out_shape = shift_blk, jax.ShapeDtypeStruct((B, N_PAIRS, S, LANES), F32)
    else:
        out_specs, out_shape = tok_blk, jax.ShapeDtypeStruct((B, S, D_A), BF16)
    return pl.pallas_call(
        functools.partial(_band_kernel, mode=mode),
        grid=(B, n_sb, N_PAIRS // pp),
        in_specs=in_specs,
        out_specs=out_specs,
        out_shape=out_shape,
        scratch_shapes=[pltpu.VMEM((pp, SUPER_BLOCK, LANES), F32)] * 2,
        compiler_params=pltpu.CompilerParams(
            dimension_semantics=("arbitrary", "arbitrary", "arbitrary"), vmem_limit_bytes=VMEM_LIMIT),
        name="band_" + mode,
    )(*args)


def _out_kernel(x_ref, p_ref, ya_ref, ybt_ref, wo_ref, wple_ref, wgate_ref, o_ref):
    tm = x_ref.shape[1]
    yb = ybt_ref[0].reshape(D_B, tm).astype(F32).T.astype(BF16)
    x1 = x_ref[0] + _dot(ya_ref[0], wo_ref[:D_A, :]) + _dot(yb, wo_ref[D_A:, :])
    gate = jax.nn.sigmoid(_dot(x1.astype(BF16), wgate_ref[...]))
    ple = _dot(p_ref[0].astype(BF16), wple_ref[...])
    o_ref[0] = x1 + ple * gate


def _out_call(x, p, ya, ybt, wo, wple, wgate):
    B, S, D = x.shape
    tm = ROW_TILE
    full = lambda shape: pl.BlockSpec(shape, lambda b, i: (0,) * len(shape))
    tok = lambda c: pl.BlockSpec((1, tm, c), lambda b, i: (b, i, 0))
    return pl.pallas_call(
        _out_kernel,
        grid=(B, S // tm),
        in_specs=[tok(D), tok(p.shape[-1]), tok(D_A),
                  pl.BlockSpec((1, N_HEADS_B, HEAD_DIM, tm), lambda b, i: (b, 0, 0, i)),
                  full(wo.shape), full(wple.shape), full(wgate.shape)],
        out_specs=tok(D),
        out_shape=jax.ShapeDtypeStruct((B, S, D), F32),
        compiler_params=pltpu.CompilerParams(
            dimension_semantics=("arbitrary", "arbitrary"), vmem_limit_bytes=VMEM_LIMIT),
        name="out_proj",
    )(x, p, ya, ybt, wo, wple, wgate)


def _rope_tables(S):
    n = HEAD_DIM // 4
    inv = ROPE_THETA ** (-jnp.arange(n, dtype=F32) / n)
    rows = S // GRID_W
    ang_row = inv[:, None] * jnp.arange(rows, dtype=jnp.int32).astype(F32)[None, :]
    ang_col = inv[:, None] * jnp.arange(GRID_W, dtype=jnp.int32).astype(F32)[None, :]
    expand = lambda fn: jnp.concatenate(
        [jnp.repeat(fn(ang_row), GRID_W, axis=1), jnp.tile(fn(ang_col), (1, rows))], axis=0)
    return expand(jnp.cos), expand(jnp.sin)


def _pairs_apart(t):
    lead, tail = t.shape[:-2], t.shape[-1]
    heads = t.shape[-2] // HEAD_DIM
    t = t.reshape(*lead, heads, HEAD_DIM // 2, 2, tail)
    return jnp.swapaxes(t, -3, -2).reshape(*lead, heads * HEAD_DIM, tail)


def kernel(x, p, ln_g, w_in, qn_a, kn_a, qn_b, kn_b, w_out, w_ple, w_pgate, rel_bias):
    B, S, D = x.shape
    depth = w_in.shape[0]
    cos_t, sin_t = _rope_tables(S)
    bias = _bias_call(rel_bias)
    for i in range(depth):
        wfeat = w_in[i].T.astype(BF16)
        wrope = _pairs_apart(wfeat[4 * D_A:4 * D_A + D_B + D_KV_B])
        wtok = w_in[i][:, 2 * D_A:4 * D_A].astype(BF16)
        gains_b = _pairs_apart(jnp.stack([qn_b[i], kn_b[i]], axis=-1))
        gains = jnp.concatenate([jnp.stack([qn_a[i], kn_a[i]]), gains_b.T]).astype(F32)
        gains = jnp.broadcast_to(gains[:, :, None], (4, HEAD_DIM, ROW_TILE // PROJ_SUBTILES))

        outs = _proj_call(
            x, ln_g[i][None, :], wtok, wfeat, wrope, gains, cos_t, sin_t)
        ga, gbt, qkv1, qbt, kb, vbt = outs

        gmax = jnp.max(jnp.abs(jnp.stack([qn_a[i], kn_a[i], qn_b[i], kn_b[i]]).astype(F32)), axis=1)
        root = HEAD_DIM ** 0.5
        bound_a = root * gmax[0] * gmax[1] + jnp.max(jnp.abs(rel_bias.astype(F32)))
        bound_b = root * gmax[2] * gmax[3]
        safe = jnp.maximum(bound_a, bound_b) <= SAFE_EXPONENT

        def unshifted(ops):
            qbt, kb, vbt, gbt, qkv1, bias, ga = ops
            ybt, (qkv4, qkv16) = _gqa_call(qbt, kb, vbt, gbt, qkv1)
            return ybt, _band_call((qkv1, qkv4, qkv16), bias, ga)

        def shifted(ops):
            qbt, kb, vbt, gbt, qkv1, bias, ga = ops
            shift_b = _shift_rows(_gqa_call(qbt, kb, vbt, gbt, qkv1, mode="max"))
            ybt, (qkv4, qkv16) = _gqa_call(qbt, kb, vbt, gbt, qkv1, shift_b, mode="attn_shift")
            layouts = (qkv1, qkv4, qkv16)
            shift_a = _band_call(layouts, bias, ga, mode="max")
            return ybt, _band_call(layouts, bias, ga, shift_a, mode="attn_shift")

        ybt, ya = lax.cond(safe, unshifted, shifted, (qbt, kb, vbt, gbt, qkv1, bias, ga))
        x = _out_call(x, p[i], ya, ybt, w_out[i].astype(BF16), w_ple[i].astype(BF16), w_pgate[i].astype(BF16))
    return x
```

```python
import functools
import math

import numpy as np
import jax
import jax.numpy as jnp
from jax import lax
from jax.experimental import pallas as pl
from jax.experimental.pallas import tpu as pltpu

HEAD_DIM = 64
N_HEADS_A = 8
N_HEADS_B = 8
N_KV_B = 2
GQA_GROUP = N_HEADS_B // N_KV_B
D_A = N_HEADS_A * HEAD_DIM
D_B = N_HEADS_B * HEAD_DIM
D_KV_B = N_KV_B * HEAD_DIM
DILATIONS = (1, 4, 16)
BAND_BLOCK = 128
BAND_HALF = 64
BAND_SPAN = BAND_BLOCK + 2 * BAND_HALF
SUPER_BLOCK = BAND_BLOCK * max(DILATIONS)
LANES = 128
N_PAIRS = D_A // LANES
PAIRS_PER_STEP = 2
GRID_W = 64
ROPE_THETA = 10000.0
N_BUCKETS = 32
MAX_DISTANCE = 1024
EPS = 1e-6
NEG_INF = -1e30

ROW_TILE = 1024
PROJ_SUBTILES = 4
Q_TILE = 512
Q_TILES_PER_STEP = 2
KEY_STEP = 256
LOOKAHEAD = 2
V7X_VMEM_BYTES = 64 * 1024 * 1024
VMEM_LIMIT = V7X_VMEM_BYTES - 8 * 1024 * 1024
F32_SUBLANES = 8
LOG2_E = math.log2(math.e)
SAFE_EXPONENT = 60.0
SHIFT_TERMS = 3
SHIFT_ROWS = 16

F32 = jnp.float32
BF16 = jnp.bfloat16


def _nt_dot(a, b):
    return lax.dot_general(a, b, (((1,), (1,)), ((), ())), preferred_element_type=F32)


def _dot(a, b):
    return jnp.dot(a, b, preferred_element_type=F32)


def _silu(g):
    return g * jax.nn.sigmoid(g)


def _head_rms(t, gain):
    ms = jnp.mean(t * t, axis=1, keepdims=True)
    return t * lax.rsqrt(ms + EPS) * gain[None]


def _rope_halves(t, cos, sin):
    half = HEAD_DIM // 2
    x1 = t[:, :half, :]
    x2 = t[:, half:, :]
    c = cos[None]
    s = sin[None]
    return jnp.concatenate([x1 * c - x2 * s, x1 * s + x2 * c], axis=1)


def _zero_from(tile, width):
    bits = pltpu.bitcast(tile, jnp.uint32)
    zero = lax.shift_right_logical(lax.shift_right_logical(bits, jnp.uint32(16)), jnp.uint32(16))
    return jnp.tile(pltpu.bitcast(zero, F32), (1, width // LANES))


def _qkv_lanes(pr, dil, r, j):
    lo = ((pr * dil + r) * 3 + j) * LANES
    return slice(lo, lo + LANES)


def _residue_layout_jobs(src, j, scr, o4, o16):
    t = src.shape[1]

    def fill(pr):
        scr[pr] = src[0, :, _qkv_lanes(pr, 1, 0, j)].astype(F32)
        return None

    def gather(pr, dil, out, r):
        lanes = _qkv_lanes(pr, dil, r, j)
        lo = lanes.start
        out[0, :, lanes] = scr[pr, pl.ds(r, t // dil, stride=dil), :].astype(BF16)
        row = pl.multiple_of(jnp.minimum(pl.program_id(0), 0), 16)
        return out[0, pl.ds(row, 16), lo:lo + LANES]

    jobs = [functools.partial(fill, pr) for pr in range(N_PAIRS)]
    for pr in range(N_PAIRS):
        for dil, out in ((4, o4), (16, o16)):
            jobs += [functools.partial(gather, pr, dil, out, r) for r in range(dil)]
    return jobs


def _proj_kernel(x_ref, g_ref, wtok_ref, wfeat_ref, wrope_ref, gains_ref, cos_ref, sin_ref,
                 ga_ref, gbt_ref, qkv_ref, qbt_ref, kb_ref, vbt_ref):
    for sub in range(PROJ_SUBTILES):
        _proj_subtile(sub, x_ref, g_ref, wtok_ref, wfeat_ref, wrope_ref, gains_ref, cos_ref, sin_ref,
                      ga_ref, gbt_ref, qkv_ref, qbt_ref, kb_ref, vbt_ref)


def _proj_subtile(sub, x_ref, g_ref, wtok_ref, wfeat_ref, wrope_ref, gains_ref, cos_ref, sin_ref,
                  ga_ref, gbt_ref, qkv_ref, qbt_ref, kb_ref, vbt_ref):
    t = x_ref.shape[1] // PROJ_SUBTILES
    row0 = sub * t
    rows = slice(row0, row0 + t)
    scale = HEAD_DIM ** -0.5 * LOG2_E
    x = x_ref[0, rows, :]
    ms = jnp.mean(x * x, axis=-1, keepdims=True)
    h = (x * lax.rsqrt(ms + EPS) * g_ref[...]).astype(BF16)

    def store_pairs(j, y):
        for pr in range(N_PAIRS):
            qkv_ref[0, rows, _qkv_lanes(pr, 1, 0, j)] = y[:, pr * LANES:(pr + 1) * LANES].astype(BF16)

    qa = _nt_dot(wfeat_ref[:D_A, :], h).reshape(N_HEADS_A, HEAD_DIM, t)
    qa = _head_rms(qa, gains_ref[0]) * scale
    store_pairs(0, qa.reshape(D_A, t).T)
    ka = _nt_dot(wfeat_ref[D_A:2 * D_A, :], h).reshape(N_HEADS_A, HEAD_DIM, t)
    ka = _head_rms(ka, gains_ref[1])
    store_pairs(1, ka.reshape(D_A, t).T)
    store_pairs(2, _dot(h, wtok_ref[:, :D_A]))
    ga_ref[0, rows, :] = _dot(h, wtok_ref[:, D_A:]).astype(BF16)

    rope = _nt_dot(wrope_ref[...], h)
    o = 4 * D_A + D_B + D_KV_B
    vg = _nt_dot(wfeat_ref[o:, :], h)
    gbt_ref[0, :, :, rows] = vg[D_KV_B:].reshape(N_HEADS_B, HEAD_DIM, t).astype(BF16)
    cos = cos_ref[:, rows]
    sin = sin_ref[:, rows]
    qb = _head_rms(rope[:D_B].reshape(N_HEADS_B, HEAD_DIM, t), gains_ref[2])
    qbt_ref[0, :, :, rows] = (_rope_halves(qb, cos, sin) * scale).astype(BF16)
    kb = _head_rms(rope[D_B:].reshape(N_KV_B, HEAD_DIM, t), gains_ref[3])
    kb = _rope_halves(kb, cos, sin)
    ext = (lax.broadcasted_iota(jnp.int32, (HEAD_DIM, t), 0) < SHIFT_TERMS).astype(F32)
    for kv in range(N_KV_B):
        kb_ref[0, kv, rows, :] = jnp.concatenate([kb[kv], ext], axis=0).T.astype(BF16)
    vbt_ref[0, :, 0, :, rows] = vg[:D_KV_B].reshape(N_KV_B, HEAD_DIM, t).astype(BF16)


def _proj_call(x, ln_g, wtok, wfeat, wrope, gains, cos_t, sin_t):
    B, S, D = x.shape
    tm = ROW_TILE
    nt = S // tm
    full = lambda shape: pl.BlockSpec(shape, lambda b, i: (0,) * len(shape))
    tok_spec = pl.BlockSpec((1, tm, D_A), lambda b, i: (b, i, 0))
    tok_shape = jax.ShapeDtypeStruct((B, S, D_A), BF16)
    out_shape = (
        tok_shape,
        jax.ShapeDtypeStruct((B, N_HEADS_B, HEAD_DIM, S), BF16),
        jax.ShapeDtypeStruct((B, S, 3 * D_A), BF16),
        jax.ShapeDtypeStruct((B, N_HEADS_B, HEAD_DIM, S), BF16),
        jax.ShapeDtypeStruct((B, N_KV_B, S, 2 * HEAD_DIM), BF16),
        jax.ShapeDtypeStruct((B, N_KV_B, nt, HEAD_DIM, tm), BF16),
    )
    out_specs = (
        tok_spec, pl.BlockSpec((1, N_HEADS_B, HEAD_DIM, tm), lambda b, i: (b, 0, 0, i)),
        pl.BlockSpec((1, tm, 3 * D_A), lambda b, i: (b, i, 0)),
        pl.BlockSpec((1, N_HEADS_B, HEAD_DIM, tm), lambda b, i: (b, 0, 0, i)),
        pl.BlockSpec((1, N_KV_B, tm, 2 * HEAD_DIM), lambda b, i: (b, 0, i, 0)),
        pl.BlockSpec((1, N_KV_B, 1, HEAD_DIM, tm), lambda b, i: (b, 0, i, 0, 0)),
    )
    in_specs = [
        pl.BlockSpec((1, tm, D), lambda b, i: (b, i, 0)),
        full((1, D)),
        full(wtok.shape),
        full(wfeat.shape),
        full(wrope.shape),
        full(gains.shape),
        pl.BlockSpec((HEAD_DIM // 2, tm), lambda b, i: (0, i)),
        pl.BlockSpec((HEAD_DIM // 2, tm), lambda b, i: (0, i)),
    ]
    return pl.pallas_call(
        _proj_kernel,
        grid=(B, nt),
        in_specs=in_specs,
        out_specs=out_specs,
        out_shape=out_shape,
        compiler_params=pltpu.CompilerParams(
            dimension_semantics=("arbitrary", "arbitrary"), vmem_limit_bytes=VMEM_LIMIT),
        name="proj",
    )(x, ln_g, wtok, wfeat, wrope, gains, cos_t, sin_t)


def _gqa_kernel(*refs, mode):
    if mode == "max":
        qt_ref, k_ref, o_ref = refs
    else:
        n_in = 5 if mode == "attn_shift" else 4
        qt_ref, k_ref, vt_ref, gt_ref = refs[:4]
        c_ref = refs[4] if mode == "attn_shift" else None
        qkv_ref, o_ref, qkv4_ref, qkv16_ref = refs[n_in:n_in + 4]
        scr_refs = refs[n_in + 4:]
        side_jobs = []
        for j in range(3):
            side_jobs += _residue_layout_jobs(qkv_ref, j, scr_refs[j], qkv4_ref, qkv16_ref)
    tq = Q_TILE
    n_steps = k_ref.shape[2] // KEY_STEP
    sub = F32_SUBLANES
    heads = [(g, slice(j * tq, (j + 1) * tq)) for j in range(qt_ref.shape[3] // tq) for g in range(GQA_GROUP)]
    qext = {}
    for g, cols in heads:
        q = qt_ref[0, g, :, cols]
        if mode == "attn_shift":
            pad = jnp.zeros((HEAD_DIM - SHIFT_ROWS, tq), BF16)
            qext[g, cols.start] = jnp.concatenate([q, c_ref[0, g, :, cols], pad], axis=0)
        else:
            qext[g, cols.start] = jnp.concatenate([q, jnp.zeros_like(q)], axis=0)
    units = [(g, cols, t) for g, cols in heads for t in range(n_steps)]

    def scores(u):
        g, cols, t = u
        return _dot(k_ref[0, 0, t * KEY_STEP:(t + 1) * KEY_STEP, :], qext[g, cols.start])

    pending = [scores(u) for u in units[:LOOKAHEAD]]
    for i, (g, cols, t) in enumerate(units):
        if i + LOOKAHEAD < len(units):
            pending.append(scores(units[i + LOOKAHEAD]))
        s = pending[i]
        pending[i] = None
        if mode == "max":
            part = s.reshape(KEY_STEP // sub, sub, tq).max(axis=0)
            mx = part if t == 0 else jnp.maximum(mx, part)
            if t == n_steps - 1:
                o_ref[0, g, :, cols] = jnp.broadcast_to(mx.max(axis=0, keepdims=True), (sub, tq))
            continue
        if t == 0:
            acc = jnp.zeros((HEAD_DIM, tq), F32)
            den = jnp.zeros((sub, tq), F32)
        p = jnp.exp2(s)
        den = den + p.reshape(KEY_STEP // sub, sub, tq).sum(axis=0)
        if side_jobs:
            back = side_jobs.pop(0)()
            if back is not None:
                den = den + _zero_from(back, tq)
        tk = vt_ref.shape[4]
        lo = (t * KEY_STEP) % tk
        acc = acc + _dot(vt_ref[0, 0, (t * KEY_STEP) // tk, :, lo:lo + KEY_STEP], p.astype(BF16))
        if t == n_steps - 1:
            y = acc / den.sum(axis=0, keepdims=True)
            o_ref[0, g, :, cols] = (y * _silu(gt_ref[0, g, :, cols].astype(F32))).astype(BF16)
    if mode != "max":
        for job in side_jobs:
            job()


def _gqa_call(qbt, kb, vbt, gbt, window_qkv, shift=None, mode="attn"):
    B, H, _, S = qbt.shape
    tq = Q_TILE * min(Q_TILES_PER_STEP, S // Q_TILE)
    n_q = S // tq
    nt, rows, tk = vbt.shape[2], vbt.shape[3], vbt.shape[4]
    head_blk = lambda r: pl.BlockSpec((1, GQA_GROUP, r, tq), lambda b, kv, i: (b, kv, 0, i))
    in_specs = [head_blk(HEAD_DIM), pl.BlockSpec((1, 1, S, 2 * HEAD_DIM), lambda b, kv, i: (b, kv, 0, 0))]
    args = [qbt, kb]
    grid = (B, N_KV_B, n_q)
    params = pltpu.CompilerParams(
        dimension_semantics=("arbitrary", "arbitrary", "arbitrary"), vmem_limit_bytes=VMEM_LIMIT)
    if mode == "max":
        return pl.pallas_call(
            functools.partial(_gqa_kernel, mode=mode), grid=grid, in_specs=in_specs, out_specs=head_blk(8),
            out_shape=jax.ShapeDtypeStruct((B, H, 8, S), F32), compiler_params=params, name="gqa_" + mode,
        )(*args)
    in_specs += [pl.BlockSpec((1, 1, nt, rows, tk), lambda b, kv, i: (b, kv, 0, 0, 0)), head_blk(HEAD_DIM)]
    args += [vbt, gbt]
    if mode == "attn_shift":
        in_specs.append(head_blk(SHIFT_ROWS))
        args.append(shift)
    t = S // (N_KV_B * n_q)
    tile = lambda b, kv, i: (b, kv * n_q + i, 0)
    in_specs.append(pl.BlockSpec((1, t, 3 * D_A), tile))
    args.append(window_qkv)
    out_specs, out_shape = [head_blk(HEAD_DIM)], [jax.ShapeDtypeStruct((B, H, HEAD_DIM, S), BF16)]
    for dil in DILATIONS[1:]:
        out_specs.append(pl.BlockSpec((1, t // dil, dil * 3 * D_A), tile))
        out_shape.append(jax.ShapeDtypeStruct((B, S // dil, dil * 3 * D_A), BF16))
    outs = pl.pallas_call(
        functools.partial(_gqa_kernel, mode=mode), grid=grid, in_specs=in_specs, out_specs=out_specs,
        out_shape=out_shape, scratch_shapes=[pltpu.VMEM((N_PAIRS, t, LANES), F32)] * 3,
        compiler_params=params, name="gqa_" + mode,
    )(*args)
    return outs[0], outs[1:]


def _shift_rows(m):
    rest = m[:, :, :1, :]
    terms = []
    for _ in range(SHIFT_TERMS):
        t = rest.astype(BF16)
        terms.append(-t)
        rest = rest - t.astype(F32)
    pad = jnp.zeros(m.shape[:2] + (SHIFT_ROWS - SHIFT_TERMS, m.shape[3]), BF16)
    return jnp.concatenate(terms + [pad], axis=2)


def _band_bucket_index():
    nb = N_BUCKETS // 2
    max_exact = nb // 2
    rel = np.arange(BAND_SPAN)[None, :] - BAND_HALF - np.arange(BAND_BLOCK)[:, None]
    out = []
    for dilation in DILATIONS:
        dist = rel * dilation
        ret = np.where(dist > 0, nb, 0)
        n = np.abs(dist)
        nf = np.maximum(n, max_exact).astype(np.float32)
        large = max_exact + (np.log(nf / np.float32(max_exact)) / np.float32(math.log(MAX_DISTANCE / max_exact))
                             * np.float32(nb - max_exact)).astype(np.int32)
        large = np.minimum(large, nb - 1)
        bucket = ret + np.where(n < max_exact, n, large)
        out.append(np.where(np.abs(rel) <= BAND_HALF, bucket, -1))
    return np.stack(out).astype(np.int32)


def _bias_kernel(table_ref, idx_ref, o_ref):
    chunk = 16
    index = _band_bucket_index()
    for pat in range(len(DILATIONS)):
        present = [int(b) for b in np.unique(index[pat]) if b >= 0]
        for lo in range(0, BAND_BLOCK, chunk):
            idx = idx_ref[pat, lo:lo + chunk, :]
            acc = [jnp.full(idx.shape, NEG_INF, F32) for _ in range(N_HEADS_A)]
            for b in present:
                hit = idx == b
                for h in range(N_HEADS_A):
                    acc[h] = jnp.where(hit, table_ref[b, h] * LOG2_E, acc[h])
            for h in range(N_HEADS_A):
                o_ref[pat, h, lo:lo + chunk, :] = acc[h]


def _bias_call(rel_bias):
    idx = jnp.asarray(_band_bucket_index())
    n_pat = len(DILATIONS)
    return pl.pallas_call(
        _bias_kernel,
        in_specs=[pl.BlockSpec(memory_space=pltpu.SMEM), pl.BlockSpec(memory_space=pltpu.VMEM)],
        out_specs=pl.BlockSpec(memory_space=pltpu.VMEM),
        out_shape=jax.ShapeDtypeStruct((n_pat, N_HEADS_A, BAND_BLOCK, BAND_SPAN), F32),
        name="bias_bands",
    )(rel_bias, idx)


def _band_kernel(*refs, mode):
    n_in = 3 * len(DILATIONS)
    ins, bias_ref = refs[:n_in], refs[n_in]
    g_ref = refs[n_in + 1] if mode != "max" else None
    c_ref = refs[n_in + 2] if mode == "attn_shift" else None
    o_ref, nums_ref, dens_ref = refs[-3:]
    n = pl.program_id(1)
    last = pl.num_programs(1) - 1
    col = lax.broadcasted_iota(jnp.int32, (1, BAND_SPAN), 1)
    edge_lo = jnp.where((n == 0) & (col < BAND_HALF), NEG_INF, 0.0).astype(F32)
    edge_hi = jnp.where((n == last) & (col >= BAND_BLOCK + BAND_HALF), NEG_INF, 0.0).astype(F32)
    lane = lax.broadcasted_iota(jnp.int32, (BAND_BLOCK, LANES), 1)
    first_half = lane < HEAD_DIM
    ones = jnp.ones((BAND_SPAN, LANES), BF16)

    def band(main, prev, nxt, blk, n_blk, ls):
        lo = blk * BAND_BLOCK - BAND_HALF
        parts = []
        if blk == 0:
            parts.append(prev[0, :, ls])
        start = max(lo, 0)
        stop = min(lo + BAND_SPAN, n_blk * BAND_BLOCK)
        parts.append(main[0, start:stop, ls])
        if blk == n_blk - 1:
            parts.append(nxt[0, :, ls])
        return parts[0] if len(parts) == 1 else jnp.concatenate(parts, axis=0)

    units = []
    for pp in range(nums_ref.shape[0]):
        for pat in reversed(range(len(DILATIONS))):
            dil = DILATIONS[pat]
            main_ref, prev_ref, next_ref = ins[3 * pat:3 * pat + 3]
            n_blk = SUPER_BLOCK // (dil * BAND_BLOCK)
            for r in range(dil):
                for blk in range(n_blk):
                    units.append((pat, dil, r, blk, n_blk, main_ref, prev_ref, next_ref, pp))

    def stage_scores(u):
        pat, dil, r, blk, n_blk, main_ref, prev_ref, next_ref, pp = u
        q = main_ref[0, blk * BAND_BLOCK:(blk + 1) * BAND_BLOCK, _qkv_lanes(pp, dil, r, 0)]
        kband = band(main_ref, prev_ref, next_ref, blk, n_blk, _qkv_lanes(pp, dil, r, 1))
        zero = jnp.zeros_like(q)
        return [_nt_dot(jnp.where(first_half, q, zero), kband), _nt_dot(jnp.where(first_half, zero, q), kband)]

    def stage_values(u, s_pair):
        pat, dil, r, blk, n_blk, main_ref, prev_ref, next_ref, pp = u
        out_ls = slice(pp * LANES, (pp + 1) * LANES)
        num_ref, den_ref = nums_ref.at[pp], dens_ref.at[pp]
        if dil == 1:
            rows = slice(blk * BAND_BLOCK, (blk + 1) * BAND_BLOCK)
        else:
            rows = pl.ds(dil * blk * BAND_BLOCK + r, BAND_BLOCK, stride=dil)
        def biased(half):
            s = s_pair[half] + bias_ref[pat, 2 * pp + half]
            if blk == 0:
                s = s + edge_lo
            if blk == n_blk - 1:
                s = s + edge_hi
            return s

        if mode == "max":
            mx = jnp.where(first_half, biased(0).max(axis=-1, keepdims=True), biased(1).max(axis=-1, keepdims=True))
            if dil == DILATIONS[-1]:
                num_ref[rows, :] = mx
            elif dil > 1:
                num_ref[rows, :] = jnp.maximum(num_ref[rows, :], mx)
            else:
                o_ref[0, pp, rows, :] = jnp.maximum(num_ref[rows, :], mx)
            return
        vband = band(main_ref, prev_ref, next_ref, blk, n_blk, _qkv_lanes(pp, dil, r, 2))
        vext = jnp.concatenate([vband, ones], axis=1)
        outs = []
        for half in range(2):
            s = biased(half)
            if mode == "attn_shift":
                s = s - c_ref[0, pp, rows, :][:, half * HEAD_DIM:half * HEAD_DIM + 1]
            outs.append(_dot(jnp.exp2(s).astype(BF16), vext))
        num = jnp.where(first_half, outs[0][:, :LANES], outs[1][:, :LANES])
        den = jnp.where(first_half, outs[0][:, LANES:], outs[1][:, LANES:])
        if dil == DILATIONS[-1]:
            num_ref[rows, :] = num
            den_ref[rows, :] = den
        elif dil > 1:
            num_ref[rows, :] = num_ref[rows, :] + num
            den_ref[rows, :] = den_ref[rows, :] + den
        else:
            y = (num_ref[rows, :] + num) / (den_ref[rows, :] + den)
            o_ref[0, rows, out_ls] = (y * _silu(g_ref[0, rows, out_ls].astype(F32))).astype(BF16)

    pending = [stage_scores(u) for u in units[:LOOKAHEAD]]
    for t, u in enumerate(units):
        if t + LOOKAHEAD < len(units):
            pending.append(stage_scores(units[t + LOOKAHEAD]))
        stage_values(u, pending[t])
        pending[t] = None


def _band_call(layouts, bias, gate, shift=None, mode="attn"):
    B, S, _ = layouts[0].shape
    n_sb = S // SUPER_BLOCK
    pp = PAIRS_PER_STEP
    in_specs, args = [], []
    for qkv, dil in zip(layouts, DILATIONS):
        rows = SUPER_BLOCK // dil
        width = pp * dil * 3 * LANES
        n_halo = (S // dil) // BAND_HALF
        per_sb = rows // BAND_HALF
        main = pl.BlockSpec((1, rows, width), lambda b, n, pr: (b, n, pr))
        prev = pl.BlockSpec((1, BAND_HALF, width),
                            lambda b, n, pr, per_sb=per_sb: (b, jnp.maximum(n * per_sb - 1, 0), pr))
        nxt = pl.BlockSpec((1, BAND_HALF, width),
                           lambda b, n, pr, per_sb=per_sb, n_halo=n_halo: (b, jnp.minimum((n + 1) * per_sb, n_halo - 1), pr))
        in_specs += [main, prev, nxt]
        args += [qkv, qkv, qkv]
    in_specs.append(pl.BlockSpec((len(DILATIONS), 2 * pp, BAND_BLOCK, BAND_SPAN), lambda b, n, pr: (0, pr, 0, 0)))
    args.append(bias)
    tok_blk = pl.BlockSpec((1, SUPER_BLOCK, pp * LANES), lambda b, n, pr: (b, n, pr))
    shift_blk = pl.BlockSpec((1, pp, SUPER_BLOCK, LANES), lambda b, n, pr: (b, pr, n, 0))
    if mode != "max":
        in_specs.append(tok_blk)
        args.append(gate)
    if mode == "attn_shift":
        in_specs.append(shift_blk)
        args.append(shift)
    if mode == "max":
        out_specs, out_shape = shift_blk, jax.ShapeDtypeStruct((B, N_PAIRS, S, LANES), F32)
    else:
        out_specs, out_shape = tok_blk, jax.ShapeDtypeStruct((B, S, D_A), BF16)
    return pl.pallas_call(
        functools.partial(_band_kernel, mode=mode),
        grid=(B, n_sb, N_PAIRS // pp),
        in_specs=in_specs,
        out_specs=out_specs,
        out_shape=out_shape,
        scratch_shapes=[pltpu.VMEM((pp, SUPER_BLOCK, LANES), F32)] * 2,
        compiler_params=pltpu.CompilerParams(
            dimension_semantics=("arbitrary", "arbitrary", "arbitrary"), vmem_limit_bytes=VMEM_LIMIT),
        name="band_" + mode,
    )(*args)


def _out_kernel(x_ref, p_ref, ya_ref, ybt_ref, wo_ref, wple_ref, wgate_ref, o_ref):
    tm = x_ref.shape[1]
    yb = ybt_ref[0].reshape(D_B, tm).astype(F32).T.astype(BF16)
    x1 = x_ref[0] + _dot(ya_ref[0], wo_ref[:D_A, :]) + _dot(yb, wo_ref[D_A:, :])
    gate = jax.nn.sigmoid(_dot(x1.astype(BF16), wgate_ref[...]))
    ple = _dot(p_ref[0].astype(BF16), wple_ref[...])
    o_ref[0] = x1 + ple * gate


def _out_call(x, p, ya, ybt, wo, wple, wgate):
    B, S, D = x.shape
    tm = ROW_TILE
    full = lambda shape: pl.BlockSpec(shape, lambda b, i: (0,) * len(shape))
    tok = lambda c: pl.BlockSpec((1, tm, c), lambda b, i: (b, i, 0))
    return pl.pallas_call(
        _out_kernel,
        grid=(B, S // tm),
        in_specs=[tok(D), tok(p.shape[-1]), tok(D_A),
                  pl.BlockSpec((1, N_HEADS_B, HEAD_DIM, tm), lambda b, i: (b, 0, 0, i)),
                  full(wo.shape), full(wple.shape), full(wgate.shape)],
        out_specs=tok(D),
        out_shape=jax.ShapeDtypeStruct((B, S, D), F32),
        compiler_params=pltpu.CompilerParams(
            dimension_semantics=("arbitrary", "arbitrary"), vmem_limit_bytes=VMEM_LIMIT),
        name="out_proj",
    )(x, p, ya, ybt, wo, wple, wgate)


def _rope_tables(S):
    n = HEAD_DIM // 4
    inv = ROPE_THETA ** (-jnp.arange(n, dtype=F32) / n)
    rows = S // GRID_W
    ang_row = inv[:, None] * jnp.arange(rows, dtype=jnp.int32).astype(F32)[None, :]
    ang_col = inv[:, None] * jnp.arange(GRID_W, dtype=jnp.int32).astype(F32)[None, :]
    expand = lambda fn: jnp.concatenate(
        [jnp.repeat(fn(ang_row), GRID_W, axis=1), jnp.tile(fn(ang_col), (1, rows))], axis=0)
    return expand(jnp.cos), expand(jnp.sin)


def _pairs_apart(t):
    lead, tail = t.shape[:-2], t.shape[-1]
    heads = t.shape[-2] // HEAD_DIM
    t = t.reshape(*lead, heads, HEAD_DIM // 2, 2, tail)
    return jnp.swapaxes(t, -3, -2).reshape(*lead, heads * HEAD_DIM, tail)


def kernel(x, p, ln_g, w_in, qn_a, kn_a, qn_b, kn_b, w_out, w_ple, w_pgate, rel_bias):
    B, S, D = x.shape
    depth = w_in.shape[0]
    cos_t, sin_t = _rope_tables(S)
    bias = _bias_call(rel_bias)
    for i in range(depth):
        wfeat = w_in[i].T.astype(BF16)
        wrope = _pairs_apart(wfeat[4 * D_A:4 * D_A + D_B + D_KV_B])
        wtok = w_in[i][:, 2 * D_A:4 * D_A].astype(BF16)
        gains_b = _pairs_apart(jnp.stack([qn_b[i], kn_b[i]], axis=-1))
        gains = jnp.concatenate([jnp.stack([qn_a[i], kn_a[i]]), gains_b.T]).astype(F32)
        gains = jnp.broadcast_to(gains[:, :, None], (4, HEAD_DIM, ROW_TILE // PROJ_SUBTILES))

        outs = _proj_call(
            x, ln_g[i][None, :], wtok, wfeat, wrope, gains, cos_t, sin_t)
        ga, gbt, qkv1, qbt, kb, vbt = outs

        gmax = jnp.max(jnp.abs(jnp.stack([qn_a[i], kn_a[i], qn_b[i], kn_b[i]]).astype(F32)), axis=1)
        root = HEAD_DIM ** 0.5
        bound_a = root * gmax[0] * gmax[1] + jnp.max(jnp.abs(rel_bias.astype(F32)))
        bound_b = root * gmax[2] * gmax[3]
        safe = jnp.maximum(bound_a, bound_b) <= SAFE_EXPONENT

        def unshifted(ops):
            qbt, kb, vbt, gbt, qkv1, bias, ga = ops
            ybt, (qkv4, qkv16) = _gqa_call(qbt, kb, vbt, gbt, qkv1)
            return ybt, _band_call((qkv1, qkv4, qkv16), bias, ga)

        def shifted(ops):
            qbt, kb, vbt, gbt, qkv1, bias, ga = ops
            shift_b = _shift_rows(_gqa_call(qbt, kb, vbt, gbt, qkv1, mode="max"))
            ybt, (qkv4, qkv16) = _gqa_call(qbt, kb, vbt, gbt, qkv1, shift_b, mode="attn_shift")
            layouts = (qkv1, qkv4, qkv16)
            shift_a = _band_call(layouts, bias, ga, mode="max")
            return ybt, _band_call(layouts, bias, ga, shift_a, mode="attn_shift")

        ybt, ya = lax.cond(safe, unshifted, shifted, (qbt, kb, vbt, gbt, qkv1, bias, ga))
        x = _out_call(x, p[i], ya, ybt, w_out[i].astype(BF16), w_ple[i].astype(BF16), w_pgate[i].astype(BF16))
    return x
```

```python
import functools
import math

import numpy as np
import jax
import jax.numpy as jnp
from jax import lax
from jax.experimental import pallas as pl
from jax.experimental.pallas import tpu as pltpu

HEAD_DIM = 64
N_HEADS_A = 8
N_HEADS_B = 8
N_KV_B = 2
GQA_GROUP = N_HEADS_B // N_KV_B
D_A = N_HEADS_A * HEAD_DIM
D_B = N_HEADS_B * HEAD_DIM
D_KV_B = N_KV_B * HEAD_DIM
DILATIONS = (1, 4, 16)
BAND_BLOCK = 128
BAND_HALF = 64
BAND_SPAN = BAND_BLOCK + 2 * BAND_HALF
SUPER_BLOCK = BAND_BLOCK * max(DILATIONS)
LANES = 128
N_PAIRS = D_A // LANES
PAIRS_PER_STEP = 2
GRID_W = 64
ROPE_THETA = 10000.0
N_BUCKETS = 32
MAX_DISTANCE = 1024
EPS = 1e-6
NEG_INF = -1e30

ROW_TILE = 1024
PROJ_SUBTILES = 4
Q_TILE = 512
Q_TILES_PER_STEP = 2
KEY_STEP = 256
LOOKAHEAD = 2
V7X_VMEM_BYTES = 64 * 1024 * 1024
VMEM_LIMIT = V7X_VMEM_BYTES - 8 * 1024 * 1024
F32_SUBLANES = 8
LOG2_E = math.log2(math.e)
SAFE_EXPONENT = 60.0
SHIFT_TERMS = 3
SHIFT_ROWS = 16

F32 = jnp.float32
BF16 = jnp.bfloat16


def _nt_dot(a, b):
    return lax.dot_general(a, b, (((1,), (1,)), ((), ())), preferred_element_type=F32)


def _dot(a, b):
    return jnp.dot(a, b, preferred_element_type=F32)


def _silu(g):
    return g * jax.nn.sigmoid(g)


def _head_rms(t, gain):
    ms = jnp.mean(t * t, axis=1, keepdims=True)
    return t * lax.rsqrt(ms + EPS) * gain[None]


def _rope_halves(t, cos, sin):
    half = HEAD_DIM // 2
    x1 = t[:, :half, :]
    x2 = t[:, half:, :]
    c = cos[None]
    s = sin[None]
    return jnp.concatenate([x1 * c - x2 * s, x1 * s + x2 * c], axis=1)


def _zero_from(tile, width):
    bits = pltpu.bitcast(tile, jnp.uint32)
    zero = lax.shift_right_logical(lax.shift_right_logical(bits, jnp.uint32(16)), jnp.uint32(16))
    return jnp.tile(pltpu.bitcast(zero, F32), (1, width // LANES))


def _qkv_lanes(pr, dil, r, j):
    lo = ((pr * dil + r) * 3 + j) * LANES
    return slice(lo, lo + LANES)


def _residue_layout_jobs(src, j, scr, o4, o16):
    t = src.shape[1]

    def fill(pr):
        scr[pr] = src[0, :, _qkv_lanes(pr, 1, 0, j)].astype(F32)
        return None

    def gather(pr, dil, out, r):
        lanes = _qkv_lanes(pr, dil, r, j)
        lo = lanes.start
        out[0, :, lanes] = scr[pr, pl.ds(r, t // dil, stride=dil), :].astype(BF16)
        row = pl.multiple_of(jnp.minimum(pl.program_id(0), 0), 16)
        return out[0, pl.ds(row, 16), lo:lo + LANES]

    jobs = [functools.partial(fill, pr) for pr in range(N_PAIRS)]
    for pr in range(N_PAIRS):
        for dil, out in ((4, o4), (16, o16)):
            jobs += [functools.partial(gather, pr, dil, out, r) for r in range(dil)]
    return jobs


def _proj_kernel(x_ref, g_ref, wtok_ref, wfeat_ref, wrope_ref, gains_ref, cos_ref, sin_ref,
                 ga_ref, gbt_ref, qkv_ref, qbt_ref, kb_ref, vbt_ref):
    for sub in range(PROJ_SUBTILES):
        _proj_subtile(sub, x_ref, g_ref, wtok_ref, wfeat_ref, wrope_ref, gains_ref, cos_ref, sin_ref,
                      ga_ref, gbt_ref, qkv_ref, qbt_ref, kb_ref, vbt_ref)


def _proj_subtile(sub, x_ref, g_ref, wtok_ref, wfeat_ref, wrope_ref, gains_ref, cos_ref, sin_ref,
                  ga_ref, gbt_ref, qkv_ref, qbt_ref, kb_ref, vbt_ref):
    t = x_ref.shape[1] // PROJ_SUBTILES
    row0 = sub * t
    rows = slice(row0, row0 + t)
    scale = HEAD_DIM ** -0.5 * LOG2_E
    x = x_ref[0, rows, :]
    ms = jnp.mean(x * x, axis=-1, keepdims=True)
    h = (x * lax.rsqrt(ms + EPS) * g_ref[...]).astype(BF16)

    def store_pairs(j, y):
        for pr in range(N_PAIRS):
            qkv_ref[0, rows, _qkv_lanes(pr, 1, 0, j)] = y[:, pr * LANES:(pr + 1) * LANES].astype(BF16)

    qa = _nt_dot(wfeat_ref[:D_A, :], h).reshape(N_HEADS_A, HEAD_DIM, t)
    qa = _head_rms(qa, gains_ref[0]) * scale
    store_pairs(0, qa.reshape(D_A, t).T)
    ka = _nt_dot(wfeat_ref[D_A:2 * D_A, :], h).reshape(N_HEADS_A, HEAD_DIM, t)
    ka = _head_rms(ka, gains_ref[1])
    store_pairs(1, ka.reshape(D_A, t).T)
    store_pairs(2, _dot(h, wtok_ref[:, :D_A]))
    ga_ref[0, rows, :] = _dot(h, wtok_ref[:, D_A:]).astype(BF16)

    rope = _nt_dot(wrope_ref[...], h)
    o = 4 * D_A + D_B + D_KV_B
    vg = _nt_dot(wfeat_ref[o:, :], h)
    gbt_ref[0, :, :, rows] = vg[D_KV_B:].reshape(N_HEADS_B, HEAD_DIM, t).astype(BF16)
    cos = cos_ref[:, rows]
    sin = sin_ref[:, rows]
    qb = _head_rms(rope[:D_B].reshape(N_HEADS_B, HEAD_DIM, t), gains_ref[2])
    qbt_ref[0, :, :, rows] = (_rope_halves(qb, cos, sin) * scale).astype(BF16)
    kb = _head_rms(rope[D_B:].reshape(N_KV_B, HEAD_DIM, t), gains_ref[3])
    kb = _rope_halves(kb, cos, sin)
    ext = (lax.broadcasted_iota(jnp.int32, (HEAD_DIM, t), 0) < SHIFT_TERMS).astype(F32)
    for kv in range(N_KV_B):
        kb_ref[0, kv, rows, :] = jnp.concatenate([kb[kv], ext], axis=0).T.astype(BF16)
    vbt_ref[0, :, 0, :, rows] = vg[:D_KV_B].reshape(N_KV_B, HEAD_DIM, t).astype(BF16)


def _proj_call(x, ln_g, wtok, wfeat, wrope, gains, cos_t, sin_t):
    B, S, D = x.shape
    tm = ROW_TILE
    nt = S // tm
    full = lambda shape: pl.BlockSpec(shape, lambda b, i: (0,) * len(shape))
    tok_spec = pl.BlockSpec((1, tm, D_A), lambda b, i: (b, i, 0))
    tok_shape = jax.ShapeDtypeStruct((B, S, D_A), BF16)
    out_shape = (
        tok_shape,
        jax.ShapeDtypeStruct((B, N_HEADS_B, HEAD_DIM, S), BF16),
        jax.ShapeDtypeStruct((B, S, 3 * D_A), BF16),
        jax.ShapeDtypeStruct((B, N_HEADS_B, HEAD_DIM, S), BF16),
        jax.ShapeDtypeStruct((B, N_KV_B, S, 2 * HEAD_DIM), BF16),
        jax.ShapeDtypeStruct((B, N_KV_B, nt, HEAD_DIM, tm), BF16),
    )
    out_specs = (
        tok_spec, pl.BlockSpec((1, N_HEADS_B, HEAD_DIM, tm), lambda b, i: (b, 0, 0, i)),
        pl.BlockSpec((1, tm, 3 * D_A), lambda b, i: (b, i, 0)),
        pl.BlockSpec((1, N_HEADS_B, HEAD_DIM, tm), lambda b, i: (b, 0, 0, i)),
        pl.BlockSpec((1, N_KV_B, tm, 2 * HEAD_DIM), lambda b, i: (b, 0, i, 0)),
        pl.BlockSpec((1, N_KV_B, 1, HEAD_DIM, tm), lambda b, i: (b, 0, i, 0, 0)),
    )
    in_specs = [
        pl.BlockSpec((1, tm, D), lambda b, i: (b, i, 0)),
        full((1, D)),
        full(wtok.shape),
        full(wfeat.shape),
        full(wrope.shape),
        full(gains.shape),
        pl.BlockSpec((HEAD_DIM // 2, tm), lambda b, i: (0, i)),
        pl.BlockSpec((HEAD_DIM // 2, tm), lambda b, i: (0, i)),
    ]
    return pl.pallas_call(
        _proj_kernel,
        grid=(B, nt),
        in_specs=in_specs,
        out_specs=out_specs,
        out_shape=out_shape,
        compiler_params=pltpu.CompilerParams(
            dimension_semantics=("arbitrary", "arbitrary"), vmem_limit_bytes=VMEM_LIMIT),
        name="proj",
    )(x, ln_g, wtok, wfeat, wrope, gains, cos_t, sin_t)


def _gqa_kernel(*refs, mode):
    if mode == "max":
        qt_ref, k_ref, o_ref = refs
    else:
        n_in = 5 if mode == "attn_shift" else 4
        qt_ref, k_ref, vt_ref, gt_ref = refs[:4]
        c_ref = refs[4] if mode == "attn_shift" else None
        qkv_ref, o_ref, qkv4_ref, qkv16_ref = refs[n_in:n_in + 4]
        scr_refs = refs[n_in + 4:]
        side_jobs = []
        for j in range(3):
            side_jobs += _residue_layout_jobs(qkv_ref, j, scr_refs[j], qkv4_ref, qkv16_ref)
    tq = Q_TILE
    n_steps = k_ref.shape[2] // KEY_STEP
    sub = F32_SUBLANES
    heads = [(g, slice(j * tq, (j + 1) * tq)) for j in range(qt_ref.shape[3] // tq) for g in range(GQA_GROUP)]
    qext = {}
    for g, cols in heads:
        q = qt_ref[0, g, :, cols]
        if mode == "attn_shift":
            pad = jnp.zeros((HEAD_DIM - SHIFT_ROWS, tq), BF16)
            qext[g, cols.start] = jnp.concatenate([q, c_ref[0, g, :, cols], pad], axis=0)
        else:
            qext[g, cols.start] = jnp.concatenate([q, jnp.zeros_like(q)], axis=0)
    units = [(g, cols, t) for g, cols in heads for t in range(n_steps)]

    def scores(u):
        g, cols, t = u
        return _dot(k_ref[0, 0, t * KEY_STEP:(t + 1) * KEY_STEP, :], qext[g, cols.start])

    pending = [scores(u) for u in units[:LOOKAHEAD]]
    for i, (g, cols, t) in enumerate(units):
        if i + LOOKAHEAD < len(units):
            pending.append(scores(units[i + LOOKAHEAD]))
        s = pending[i]
        pending[i] = None
        if mode == "max":
            part = s.reshape(KEY_STEP // sub, sub, tq).max(axis=0)
            mx = part if t == 0 else jnp.maximum(mx, part)
            if t == n_steps - 1:
                o_ref[0, g, :, cols] = jnp.broadcast_to(mx.max(axis=0, keepdims=True), (sub, tq))
            continue
        if t == 0:
            acc = jnp.zeros((HEAD_DIM, tq), F32)
            den = jnp.zeros((sub, tq), F32)
        p = jnp.exp2(s)
        den = den + p.reshape(KEY_STEP // sub, sub, tq).sum(axis=0)
        if side_jobs:
            back = side_jobs.pop(0)()
            if back is not None:
                den = den + _zero_from(back, tq)
        tk = vt_ref.shape[4]
        lo = (t * KEY_STEP) % tk
        acc = acc + _dot(vt_ref[0, 0, (t * KEY_STEP) // tk, :, lo:lo + KEY_STEP], p.astype(BF16))
        if t == n_steps - 1:
            y = acc / den.sum(axis=0, keepdims=True)
            o_ref[0, g, :, cols] = (y * _silu(gt_ref[0, g, :, cols].astype(F32))).astype(BF16)
    if mode != "max":
        for job in side_jobs:
            job()


def _gqa_call(qbt, kb, vbt, gbt, window_qkv, shift=None, mode="attn"):
    B, H, _, S = qbt.shape
    tq = Q_TILE * min(Q_TILES_PER_STEP, S // Q_TILE)
    n_q = S // tq
    nt, rows, tk = vbt.shape[2], vbt.shape[3], vbt.shape[4]
    head_blk = lambda r: pl.BlockSpec((1, GQA_GROUP, r, tq), lambda b, kv, i: (b, kv, 0, i))
    in_specs = [head_blk(HEAD_DIM), pl.BlockSpec((1, 1, S, 2 * HEAD_DIM), lambda b, kv, i: (b, kv, 0, 0))]
    args = [qbt, kb]
    grid = (B, N_KV_B, n_q)
    params = pltpu.CompilerParams(
        dimension_semantics=("arbitrary", "arbitrary", "arbitrary"), vmem_limit_bytes=VMEM_LIMIT)
    if mode == "max":
        return pl.pallas_call(
            functools.partial(_gqa_kernel, mode=mode), grid=grid, in_specs=in_specs, out_specs=head_blk(8),
            out_shape=jax.ShapeDtypeStruct((B, H, 8, S), F32), compiler_params=params, name="gqa_" + mode,
        )(*args)
    in_specs += [pl.BlockSpec((1, 1, nt, rows, tk), lambda b, kv, i: (b, kv, 0, 0, 0)), head_blk(HEAD_DIM)]
    args += [vbt, gbt]
    if mode == "attn_shift":
        in_specs.append(head_blk(SHIFT_ROWS))
        args.append(shift)
    t = S // (N_KV_B * n_q)
    tile = lambda b, kv, i: (b, kv * n_q + i, 0)
    in_specs.append(pl.BlockSpec((1, t, 3 * D_A), tile))
    args.append(window_qkv)
    out_specs, out_shape = [head_blk(HEAD_DIM)], [jax.ShapeDtypeStruct((B, H, HEAD_DIM, S), BF16)]
    for dil in DILATIONS[1:]:
        out_specs.append(pl.BlockSpec((1, t // dil, dil * 3 * D_A), tile))
        out_shape.append(jax.ShapeDtypeStruct((B, S // dil, dil * 3 * D_A), BF16))
    outs = pl.pallas_call(
        functools.partial(_gqa_kernel, mode=mode), grid=grid, in_specs=in_specs, out_specs=out_specs,
        out_shape=out_shape, scratch_shapes=[pltpu.VMEM((N_PAIRS, t, LANES), F32)] * 3,
        compiler_params=params, name="gqa_" + mode,
    )(*args)
    return outs[0], outs[1:]


def _shift_rows(m):
    rest = m[:, :, :1, :]
    terms = []
    for _ in range(SHIFT_TERMS):
        t = rest.astype(BF16)
        terms.append(-t)
        rest = rest - t.astype(F32)
    pad = jnp.zeros(m.shape[:2] + (SHIFT_ROWS - SHIFT_TERMS, m.shape[3]), BF16)
    return jnp.concatenate(terms + [pad], axis=2)


def _band_bucket_index():
    nb = N_BUCKETS // 2
    max_exact = nb // 2
    rel = np.arange(BAND_SPAN)[None, :] - BAND_HALF - np.arange(BAND_BLOCK)[:, None]
    out = []
    for dilation in DILATIONS:
        dist = rel * dilation
        ret = np.where(dist > 0, nb, 0)
        n = np.abs(dist)
        nf = np.maximum(n, max_exact).astype(np.float32)
        large = max_exact + (np.log(nf / np.float32(max_exact)) / np.float32(math.log(MAX_DISTANCE / max_exact))
                             * np.float32(nb - max_exact)).astype(np.int32)
        large = np.minimum(large, nb - 1)
        bucket = ret + np.where(n < max_exact, n, large)
        out.append(np.where(np.abs(rel) <= BAND_HALF, bucket, -1))
    return np.stack(out).astype(np.int32)


def _bias_kernel(table_ref, idx_ref, o_ref):
    chunk = 16
    index = _band_bucket_index()
    for pat in range(len(DILATIONS)):
        present = [int(b) for b in np.unique(index[pat]) if b >= 0]
        for lo in range(0, BAND_BLOCK, chunk):
            idx = idx_ref[pat, lo:lo + chunk, :]
            acc = [jnp.full(idx.shape, NEG_INF, F32) for _ in range(N_HEADS_A)]
            for b in present:
                hit = idx == b
                for h in range(N_HEADS_A):
                    acc[h] = jnp.where(hit, table_ref[b, h] * LOG2_E, acc[h])
            for h in range(N_HEADS_A):
                o_ref[pat, h, lo:lo + chunk, :] = acc[h]


def _bias_call(rel_bias):
    idx = jnp.asarray(_band_bucket_index())
    n_pat = len(DILATIONS)
    return pl.pallas_call(
        _bias_kernel,
        in_specs=[pl.BlockSpec(memory_space=pltpu.SMEM), pl.BlockSpec(memory_space=pltpu.VMEM)],
        out_specs=pl.BlockSpec(memory_space=pltpu.VMEM),
        out_shape=jax.ShapeDtypeStruct((n_pat, N_HEADS_A, BAND_BLOCK, BAND_SPAN), F32),
        name="bias_bands",
    )(rel_bias, idx)


def _band_kernel(*refs, mode):
    n_in = 3 * len(DILATIONS)
    ins, bias_ref = refs[:n_in], refs[n_in]
    g_ref = refs[n_in + 1] if mode != "max" else None
    c_ref = refs[n_in + 2] if mode == "attn_shift" else None
    o_ref, nums_ref, dens_ref = refs[-3:]
    n = pl.program_id(1)
    last = pl.num_programs(1) - 1
    col = lax.broadcasted_iota(jnp.int32, (1, BAND_SPAN), 1)
    edge_lo = jnp.where((n == 0) & (col < BAND_HALF), NEG_INF, 0.0).astype(F32)
    edge_hi = jnp.where((n == last) & (col >= BAND_BLOCK + BAND_HALF), NEG_INF, 0.0).astype(F32)
    lane = lax.broadcasted_iota(jnp.int32, (BAND_BLOCK, LANES), 1)
    first_half = lane < HEAD_DIM
    ones = jnp.ones((BAND_SPAN, LANES), BF16)

    def band(main, prev, nxt, blk, n_blk, ls):
        lo = blk * BAND_BLOCK - BAND_HALF
        parts = []
        if blk == 0:
            parts.append(prev[0, :, ls])
        start = max(lo, 0)
        stop = min(lo + BAND_SPAN, n_blk * BAND_BLOCK)
        parts.append(main[0, start:stop, ls])
        if blk == n_blk - 1:
            parts.append(nxt[0, :, ls])
        return parts[0] if len(parts) == 1 else jnp.concatenate(parts, axis=0)

    units = []
    for pp in range(nums_ref.shape[0]):
        for pat in reversed(range(len(DILATIONS))):
            dil = DILATIONS[pat]
            main_ref, prev_ref, next_ref = ins[3 * pat:3 * pat + 3]
            n_blk = SUPER_BLOCK // (dil * BAND_BLOCK)
            for r in range(dil):
                for blk in range(n_blk):
                    units.append((pat, dil, r, blk, n_blk, main_ref, prev_ref, next_ref, pp))

    def stage_scores(u):
        pat, dil, r, blk, n_blk, main_ref, prev_ref, next_ref, pp = u
        q = main_ref[0, blk * BAND_BLOCK:(blk + 1) * BAND_BLOCK, _qkv_lanes(pp, dil, r, 0)]
        kband = band(main_ref, prev_ref, next_ref, blk, n_blk, _qkv_lanes(pp, dil, r, 1))
        zero = jnp.zeros_like(q)
        return [_nt_dot(jnp.where(first_half, q, zero), kband), _nt_dot(jnp.where(first_half, zero, q), kband)]

    def stage_values(u, s_pair):
        pat, dil, r, blk, n_blk, main_ref, prev_ref, next_ref, pp = u
        out_ls = slice(pp * LANES, (pp + 1) * LANES)
        num_ref, den_ref = nums_ref.at[pp], dens_ref.at[pp]
        if dil == 1:
            rows = slice(blk * BAND_BLOCK, (blk + 1) * BAND_BLOCK)
        else:
            rows = pl.ds(dil * blk * BAND_BLOCK + r, BAND_BLOCK, stride=dil)
        def biased(half):
            s = s_pair[half] + bias_ref[pat, 2 * pp + half]
            if blk == 0:
                s = s + edge_lo
            if blk == n_blk - 1:
                s = s + edge_hi
            return s

        if mode == "max":
            mx = jnp.where(first_half, biased(0).max(axis=-1, keepdims=True), biased(1).max(axis=-1, keepdims=True))
            if dil == DILATIONS[-1]:
                num_ref[rows, :] = mx
            elif dil > 1:
                num_ref[rows, :] = jnp.maximum(num_ref[rows, :], mx)
            else:
                o_ref[0, pp, rows, :] = jnp.maximum(num_ref[rows, :], mx)
            return
        vband = band(main_ref, prev_ref, next_ref, blk, n_blk, _qkv_lanes(pp, dil, r, 2))
        vext = jnp.concatenate([vband, ones], axis=1)
        outs = []
        for half in range(2):
            s = biased(half)
            if mode == "attn_shift":
                s = s - c_ref[0, pp, rows, :][:, half * HEAD_DIM:half * HEAD_DIM + 1]
            outs.append(_dot(jnp.exp2(s).astype(BF16), vext))
        num = jnp.where(first_half, outs[0][:, :LANES], outs[1][:, :LANES])
        den = jnp.where(first_half, outs[0][:, LANES:], outs[1][:, LANES:])
        if dil == DILATIONS[-1]:
            num_ref[rows, :] = num
            den_ref[rows, :] = den
        elif dil > 1:
            num_ref[rows, :] = num_ref[rows, :] + num
            den_ref[rows, :] = den_ref[rows, :] + den
        else:
            y = (num_ref[rows, :] + num) / (den_ref[rows, :] + den)
            o_ref[0, rows, out_ls] = (y * _silu(g_ref[0, rows, out_ls].astype(F32))).astype(BF16)

    pending = [stage_scores(u) for u in units[:LOOKAHEAD]]
    for t, u in enumerate(units):
        if t + LOOKAHEAD < len(units):
            pending.append(stage_scores(units[t + LOOKAHEAD]))
        stage_values(u, pending[t])
        pending[t] = None


def _band_call(layouts, bias, gate, shift=None, mode="attn"):
    B, S, _ = layouts[0].shape
    n_sb = S // SUPER_BLOCK
    pp = PAIRS_PER_STEP
    in_specs, args = [], []
    for qkv, dil in zip(layouts, DILATIONS):
        rows = SUPER_BLOCK // dil
        width = pp * dil * 3 * LANES
        n_halo = (S // dil) // BAND_HALF
        per_sb = rows // BAND_HALF
        main = pl.BlockSpec((1, rows, width), lambda b, n, pr: (b, n, pr))
        prev = pl.BlockSpec((1, BAND_HALF, width),
                            lambda b, n, pr, per_sb=per_sb: (b, jnp.maximum(n * per_sb - 1, 0), pr))
        nxt = pl.BlockSpec((1, BAND_HALF, width),
                           lambda b, n, pr, per_sb=per_sb, n_halo=n_halo: (b, jnp.minimum((n + 1) * per_sb, n_halo - 1), pr))
        in_specs += [main, prev, nxt]
        args += [qkv, qkv, qkv]
    in_specs.append(pl.BlockSpec((len(DILATIONS), 2 * pp, BAND_BLOCK, BAND_SPAN), lambda b, n, pr: (0, pr, 0, 0)))
    args.append(bias)
    tok_blk = pl.BlockSpec((1, SUPER_BLOCK, pp * LANES), lambda b, n, pr: (b, n, pr))
    shift_blk = pl.BlockSpec((1, pp, SUPER_BLOCK, LANES), lambda b, n, pr: (b, pr, n, 0))
    if mode != "max":
        in_specs.append(tok_blk)
        args.append(gate)
    if mode == "attn_shift":
        in_specs.append(shift_blk)
        args.append(shift)
    if mode == "max":
        out_specs, out_shape = shift_blk, jax.ShapeDtypeStruct((B, N_PAIRS, S, LANES), F32)
    else:
        out_specs, out_shape = tok_blk, jax.ShapeDtypeStruct((B, S, D_A), BF16)
    return pl.pallas_call(
        functools.partial(_band_kernel, mode=mode),
        grid=(B, n_sb, N_PAIRS // pp),
        in_specs=in_specs,
        out_specs=out_specs,
        out_shape=out_shape,
        scratch_shapes=[pltpu.VMEM((pp, SUPER_BLOCK, LANES), F32)] * 2,
        compiler_params=pltpu.CompilerParams(
            dimension_semantics=("arbitrary", "arbitrary", "arbitrary"), vmem_limit_bytes=VMEM_LIMIT),
        name="band_" + mode,
    )(*args)


def _out_kernel(x_ref, p_ref, ya_ref, ybt_ref, wo_ref, wple_ref, wgate_ref, o_ref):
    tm = x_ref.shape[1]
    yb = ybt_ref[0].reshape(D_B, tm).astype(F32).T.astype(BF16)
    x1 = x_ref[0] + _dot(ya_ref[0], wo_ref[:D_A, :]) + _dot(yb, wo_ref[D_A:, :])
    gate = jax.nn.sigmoid(_dot(x1.astype(BF16), wgate_ref[...]))
    ple = _dot(p_ref[0].astype(BF16), wple_ref[...])
    o_ref[0] = x1 + ple * gate


def _out_call(x, p, ya, ybt, wo, wple, wgate):
    B, S, D = x.shape
    tm = ROW_TILE
    full = lambda shape: pl.BlockSpec(shape, lambda b, i: (0,) * len(shape))
    tok = lambda c: pl.BlockSpec((1, tm, c), lambda b, i: (b, i, 0))
    return pl.pallas_call(
        _out_kernel,
        grid=(B, S // tm),
        in_specs=[tok(D), tok(p.shape[-1]), tok(D_A),
                  pl.BlockSpec((1, N_HEADS_B, HEAD_DIM, tm), lambda b, i: (b, 0, 0, i)),
                  full(wo.shape), full(wple.shape), full(wgate.shape)],
        out_specs=tok(D),
        out_shape=jax.ShapeDtypeStruct((B, S, D), F32),
        compiler_params=pltpu.CompilerParams(
            dimension_semantics=("arbitrary", "arbitrary"), vmem_limit_bytes=VMEM_LIMIT),
        name="out_proj",
    )(x, p, ya, ybt, wo, wple, wgate)


def _rope_tables(S):
    n = HEAD_DIM // 4
    inv = ROPE_THETA ** (-jnp.arange(n, dtype=F32) / n)
    rows = S // GRID_W
    ang_row = inv[:, None] * jnp.arange(rows, dtype=jnp.int32).astype(F32)[None, :]
    ang_col = inv[:, None] * jnp.arange(GRID_W, dtype=jnp.int32).astype(F32)[None, :]
    expand = lambda fn: jnp.concatenate(
        [jnp.repeat(fn(ang_row), GRID_W, axis=1), jnp.tile(fn(ang_col), (1, rows))], axis=0)
    return expand(jnp.cos), expand(jnp.sin)


def _pairs_apart(t):
    lead, tail = t.shape[:-2], t.shape[-1]
    heads = t.shape[-2] // HEAD_DIM
    t = t.reshape(*lead, heads, HEAD_DIM // 2, 2, tail)
    return jnp.swapaxes(t, -3, -2).reshape(*lead, heads * HEAD_DIM, tail)


def kernel(x, p, ln_g, w_in, qn_a, kn_a, qn_b, kn_b, w_out, w_ple, w_pgate, rel_bias):
    B, S, D = x.shape
    depth = w_in.shape[0]
    cos_t, sin_t = _rope_tables(S)
    for i in range(depth):
        wfeat = w_in[i].T.astype(BF16)
        wrope = _pairs_apart(wfeat[4 * D_A:4 * D_A + D_B + D_KV_B])
        wtok = w_in[i][:, 2 * D_A:4 * D_A].astype(BF16)
        gains_b = _pairs_apart(jnp.stack([qn_b[i], kn_b[i]], axis=-1))
        gains = jnp.concatenate([jnp.stack([qn_a[i], kn_a[i]]), gains_b.T]).astype(F32)
        gains = jnp.broadcast_to(gains[:, :, None], (4, HEAD_DIM, ROW_TILE // PROJ_SUBTILES))

        outs = _proj_call(
            x, ln_g[i][None, :], wtok, wfeat, wrope, gains, cos_t, sin_t)
        ga, gbt, qkv1, qbt, kb, vbt = outs

        gmax = jnp.max(jnp.abs(jnp.stack([qn_a[i], kn_a[i], qn_b[i], kn_b[i]]).astype(F32)), axis=1)
        root = HEAD_DIM ** 0.5
        bound_a = root * gmax[0] * gmax[1] + jnp.max(jnp.abs(rel_bias.astype(F32)))
        bound_b = root * gmax[2] * gmax[3]
        safe = jnp.maximum(bound_a, bound_b) <= SAFE_EXPONENT

        def unshifted(ops):
            qbt, kb, vbt, gbt, qkv1, rel_bias, ga = ops
            bias = _bias_call(rel_bias)
            ybt, (qkv4, qkv16) = _gqa_call(qbt, kb, vbt, gbt, qkv1)
            return ybt, _band_call((qkv1, qkv4, qkv16), bias, ga)

        def shifted(ops):
            qbt, kb, vbt, gbt, qkv1, rel_bias, ga = ops
            bias = _bias_call(rel_bias)
            shift_b = _shift_rows(_gqa_call(qbt, kb, vbt, gbt, qkv1, mode="max"))
            ybt, (qkv4, qkv16) = _gqa_call(qbt, kb, vbt, gbt, qkv1, shift_b, mode="attn_shift")
            layouts = (qkv1, qkv4, qkv16)
            shift_a = _band_call(layouts, bias, ga, mode="max")
            return ybt, _band_call(layouts, bias, ga, shift_a, mode="attn_shift")

        ybt, ya = lax.cond(safe, unshifted, shifted, (qbt, kb, vbt, gbt, qkv1, rel_bias, ga))
        x = _out_call(x, p[i], ya, ybt, w_out[i].astype(BF16), w_ple[i].astype(BF16), w_pgate[i].astype(BF16))
    return x
```

```python
import functools
import math

import numpy as np
import jax
import jax.numpy as jnp
from jax import lax
from jax.experimental import pallas as pl
from jax.experimental.pallas import tpu as pltpu

HEAD_DIM = 64
N_HEADS_A = 8
N_HEADS_B = 8
N_KV_B = 2
GQA_GROUP = N_HEADS_B // N_KV_B
D_A = N_HEADS_A * HEAD_DIM
D_B = N_HEADS_B * HEAD_DIM
D_KV_B = N_KV_B * HEAD_DIM
DILATIONS = (1, 4, 16)
BAND_BLOCK = 128
BAND_HALF = 64
BAND_SPAN = BAND_BLOCK + 2 * BAND_HALF
SUPER_BLOCK = BAND_BLOCK * max(DILATIONS)
LANES = 128
N_PAIRS = D_A // LANES
PAIRS_PER_STEP = 2
GRID_W = 64
ROPE_THETA = 10000.0
N_BUCKETS = 32
MAX_DISTANCE = 1024
EPS = 1e-6
NEG_INF = -1e30

ROW_TILE = 1024
PROJ_SUBTILES = 4
Q_TILE = 512
Q_TILES_PER_STEP = 2
KEY_STEP = 256
LOOKAHEAD = 2
V7X_VMEM_BYTES = 64 * 1024 * 1024
VMEM_LIMIT = V7X_VMEM_BYTES - 8 * 1024 * 1024
F32_SUBLANES = 8
LOG2_E = math.log2(math.e)
SAFE_EXPONENT = 60.0
SHIFT_TERMS = 3
SHIFT_ROWS = 16

F32 = jnp.float32
BF16 = jnp.bfloat16


def _nt_dot(a, b):
    return lax.dot_general(a, b, (((1,), (1,)), ((), ())), preferred_element_type=F32)


def _dot(a, b):
    return jnp.dot(a, b, preferred_element_type=F32)


def _silu(g):
    return g * jax.nn.sigmoid(g)


def _head_rms(t, gain):
    ms = jnp.mean(t * t, axis=1, keepdims=True)
    return t * lax.rsqrt(ms + EPS) * gain[None]


def _rope_halves(t, cos, sin):
    half = HEAD_DIM // 2
    x1 = t[:, :half, :]
    x2 = t[:, half:, :]
    c = cos[None]
    s = sin[None]
    return jnp.concatenate([x1 * c - x2 * s, x1 * s + x2 * c], axis=1)


def _zero_from(tile, width):
    bits = pltpu.bitcast(tile, jnp.uint32)
    zero = lax.shift_right_logical(lax.shift_right_logical(bits, jnp.uint32(16)), jnp.uint32(16))
    return jnp.tile(pltpu.bitcast(zero, F32), (1, width // LANES))


def _qkv_lanes(pr, dil, r, j):
    lo = ((pr * dil + r) * 3 + j) * LANES
    return slice(lo, lo + LANES)


def _residue_layout_jobs(src, j, scr, o4, o16):
    t = src.shape[1]

    def fill(pr):
        scr[pr] = src[0, :, _qkv_lanes(pr, 1, 0, j)].astype(F32)
        return None

    def gather(pr, dil, out, r):
        lanes = _qkv_lanes(pr, dil, r, j)
        lo = lanes.start
        out[0, :, lanes] = scr[pr, pl.ds(r, t // dil, stride=dil), :].astype(BF16)
        row = pl.multiple_of(jnp.minimum(pl.program_id(0), 0), 16)
        return out[0, pl.ds(row, 16), lo:lo + LANES]

    jobs = [functools.partial(fill, pr) for pr in range(N_PAIRS)]
    for pr in range(N_PAIRS):
        for dil, out in ((4, o4), (16, o16)):
            jobs += [functools.partial(gather, pr, dil, out, r) for r in range(dil)]
    return jobs


def _proj_kernel(x_ref, g_ref, wtok_ref, wfeat_ref, wrope_ref, gains_ref, cos_ref, sin_ref,
                 ga_ref, gbt_ref, qkv_ref, qbt_ref, kb_ref, vbt_ref):
    for sub in range(PROJ_SUBTILES):
        _proj_subtile(sub, x_ref, g_ref, wtok_ref, wfeat_ref, wrope_ref, gains_ref, cos_ref, sin_ref,
                      ga_ref, gbt_ref, qkv_ref, qbt_ref, kb_ref, vbt_ref)


def _proj_subtile(sub, x_ref, g_ref, wtok_ref, wfeat_ref, wrope_ref, gains_ref, cos_ref, sin_ref,
                  ga_ref, gbt_ref, qkv_ref, qbt_ref, kb_ref, vbt_ref):
    t = x_ref.shape[1] // PROJ_SUBTILES
    row0 = sub * t
    rows = slice(row0, row0 + t)
    scale = HEAD_DIM ** -0.5 * LOG2_E
    x = x_ref[0, rows, :]
    ms = jnp.mean(x * x, axis=-1, keepdims=True)
    h = (x * lax.rsqrt(ms + EPS) * g_ref[...]).astype(BF16)

    def store_pairs(j, y):
        for pr in range(N_PAIRS):
            qkv_ref[0, rows, _qkv_lanes(pr, 1, 0, j)] = y[:, pr * LANES:(pr + 1) * LANES].astype(BF16)

    qa = _nt_dot(wfeat_ref[:D_A, :], h).reshape(N_HEADS_A, HEAD_DIM, t)
    qa = _head_rms(qa, gains_ref[0]) * scale
    store_pairs(0, qa.reshape(D_A, t).T)
    ka = _nt_dot(wfeat_ref[D_A:2 * D_A, :], h).reshape(N_HEADS_A, HEAD_DIM, t)
    ka = _head_rms(ka, gains_ref[1])
    store_pairs(1, ka.reshape(D_A, t).T)
    store_pairs(2, _dot(h, wtok_ref[:, :D_A]))
    ga_ref[0, rows, :] = _dot(h, wtok_ref[:, D_A:]).astype(BF16)

    rope = _nt_dot(wrope_ref[...], h)
    o = 4 * D_A + D_B + D_KV_B
    vg = _nt_dot(wfeat_ref[o:, :], h)
    gbt_ref[0, :, :, rows] = vg[D_KV_B:].reshape(N_HEADS_B, HEAD_DIM, t).astype(BF16)
    cos = cos_ref[:, rows]
    sin = sin_ref[:, rows]
    qb = _head_rms(rope[:D_B].reshape(N_HEADS_B, HEAD_DIM, t), gains_ref[2])
    qbt_ref[0, :, :, rows] = (_rope_halves(qb, cos, sin) * scale).astype(BF16)
    kb = _head_rms(rope[D_B:].reshape(N_KV_B, HEAD_DIM, t), gains_ref[3])
    kb = _rope_halves(kb, cos, sin)
    ext = (lax.broadcasted_iota(jnp.int32, (HEAD_DIM, t), 0) < SHIFT_TERMS).astype(F32)
    for kv in range(N_KV_B):
        kb_ref[0, kv, rows, :] = jnp.concatenate([kb[kv], ext], axis=0).T.astype(BF16)
    vbt_ref[0, :, 0, :, rows] = vg[:D_KV_B].reshape(N_KV_B, HEAD_DIM, t).astype(BF16)


def _proj_call(x, ln_g, wtok, wfeat, wrope, gains, cos_t, sin_t):
    B, S, D = x.shape
    tm = ROW_TILE
    nt = S // tm
    full = lambda shape: pl.BlockSpec(shape, lambda b, i: (0,) * len(shape))
    tok_spec = pl.BlockSpec((1, tm, D_A), lambda b, i: (b, i, 0))
    tok_shape = jax.ShapeDtypeStruct((B, S, D_A), BF16)
    out_shape = (
        tok_shape,
        jax.ShapeDtypeStruct((B, N_HEADS_B, HEAD_DIM, S), BF16),
        jax.ShapeDtypeStruct((B, S, 3 * D_A), BF16),
        jax.ShapeDtypeStruct((B, N_HEADS_B, HEAD_DIM, S), BF16),
        jax.ShapeDtypeStruct((B, N_KV_B, S, 2 * HEAD_DIM), BF16),
        jax.ShapeDtypeStruct((B, N_KV_B, nt, HEAD_DIM, tm), BF16),
    )
    out_specs = (
        tok_spec, pl.BlockSpec((1, N_HEADS_B, HEAD_DIM, tm), lambda b, i: (b, 0, 0, i)),
        pl.BlockSpec((1, tm, 3 * D_A), lambda b, i: (b, i, 0)),
        pl.BlockSpec((1, N_HEADS_B, HEAD_DIM, tm), lambda b, i: (b, 0, 0, i)),
        pl.BlockSpec((1, N_KV_B, tm, 2 * HEAD_DIM), lambda b, i: (b, 0, i, 0)),
        pl.BlockSpec((1, N_KV_B, 1, HEAD_DIM, tm), lambda b, i: (b, 0, i, 0, 0)),
    )
    in_specs = [
        pl.BlockSpec((1, tm, D), lambda b, i: (b, i, 0)),
        full((1, D)),
        full(wtok.shape),
        full(wfeat.shape),
        full(wrope.shape),
        full(gains.shape),
        pl.BlockSpec((HEAD_DIM // 2, tm), lambda b, i: (0, i)),
        pl.BlockSpec((HEAD_DIM // 2, tm), lambda b, i: (0, i)),
    ]
    return pl.pallas_call(
        _proj_kernel,
        grid=(B, nt),
        in_specs=in_specs,
        out_specs=out_specs,
        out_shape=out_shape,
        compiler_params=pltpu.CompilerParams(
            dimension_semantics=("arbitrary", "arbitrary"), vmem_limit_bytes=VMEM_LIMIT),
        name="proj",
    )(x, ln_g, wtok, wfeat, wrope, gains, cos_t, sin_t)


def _gqa_kernel(*refs, mode):
    if mode == "max":
        qt_ref, k_ref, o_ref = refs
    else:
        n_in = 5 if mode == "attn_shift" else 4
        qt_ref, k_ref, vt_ref, gt_ref = refs[:4]
        c_ref = refs[4] if mode == "attn_shift" else None
        qkv_ref, o_ref, qkv4_ref, qkv16_ref = refs[n_in:n_in + 4]
        scr_refs = refs[n_in + 4:]
        side_jobs = []
        for j in range(3):
            side_jobs += _residue_layout_jobs(qkv_ref, j, scr_refs[j], qkv4_ref, qkv16_ref)
    tq = Q_TILE
    n_steps = k_ref.shape[2] // KEY_STEP
    sub = F32_SUBLANES
    heads = [(g, slice(j * tq, (j + 1) * tq)) for j in range(qt_ref.shape[3] // tq) for g in range(GQA_GROUP)]
    qext = {}
    for g, cols in heads:
        q = qt_ref[0, g, :, cols]
        if mode == "attn_shift":
            pad = jnp.zeros((HEAD_DIM - SHIFT_ROWS, tq), BF16)
            qext[g, cols.start] = jnp.concatenate([q, c_ref[0, g, :, cols], pad], axis=0)
        else:
            qext[g, cols.start] = jnp.concatenate([q, jnp.zeros_like(q)], axis=0)
    units = [(g, cols, t) for g, cols in heads for t in range(n_steps)]

    def scores(u):
        g, cols, t = u
        return _dot(k_ref[0, 0, t * KEY_STEP:(t + 1) * KEY_STEP, :], qext[g, cols.start])

    pending = [scores(u) for u in units[:LOOKAHEAD]]
    for i, (g, cols, t) in enumerate(units):
        if i + LOOKAHEAD < len(units):
            pending.append(scores(units[i + LOOKAHEAD]))
        s = pending[i]
        pending[i] = None
        if mode == "max":
            part = s.reshape(KEY_STEP // sub, sub, tq).max(axis=0)
            mx = part if t == 0 else jnp.maximum(mx, part)
            if t == n_steps - 1:
                o_ref[0, g, :, cols] = jnp.broadcast_to(mx.max(axis=0, keepdims=True), (sub, tq))
            continue
        if t == 0:
            acc = jnp.zeros((HEAD_DIM, tq), F32)
            den = jnp.zeros((sub, tq), F32)
        p = jnp.exp2(s)
        den = den + p.reshape(KEY_STEP // sub, sub, tq).sum(axis=0)
        if side_jobs:
            back = side_jobs.pop(0)()
            if back is not None:
                den = den + _zero_from(back, tq)
        tk = vt_ref.shape[4]
        lo = (t * KEY_STEP) % tk
        acc = acc + _dot(vt_ref[0, 0, (t * KEY_STEP) // tk, :, lo:lo + KEY_STEP], p.astype(BF16))
        if t == n_steps - 1:
            y = acc / den.sum(axis=0, keepdims=True)
            o_ref[0, g, :, cols] = (y * _silu(gt_ref[0, g, :, cols].astype(F32))).astype(BF16)
    if mode != "max":
        for job in side_jobs:
            job()


def _gqa_call(qbt, kb, vbt, gbt, window_qkv, shift=None, mode="attn"):
    B, H, _, S = qbt.shape
    tq = Q_TILE * min(Q_TILES_PER_STEP, S // Q_TILE)
    n_q = S // tq
    nt, rows, tk = vbt.shape[2], vbt.shape[3], vbt.shape[4]
    head_blk = lambda r: pl.BlockSpec((1, GQA_GROUP, r, tq), lambda b, kv, i: (b, kv, 0, i))
    in_specs = [head_blk(HEAD_DIM), pl.BlockSpec((1, 1, S, 2 * HEAD_DIM), lambda b, kv, i: (b, kv, 0, 0))]
    args = [qbt, kb]
    grid = (B, N_KV_B, n_q)
    params = pltpu.CompilerParams(
        dimension_semantics=("arbitrary", "arbitrary", "arbitrary"), vmem_limit_bytes=VMEM_LIMIT)
    if mode == "max":
        return pl.pallas_call(
            functools.partial(_gqa_kernel, mode=mode), grid=grid, in_specs=in_specs, out_specs=head_blk(8),
            out_shape=jax.ShapeDtypeStruct((B, H, 8, S), F32), compiler_params=params, name="gqa_" + mode,
        )(*args)
    in_specs += [pl.BlockSpec((1, 1, nt, rows, tk), lambda b, kv, i: (b, kv, 0, 0, 0)), head_blk(HEAD_DIM)]
    args += [vbt, gbt]
    if mode == "attn_shift":
        in_specs.append(head_blk(SHIFT_ROWS))
        args.append(shift)
    t = S // (N_KV_B * n_q)
    tile = lambda b, kv, i: (b, kv * n_q + i, 0)
    in_specs.append(pl.BlockSpec((1, t, 3 * D_A), tile))
    args.append(window_qkv)
    out_specs, out_shape = [head_blk(HEAD_DIM)], [jax.ShapeDtypeStruct((B, H, HEAD_DIM, S), BF16)]
    for dil in DILATIONS[1:]:
        out_specs.append(pl.BlockSpec((1, t // dil, dil * 3 * D_A), tile))
        out_shape.append(jax.ShapeDtypeStruct((B, S // dil, dil * 3 * D_A), BF16))
    outs = pl.pallas_call(
        functools.partial(_gqa_kernel, mode=mode), grid=grid, in_specs=in_specs, out_specs=out_specs,
        out_shape=out_shape, scratch_shapes=[pltpu.VMEM((N_PAIRS, t, LANES), F32)] * 3,
        compiler_params=params, name="gqa_" + mode,
    )(*args)
    return outs[0], outs[1:]


def _shift_rows(m):
    rest = m[:, :, :1, :]
    terms = []
    for _ in range(SHIFT_TERMS):
        t = rest.astype(BF16)
        terms.append(-t)
        rest = rest - t.astype(F32)
    pad = jnp.zeros(m.shape[:2] + (SHIFT_ROWS - SHIFT_TERMS, m.shape[3]), BF16)
    return jnp.concatenate(terms + [pad], axis=2)


def _band_bucket_index():
    nb = N_BUCKETS // 2
    max_exact = nb // 2
    rel = np.arange(BAND_SPAN)[None, :] - BAND_HALF - np.arange(BAND_BLOCK)[:, None]
    out = []
    for dilation in DILATIONS:
        dist = rel * dilation
        ret = np.where(dist > 0, nb, 0)
        n = np.abs(dist)
        nf = np.maximum(n, max_exact).astype(np.float32)
        large = max_exact + (np.log(nf / np.float32(max_exact)) / np.float32(math.log(MAX_DISTANCE / max_exact))
                             * np.float32(nb - max_exact)).astype(np.int32)
        large = np.minimum(large, nb - 1)
        bucket = ret + np.where(n < max_exact, n, large)
        out.append(np.where(np.abs(rel) <= BAND_HALF, bucket, -1))
    return np.stack(out).astype(np.int32)


def _bias_kernel(table_ref, idx_ref, o_ref):
    chunk = 16
    index = _band_bucket_index()
    for pat in range(len(DILATIONS)):
        present = [int(b) for b in np.unique(index[pat]) if b >= 0]
        for lo in range(0, BAND_BLOCK, chunk):
            idx = idx_ref[pat, lo:lo + chunk, :]
            acc = [jnp.full(idx.shape, NEG_INF, F32) for _ in range(N_HEADS_A)]
            for b in present:
                hit = idx == b
                for h in range(N_HEADS_A):
                    acc[h] = jnp.where(hit, table_ref[b, h] * LOG2_E, acc[h])
            for h in range(N_HEADS_A):
                o_ref[pat, h, lo:lo + chunk, :] = acc[h]


def _bias_call(rel_bias):
    idx = jnp.asarray(_band_bucket_index())
    n_pat = len(DILATIONS)
    return pl.pallas_call(
        _bias_kernel,
        in_specs=[pl.BlockSpec(memory_space=pltpu.SMEM), pl.BlockSpec(memory_space=pltpu.VMEM)],
        out_specs=pl.BlockSpec(memory_space=pltpu.VMEM),
        out_shape=jax.ShapeDtypeStruct((n_pat, N_HEADS_A, BAND_BLOCK, BAND_SPAN), F32),
        name="bias_bands",
    )(rel_bias, idx)


def _band_kernel(*refs, mode):
    n_in = 3 * len(DILATIONS)
    ins, bias_ref = refs[:n_in], refs[n_in]
    g_ref = refs[n_in + 1] if mode != "max" else None
    c_ref = refs[n_in + 2] if mode == "attn_shift" else None
    o_ref, nums_ref, dens_ref = refs[-3:]
    n = pl.program_id(1)
    last = pl.num_programs(1) - 1
    col = lax.broadcasted_iota(jnp.int32, (1, BAND_SPAN), 1)
    edge_lo = jnp.where((n == 0) & (col < BAND_HALF), NEG_INF, 0.0).astype(F32)
    edge_hi = jnp.where((n == last) & (col >= BAND_BLOCK + BAND_HALF), NEG_INF, 0.0).astype(F32)
    lane = lax.broadcasted_iota(jnp.int32, (BAND_BLOCK, LANES), 1)
    first_half = lane < HEAD_DIM
    ones = jnp.ones((BAND_SPAN, LANES), BF16)

    def band(main, prev, nxt, blk, n_blk, ls):
        lo = blk * BAND_BLOCK - BAND_HALF
        parts = []
        if blk == 0:
            parts.append(prev[0, :, ls])
        start = max(lo, 0)
        stop = min(lo + BAND_SPAN, n_blk * BAND_BLOCK)
        parts.append(main[0, start:stop, ls])
        if blk == n_blk - 1:
            parts.append(nxt[0, :, ls])
        return parts[0] if len(parts) == 1 else jnp.concatenate(parts, axis=0)

    units = []
    for pp in range(nums_ref.shape[0]):
        for pat in reversed(range(len(DILATIONS))):
            dil = DILATIONS[pat]
            main_ref, prev_ref, next_ref = ins[3 * pat:3 * pat + 3]
            n_blk = SUPER_BLOCK // (dil * BAND_BLOCK)
            for r in range(dil):
                for blk in range(n_blk):
                    units.append((pat, dil, r, blk, n_blk, main_ref, prev_ref, next_ref, pp))

    def stage_scores(u):
        pat, dil, r, blk, n_blk, main_ref, prev_ref, next_ref, pp = u
        q = main_ref[0, blk * BAND_BLOCK:(blk + 1) * BAND_BLOCK, _qkv_lanes(pp, dil, r, 0)]
        kband = band(main_ref, prev_ref, next_ref, blk, n_blk, _qkv_lanes(pp, dil, r, 1))
        zero = jnp.zeros_like(q)
        return [_nt_dot(jnp.where(first_half, q, zero), kband), _nt_dot(jnp.where(first_half, zero, q), kband)]

    def stage_values(u, s_pair):
        pat, dil, r, blk, n_blk, main_ref, prev_ref, next_ref, pp = u
        out_ls = slice(pp * LANES, (pp + 1) * LANES)
        num_ref, den_ref = nums_ref.at[pp], dens_ref.at[pp]
        if dil == 1:
            rows = slice(blk * BAND_BLOCK, (blk + 1) * BAND_BLOCK)
        else:
            rows = pl.ds(dil * blk * BAND_BLOCK + r, BAND_BLOCK, stride=dil)
        def biased(half):
            s = s_pair[half] + bias_ref[pat, 2 * pp + half]
            if blk == 0:
                s = s + edge_lo
            if blk == n_blk - 1:
                s = s + edge_hi
            return s

        if mode == "max":
            mx = jnp.where(first_half, biased(0).max(axis=-1, keepdims=True), biased(1).max(axis=-1, keepdims=True))
            if dil == DILATIONS[-1]:
                num_ref[rows, :] = mx
            elif dil > 1:
                num_ref[rows, :] = jnp.maximum(num_ref[rows, :], mx)
            else:
                o_ref[0, pp, rows, :] = jnp.maximum(num_ref[rows, :], mx)
            return
        vband = band(main_ref, prev_ref, next_ref, blk, n_blk, _qkv_lanes(pp, dil, r, 2))
        vext = jnp.concatenate([vband, ones], axis=1)
        outs = []
        for half in range(2):
            s = biased(half)
            if mode == "attn_shift":
                s = s - c_ref[0, pp, rows, :][:, half * HEAD_DIM:half * HEAD_DIM + 1]
            outs.append(_dot(jnp.exp2(s).astype(BF16), vext))
        num = jnp.where(first_half, outs[0][:, :LANES], outs[1][:, :LANES])
        den = jnp.where(first_half, outs[0][:, LANES:], outs[1][:, LANES:])
        if dil == DILATIONS[-1]:
            num_ref[rows, :] = num
            den_ref[rows, :] = den
        elif dil > 1:
            num_ref[rows, :] = num_ref[rows, :] + num
            den_ref[rows, :] = den_ref[rows, :] + den
        else:
            y = (num_ref[rows, :] + num) / (den_ref[rows, :] + den)
            o_ref[0, rows, out_ls] = (y * _silu(g_ref[0, rows, out_ls].astype(F32))).astype(BF16)

    pending = [stage_scores(u) for u in units[:LOOKAHEAD]]
    for t, u in enumerate(units):
        if t + LOOKAHEAD < len(units):
            pending.append(stage_scores(units[t + LOOKAHEAD]))
        stage_values(u, pending[t])
        pending[t] = None


def _band_call(layouts, bias, gate, shift=None, mode="attn"):
    B, S, _ = layouts[0].shape
    n_sb = S // SUPER_BLOCK
    pp = PAIRS_PER_STEP
    in_specs, args = [], []
    for qkv, dil in zip(layouts, DILATIONS):
        rows = SUPER_BLOCK // dil
        width = pp * dil * 3 * LANES
        n_halo = (S // dil) // BAND_HALF
        per_sb = rows // BAND_HALF
        main = pl.BlockSpec((1, rows, width), lambda b, n, pr: (b, n, pr))
        prev = pl.BlockSpec((1, BAND_HALF, width),
                            lambda b, n, pr, per_sb=per_sb: (b, jnp.maximum(n * per_sb - 1, 0), pr))
        nxt = pl.BlockSpec((1, BAND_HALF, width),
                           lambda b, n, pr, per_sb=per_sb, n_halo=n_halo: (b, jnp.minimum((n + 1) * per_sb, n_halo - 1), pr))
        in_specs += [main, prev, nxt]
        args += [qkv, qkv, qkv]
    in_specs.append(pl.BlockSpec((len(DILATIONS), 2 * pp, BAND_BLOCK, BAND_SPAN), lambda b, n, pr: (0, pr, 0, 0)))
    args.append(bias)
    tok_blk = pl.BlockSpec((1, SUPER_BLOCK, pp * LANES), lambda b, n, pr: (b, n, pr))
    shift_blk = pl.BlockSpec((1, pp, SUPER_BLOCK, LANES), lambda b, n, pr: (b, pr, n, 0))
    if mode != "max":
        in_specs.append(tok_blk)
        args.append(gate)
    if mode == "attn_shift":
        in_specs.append(shift_blk)
        args.append(shift)
    if mode == "max":
        out_specs, out_shape = shift_blk, jax.ShapeDtypeStruct((B, N_PAIRS, S, LANES), F32)
    else:
        out_specs, out_shape = tok_blk, jax.ShapeDtypeStruct((B, S, D_A), BF16)
    return pl.pallas_call(
        functools.partial(_band_kernel, mode=mode),
        grid=(B, n_sb, N_PAIRS // pp),
        in_specs=in_specs,
        out_specs=out_specs,
        out_shape=out_shape,
        scratch_shapes=[pltpu.VMEM((pp, SUPER_BLOCK, LANES), F32)] * 2,
        compiler_params=pltpu.CompilerParams(
            dimension_semantics=("arbitrary", "arbitrary", "arbitrary"), vmem_limit_bytes=VMEM_LIMIT),
        name="band_" + mode,
    )(*args)


def _out_kernel(x_ref, p_ref, ya_ref, ybt_ref, wo32_ref, wple32_ref, wgate32_ref, o_ref, wo_ref, wple_ref, wgate_ref):
    @pl.when((pl.program_id(0) == 0) & (pl.program_id(1) == 0))
    def _():
        wo_ref[...] = wo32_ref[...].astype(BF16)
        wple_ref[...] = wple32_ref[...].astype(BF16)
        wgate_ref[...] = wgate32_ref[...].astype(BF16)

    tm = x_ref.shape[1]
    yb = ybt_ref[0].reshape(D_B, tm).astype(F32).T.astype(BF16)
    x1 = x_ref[0] + _dot(ya_ref[0], wo_ref[:D_A, :]) + _dot(yb, wo_ref[D_A:, :])
    gate = jax.nn.sigmoid(_dot(x1.astype(BF16), wgate_ref[...]))
    ple = _dot(p_ref[0].astype(BF16), wple_ref[...])
    o_ref[0] = x1 + ple * gate


def _out_call(x, p, ya, ybt, wo, wple, wgate):
    B, S, D = x.shape
    tm = ROW_TILE
    full = lambda shape: pl.BlockSpec(shape, lambda b, i: (0,) * len(shape), pipeline_mode=pl.Buffered(1))
    tok = lambda c: pl.BlockSpec((1, tm, c), lambda b, i: (b, i, 0))
    return pl.pallas_call(
        _out_kernel,
        grid=(B, S // tm),
        in_specs=[tok(D), tok(p.shape[-1]), tok(D_A),
                  pl.BlockSpec((1, N_HEADS_B, HEAD_DIM, tm), lambda b, i: (b, 0, 0, i)),
                  full(wo.shape), full(wple.shape), full(wgate.shape)],
        out_specs=tok(D),
        out_shape=jax.ShapeDtypeStruct((B, S, D), F32),
        scratch_shapes=[pltpu.VMEM(w.shape, BF16) for w in (wo, wple, wgate)],
        compiler_params=pltpu.CompilerParams(
            dimension_semantics=("arbitrary", "arbitrary"), vmem_limit_bytes=VMEM_LIMIT),
        name="out_proj",
    )(x, p, ya, ybt, wo, wple, wgate)


def _rope_tables(S):
    n = HEAD_DIM // 4
    inv = ROPE_THETA ** (-jnp.arange(n, dtype=F32) / n)
    rows = S // GRID_W
    ang_row = inv[:, None] * jnp.arange(rows, dtype=jnp.int32).astype(F32)[None, :]
    ang_col = inv[:, None] * jnp.arange(GRID_W, dtype=jnp.int32).astype(F32)[None, :]
    expand = lambda fn: jnp.concatenate(
        [jnp.repeat(fn(ang_row), GRID_W, axis=1), jnp.tile(fn(ang_col), (1, rows))], axis=0)
    return expand(jnp.cos), expand(jnp.sin)


def _pairs_apart(t):
    lead, tail = t.shape[:-2], t.shape[-1]
    heads = t.shape[-2] // HEAD_DIM
    t = t.reshape(*lead, heads, HEAD_DIM // 2, 2, tail)
    return jnp.swapaxes(t, -3, -2).reshape(*lead, heads * HEAD_DIM, tail)


def kernel(x, p, ln_g, w_in, qn_a, kn_a, qn_b, kn_b, w_out, w_ple, w_pgate, rel_bias):
    B, S, D = x.shape
    depth = w_in.shape[0]
    cos_t, sin_t = _rope_tables(S)
    bias = _bias_call(rel_bias)
    for i in range(depth):
        wfeat = w_in[i].T.astype(BF16)
        wrope = _pairs_apart(wfeat[4 * D_A:4 * D_A + D_B + D_KV_B])
        wtok = w_in[i][:, 2 * D_A:4 * D_A].astype(BF16)
        gains_b = _pairs_apart(jnp.stack([qn_b[i], kn_b[i]], axis=-1))
        gains = jnp.concatenate([jnp.stack([qn_a[i], kn_a[i]]), gains_b.T]).astype(F32)
        gains = jnp.broadcast_to(gains[:, :, None], (4, HEAD_DIM, ROW_TILE // PROJ_SUBTILES))

        outs = _proj_call(
            x, ln_g[i][None, :], wtok, wfeat, wrope, gains, cos_t, sin_t)
        ga, gbt, qkv1, qbt, kb, vbt = outs

        gmax = jnp.max(jnp.abs(jnp.stack([qn_a[i], kn_a[i], qn_b[i], kn_b[i]]).astype(F32)), axis=1)
        root = HEAD_DIM ** 0.5
        bound_a = root * gmax[0] * gmax[1] + jnp.max(jnp.abs(rel_bias.astype(F32)))
        bound_b = root * gmax[2] * gmax[3]
        safe = jnp.maximum(bound_a, bound_b) <= SAFE_EXPONENT

        def unshifted(ops):
            qbt, kb, vbt, gbt, qkv1, bias, ga = ops
            ybt, (qkv4, qkv16) = _gqa_call(qbt, kb, vbt, gbt, qkv1)
            return ybt, _band_call((qkv1, qkv4, qkv16), bias, ga)

        def shifted(ops):
            qbt, kb, vbt, gbt, qkv1, bias, ga = ops
            shift_b = _shift_rows(_gqa_call(qbt, kb, vbt, gbt, qkv1, mode="max"))
            ybt, (qkv4, qkv16) = _gqa_call(qbt, kb, vbt, gbt, qkv1, shift_b, mode="attn_shift")
            layouts = (qkv1, qkv4, qkv16)
            shift_a = _band_call(layouts, bias, ga, mode="max")
            return ybt, _band_call(layouts, bias, ga, shift_a, mode="attn_shift")

        ybt, ya = lax.cond(safe, unshifted, shifted, (qbt, kb, vbt, gbt, qkv1, bias, ga))
        x = _out_call(x, p[i], ya, ybt, w_out[i], w_ple[i], w_pgate[i])
    return x
```

```python
import functools
import math

import numpy as np
import jax
import jax.numpy as jnp
from jax import lax
from jax.experimental import pallas as pl
from jax.experimental.pallas import tpu as pltpu

HEAD_DIM = 64
N_HEADS_A = 8
N_HEADS_B = 8
N_KV_B = 2
GQA_GROUP = N_HEADS_B // N_KV_B
D_A = N_HEADS_A * HEAD_DIM
D_B = N_HEADS_B * HEAD_DIM
D_KV_B = N_KV_B * HEAD_DIM
DILATIONS = (1, 4, 16)
BAND_BLOCK = 128
BAND_HALF = 64
BAND_SPAN = BAND_BLOCK + 2 * BAND_HALF
SUPER_BLOCK = BAND_BLOCK * max(DILATIONS)
LANES = 128
N_PAIRS = D_A // LANES
PAIRS_PER_STEP = 2
GRID_W = 64
ROPE_THETA = 10000.0
N_BUCKETS = 32
MAX_DISTANCE = 1024
EPS = 1e-6
NEG_INF = -1e30

ROW_TILE = 1024
PROJ_SUBTILES = 4
Q_TILE = 512
Q_TILES_PER_STEP = 2
KEY_STEP = 256
LOOKAHEAD = 2
V7X_VMEM_BYTES = 64 * 1024 * 1024
VMEM_LIMIT = V7X_VMEM_BYTES - 8 * 1024 * 1024
F32_SUBLANES = 8
LOG2_E = math.log2(math.e)
SAFE_EXPONENT = 60.0
SHIFT_TERMS = 3
SHIFT_ROWS = 16

F32 = jnp.float32
BF16 = jnp.bfloat16


def _nt_dot(a, b):
    return lax.dot_general(a, b, (((1,), (1,)), ((), ())), preferred_element_type=F32)


def _dot(a, b):
    return jnp.dot(a, b, preferred_element_type=F32)


def _silu(g):
    return g * jax.nn.sigmoid(g)


def _head_rms(t, gain):
    ms = jnp.mean(t * t, axis=1, keepdims=True)
    return t * lax.rsqrt(ms + EPS) * gain[None]


def _rope_halves(t, cos, sin):
    half = HEAD_DIM // 2
    x1 = t[:, :half, :]
    x2 = t[:, half:, :]
    c = cos[None]
    s = sin[None]
    return jnp.concatenate([x1 * c - x2 * s, x1 * s + x2 * c], axis=1)


def _zero_from(tile, width):
    bits = pltpu.bitcast(tile, jnp.uint32)
    zero = lax.shift_right_logical(lax.shift_right_logical(bits, jnp.uint32(16)), jnp.uint32(16))
    return jnp.tile(pltpu.bitcast(zero, F32), (1, width // LANES))


def _qkv_lanes(pr, dil, r, j):
    lo = ((pr * dil + r) * 3 + j) * LANES
    return slice(lo, lo + LANES)


def _residue_layout_jobs(src, j, scr, o4, o16):
    t = src.shape[1]

    def fill(pr):
        scr[pr] = src[0, :, _qkv_lanes(pr, 1, 0, j)].astype(F32)
        return None

    def gather(pr, dil, out, r):
        lanes = _qkv_lanes(pr, dil, r, j)
        lo = lanes.start
        out[0, :, lanes] = scr[pr, pl.ds(r, t // dil, stride=dil), :].astype(BF16)
        row = pl.multiple_of(jnp.minimum(pl.program_id(0), 0), 16)
        return out[0, pl.ds(row, 16), lo:lo + LANES]

    jobs = [functools.partial(fill, pr) for pr in range(N_PAIRS)]
    for pr in range(N_PAIRS):
        for dil, out in ((4, o4), (16, o16)):
            jobs += [functools.partial(gather, pr, dil, out, r) for r in range(dil)]
    return jobs


def _proj_kernel(x_ref, g_ref, wtok_ref, wfeat_ref, wrope_ref, gains_ref, cos_ref, sin_ref,
                 ga_ref, gbt_ref, qkv_ref, qbt_ref, kb_ref, vbt_ref):
    for sub in range(PROJ_SUBTILES):
        _proj_subtile(sub, x_ref, g_ref, wtok_ref, wfeat_ref, wrope_ref, gains_ref, cos_ref, sin_ref,
                      ga_ref, gbt_ref, qkv_ref, qbt_ref, kb_ref, vbt_ref)


def _proj_subtile(sub, x_ref, g_ref, wtok_ref, wfeat_ref, wrope_ref, gains_ref, cos_ref, sin_ref,
                  ga_ref, gbt_ref, qkv_ref, qbt_ref, kb_ref, vbt_ref):
    t = x_ref.shape[1] // PROJ_SUBTILES
    row0 = sub * t
    rows = slice(row0, row0 + t)
    scale = HEAD_DIM ** -0.5 * LOG2_E
    x = x_ref[0, rows, :]
    ms = jnp.mean(x * x, axis=-1, keepdims=True)
    h = (x * lax.rsqrt(ms + EPS) * g_ref[...]).astype(BF16)

    def store_pairs(j, y):
        for pr in range(N_PAIRS):
            qkv_ref[0, rows, _qkv_lanes(pr, 1, 0, j)] = y[:, pr * LANES:(pr + 1) * LANES].astype(BF16)

    qa = _nt_dot(wfeat_ref[:D_A, :], h).reshape(N_HEADS_A, HEAD_DIM, t)
    qa = _head_rms(qa, gains_ref[0]) * scale
    store_pairs(0, qa.reshape(D_A, t).T)
    ka = _nt_dot(wfeat_ref[D_A:2 * D_A, :], h).reshape(N_HEADS_A, HEAD_DIM, t)
    ka = _head_rms(ka, gains_ref[1])
    store_pairs(1, ka.reshape(D_A, t).T)
    store_pairs(2, _dot(h, wtok_ref[:, :D_A]))
    ga_ref[0, rows, :] = _dot(h, wtok_ref[:, D_A:]).astype(BF16)

    rope = _nt_dot(wrope_ref[...], h)
    o = 4 * D_A + D_B + D_KV_B
    vg = _nt_dot(wfeat_ref[o:, :], h)
    gbt_ref[0, :, :, rows] = vg[D_KV_B:].reshape(N_HEADS_B, HEAD_DIM, t).astype(BF16)
    cos = cos_ref[:, rows]
    sin = sin_ref[:, rows]
    qb = _head_rms(rope[:D_B].reshape(N_HEADS_B, HEAD_DIM, t), gains_ref[2])
    qbt_ref[0, :, :, rows] = (_rope_halves(qb, cos, sin) * scale).astype(BF16)
    kb = _head_rms(rope[D_B:].reshape(N_KV_B, HEAD_DIM, t), gains_ref[3])
    kb = _rope_halves(kb, cos, sin)
    ext = (lax.broadcasted_iota(jnp.int32, (HEAD_DIM, t), 0) < SHIFT_TERMS).astype(F32)
    for kv in range(N_KV_B):
        kb_ref[0, kv, rows, :] = jnp.concatenate([kb[kv], ext], axis=0).T.astype(BF16)
    vbt_ref[0, :, 0, :, rows] = vg[:D_KV_B].reshape(N_KV_B, HEAD_DIM, t).astype(BF16)


def _proj_call(x, ln_g, wtok, wfeat, wrope, gains, cos_t, sin_t):
    B, S, D = x.shape
    tm = ROW_TILE
    nt = S // tm
    full = lambda shape: pl.BlockSpec(shape, lambda b, i: (0,) * len(shape))
    tok_spec = pl.BlockSpec((1, tm, D_A), lambda b, i: (b, i, 0))
    tok_shape = jax.ShapeDtypeStruct((B, S, D_A), BF16)
    out_shape = (
        tok_shape,
        jax.ShapeDtypeStruct((B, N_HEADS_B, HEAD_DIM, S), BF16),
        jax.ShapeDtypeStruct((B, S, 3 * D_A), BF16),
        jax.ShapeDtypeStruct((B, N_HEADS_B, HEAD_DIM, S), BF16),
        jax.ShapeDtypeStruct((B, N_KV_B, S, 2 * HEAD_DIM), BF16),
        jax.ShapeDtypeStruct((B, N_KV_B, nt, HEAD_DIM, tm), BF16),
    )
    out_specs = (
        tok_spec, pl.BlockSpec((1, N_HEADS_B, HEAD_DIM, tm), lambda b, i: (b, 0, 0, i)),
        pl.BlockSpec((1, tm, 3 * D_A), lambda b, i: (b, i, 0)),
        pl.BlockSpec((1, N_HEADS_B, HEAD_DIM, tm), lambda b, i: (b, 0, 0, i)),
        pl.BlockSpec((1, N_KV_B, tm, 2 * HEAD_DIM), lambda b, i: (b, 0, i, 0)),
        pl.BlockSpec((1, N_KV_B, 1, HEAD_DIM, tm), lambda b, i: (b, 0, i, 0, 0)),
    )
    in_specs = [
        pl.BlockSpec((1, tm, D), lambda b, i: (b, i, 0)),
        full((1, D)),
        full(wtok.shape),
        full(wfeat.shape),
        full(wrope.shape),
        full(gains.shape),
        pl.BlockSpec((HEAD_DIM // 2, tm), lambda b, i: (0, i)),
        pl.BlockSpec((HEAD_DIM // 2, tm), lambda b, i: (0, i)),
    ]
    return pl.pallas_call(
        _proj_kernel,
        grid=(B, nt),
        in_specs=in_specs,
        out_specs=out_specs,
        out_shape=out_shape,
        compiler_params=pltpu.CompilerParams(
            dimension_semantics=("arbitrary", "arbitrary"), vmem_limit_bytes=VMEM_LIMIT),
        name="proj",
    )(x, ln_g, wtok, wfeat, wrope, gains, cos_t, sin_t)


def _gqa_kernel(*refs, mode):
    if mode == "max":
        qt_ref, k_ref, o_ref = refs
    else:
        n_in = 5 if mode == "attn_shift" else 4
        qt_ref, k_ref, vt_ref, gt_ref = refs[:4]
        c_ref = refs[4] if mode == "attn_shift" else None
        qkv_ref, o_ref, qkv4_ref, qkv16_ref = refs[n_in:n_in + 4]
        scr_refs = refs[n_in + 4:]
        side_jobs = []
        for j in range(3):
            side_jobs += _residue_layout_jobs(qkv_ref, j, scr_refs[j], qkv4_ref, qkv16_ref)
    tq = Q_TILE
    n_steps = k_ref.shape[2] // KEY_STEP
    sub = F32_SUBLANES
    heads = [(g, slice(j * tq, (j + 1) * tq)) for j in range(qt_ref.shape[3] // tq) for g in range(GQA_GROUP)]
    qext = {}
    for g, cols in heads:
        q = qt_ref[0, g, :, cols]
        if mode == "attn_shift":
            pad = jnp.zeros((HEAD_DIM - SHIFT_ROWS, tq), BF16)
            qext[g, cols.start] = jnp.concatenate([q, c_ref[0, g, :, cols], pad], axis=0)
        else:
            qext[g, cols.start] = jnp.concatenate([q, jnp.zeros_like(q)], axis=0)
    units = [(g, cols, t) for g, cols in heads for t in range(n_steps)]

    def scores(u):
        g, cols, t = u
        return _dot(k_ref[0, 0, t * KEY_STEP:(t + 1) * KEY_STEP, :], qext[g, cols.start])

    pending = [scores(u) for u in units[:LOOKAHEAD]]
    for i, (g, cols, t) in enumerate(units):
        if i + LOOKAHEAD < len(units):
            pending.append(scores(units[i + LOOKAHEAD]))
        s = pending[i]
        pending[i] = None
        if mode == "max":
            part = s.reshape(KEY_STEP // sub, sub, tq).max(axis=0)
            mx = part if t == 0 else jnp.maximum(mx, part)
            if t == n_steps - 1:
                o_ref[0, g, :, cols] = jnp.broadcast_to(mx.max(axis=0, keepdims=True), (sub, tq))
            continue
        if t == 0:
            acc = jnp.zeros((HEAD_DIM, tq), F32)
            den = jnp.zeros((sub, tq), F32)
        p = jnp.exp2(s)
        den = den + p.reshape(KEY_STEP // sub, sub, tq).sum(axis=0)
        if side_jobs:
            back = side_jobs.pop(0)()
            if back is not None:
                den = den + _zero_from(back, tq)
        tk = vt_ref.shape[4]
        lo = (t * KEY_STEP) % tk
        acc = acc + _dot(vt_ref[0, 0, (t * KEY_STEP) // tk, :, lo:lo + KEY_STEP], p.astype(BF16))
        if t == n_steps - 1:
            y = acc / den.sum(axis=0, keepdims=True)
            o_ref[0, g, :, cols] = (y * _silu(gt_ref[0, g, :, cols].astype(F32))).astype(BF16)
    if mode != "max":
        for job in side_jobs:
            job()


def _gqa_call(qbt, kb, vbt, gbt, window_qkv, shift=None, mode="attn"):
    B, H, _, S = qbt.shape
    tq = Q_TILE * min(Q_TILES_PER_STEP, S // Q_TILE)
    n_q = S // tq
    nt, rows, tk = vbt.shape[2], vbt.shape[3], vbt.shape[4]
    head_blk = lambda r: pl.BlockSpec((1, GQA_GROUP, r, tq), lambda b, kv, i: (b, kv, 0, i))
    in_specs = [head_blk(HEAD_DIM), pl.BlockSpec((1, 1, S, 2 * HEAD_DIM), lambda b, kv, i: (b, kv, 0, 0))]
    args = [qbt, kb]
    grid = (B, N_KV_B, n_q)
    params = pltpu.CompilerParams(
        dimension_semantics=("arbitrary", "arbitrary", "arbitrary"), vmem_limit_bytes=VMEM_LIMIT)
    if mode == "max":
        return pl.pallas_call(
            functools.partial(_gqa_kernel, mode=mode), grid=grid, in_specs=in_specs, out_specs=head_blk(8),
            out_shape=jax.ShapeDtypeStruct((B, H, 8, S), F32), compiler_params=params, name="gqa_" + mode,
        )(*args)
    in_specs += [pl.BlockSpec((1, 1, nt, rows, tk), lambda b, kv, i: (b, kv, 0, 0, 0)), head_blk(HEAD_DIM)]
    args += [vbt, gbt]
    if mode == "attn_shift":
        in_specs.append(head_blk(SHIFT_ROWS))
        args.append(shift)
    t = S // (N_KV_B * n_q)
    tile = lambda b, kv, i: (b, kv * n_q + i, 0)
    in_specs.append(pl.BlockSpec((1, t, 3 * D_A), tile))
    args.append(window_qkv)
    out_specs, out_shape = [head_blk(HEAD_DIM)], [jax.ShapeDtypeStruct((B, H, HEAD_DIM, S), BF16)]
    for dil in DILATIONS[1:]:
        out_specs.append(pl.BlockSpec((1, t // dil, dil * 3 * D_A), tile))
        out_shape.append(jax.ShapeDtypeStruct((B, S // dil, dil * 3 * D_A), BF16))
    outs = pl.pallas_call(
        functools.partial(_gqa_kernel, mode=mode), grid=grid, in_specs=in_specs, out_specs=out_specs,
        out_shape=out_shape, scratch_shapes=[pltpu.VMEM((N_PAIRS, t, LANES), F32)] * 3,
        compiler_params=params, name="gqa_" + mode,
    )(*args)
    return outs[0], outs[1:]


def _shift_rows(m):
    rest = m[:, :, :1, :]
    terms = []
    for _ in range(SHIFT_TERMS):
        t = rest.astype(BF16)
        terms.append(-t)
        rest = rest - t.astype(F32)
    pad = jnp.zeros(m.shape[:2] + (SHIFT_ROWS - SHIFT_TERMS, m.shape[3]), BF16)
    return jnp.concatenate(terms + [pad], axis=2)


def _band_bucket_index():
    nb = N_BUCKETS // 2
    max_exact = nb // 2
    rel = np.arange(BAND_SPAN)[None, :] - BAND_HALF - np.arange(BAND_BLOCK)[:, None]
    out = []
    for dilation in DILATIONS:
        dist = rel * dilation
        ret = np.where(dist > 0, nb, 0)
        n = np.abs(dist)
        nf = np.maximum(n, max_exact).astype(np.float32)
        large = max_exact + (np.log(nf / np.float32(max_exact)) / np.float32(math.log(MAX_DISTANCE / max_exact))
                             * np.float32(nb - max_exact)).astype(np.int32)
        large = np.minimum(large, nb - 1)
        bucket = ret + np.where(n < max_exact, n, large)
        out.append(np.where(np.abs(rel) <= BAND_HALF, bucket, -1))
    return np.stack(out).astype(np.int32)


def _bias_kernel(table_ref, idx_ref, o_ref):
    chunk = 16
    index = _band_bucket_index()
    for pat in range(len(DILATIONS)):
        present = [int(b) for b in np.unique(index[pat]) if b >= 0]
        for lo in range(0, BAND_BLOCK, chunk):
            idx = idx_ref[pat, lo:lo + chunk, :]
            acc = [jnp.full(idx.shape, NEG_INF, F32) for _ in range(N_HEADS_A)]
            for b in present:
                hit = idx == b
                for h in range(N_HEADS_A):
                    acc[h] = jnp.where(hit, table_ref[b, h] * LOG2_E, acc[h])
            for h in range(N_HEADS_A):
                o_ref[pat, h, lo:lo + chunk, :] = acc[h]


def _bias_call(rel_bias):
    idx = jnp.asarray(_band_bucket_index())
    n_pat = len(DILATIONS)
    return pl.pallas_call(
        _bias_kernel,
        in_specs=[pl.BlockSpec(memory_space=pltpu.SMEM), pl.BlockSpec(memory_space=pltpu.VMEM)],
        out_specs=pl.BlockSpec(memory_space=pltpu.VMEM),
        out_shape=jax.ShapeDtypeStruct((n_pat, N_HEADS_A, BAND_BLOCK, BAND_SPAN), F32),
        name="bias_bands",
    )(rel_bias, idx)


def _band_kernel(*refs, mode):
    n_in = 3 * len(DILATIONS)
    ins, bias_ref = refs[:n_in], refs[n_in]
    g_ref = refs[n_in + 1] if mode != "max" else None
    c_ref = refs[n_in + 2] if mode == "attn_shift" else None
    o_ref, nums_ref, dens_ref = refs[-3:]
    n = pl.program_id(2)
    last = pl.num_programs(2) - 1
    col = lax.broadcasted_iota(jnp.int32, (1, BAND_SPAN), 1)
    edge_lo = jnp.where((n == 0) & (col < BAND_HALF), NEG_INF, 0.0).astype(F32)
    edge_hi = jnp.where((n == last) & (col >= BAND_BLOCK + BAND_HALF), NEG_INF, 0.0).astype(F32)
    lane = lax.broadcasted_iota(jnp.int32, (BAND_BLOCK, LANES), 1)
    first_half = lane < HEAD_DIM
    ones = jnp.ones((BAND_SPAN, LANES), BF16)

    def band(main, prev, nxt, blk, n_blk, ls):
        lo = blk * BAND_BLOCK - BAND_HALF
        parts = []
        if blk == 0:
            parts.append(prev[0, :, ls])
        start = max(lo, 0)
        stop = min(lo + BAND_SPAN, n_blk * BAND_BLOCK)
        parts.append(main[0, start:stop, ls])
        if blk == n_blk - 1:
            parts.append(nxt[0, :, ls])
        return parts[0] if len(parts) == 1 else jnp.concatenate(parts, axis=0)

    units = []
    for pp in range(nums_ref.shape[0]):
        for pat in reversed(range(len(DILATIONS))):
            dil = DILATIONS[pat]
            main_ref, prev_ref, next_ref = ins[3 * pat:3 * pat + 3]
            n_blk = SUPER_BLOCK // (dil * BAND_BLOCK)
            for r in range(dil):
                for blk in range(n_blk):
                    units.append((pat, dil, r, blk, n_blk, main_ref, prev_ref, next_ref, pp))

    def stage_scores(u):
        pat, dil, r, blk, n_blk, main_ref, prev_ref, next_ref, pp = u
        q = main_ref[0, blk * BAND_BLOCK:(blk + 1) * BAND_BLOCK, _qkv_lanes(pp, dil, r, 0)]
        kband = band(main_ref, prev_ref, next_ref, blk, n_blk, _qkv_lanes(pp, dil, r, 1))
        zero = jnp.zeros_like(q)
        return [_nt_dot(jnp.where(first_half, q, zero), kband), _nt_dot(jnp.where(first_half, zero, q), kband)]

    def stage_values(u, s_pair):
        pat, dil, r, blk, n_blk, main_ref, prev_ref, next_ref, pp = u
        out_ls = slice(pp * LANES, (pp + 1) * LANES)
        num_ref, den_ref = nums_ref.at[pp], dens_ref.at[pp]
        if dil == 1:
            rows = slice(blk * BAND_BLOCK, (blk + 1) * BAND_BLOCK)
        else:
            rows = pl.ds(dil * blk * BAND_BLOCK + r, BAND_BLOCK, stride=dil)
        def biased(half):
            s = s_pair[half] + bias_ref[pat, 2 * pp + half]
            if blk == 0:
                s = s + edge_lo
            if blk == n_blk - 1:
                s = s + edge_hi
            return s

        if mode == "max":
            mx = jnp.where(first_half, biased(0).max(axis=-1, keepdims=True), biased(1).max(axis=-1, keepdims=True))
            if dil == DILATIONS[-1]:
                num_ref[rows, :] = mx
            elif dil > 1:
                num_ref[rows, :] = jnp.maximum(num_ref[rows, :], mx)
            else:
                o_ref[0, pp, rows, :] = jnp.maximum(num_ref[rows, :], mx)
            return
        vband = band(main_ref, prev_ref, next_ref, blk, n_blk, _qkv_lanes(pp, dil, r, 2))
        vext = jnp.concatenate([vband, ones], axis=1)
        outs = []
        for half in range(2):
            s = biased(half)
            if mode == "attn_shift":
                s = s - c_ref[0, pp, rows, :][:, half * HEAD_DIM:half * HEAD_DIM + 1]
            outs.append(_dot(jnp.exp2(s).astype(BF16), vext))
        num = jnp.where(first_half, outs[0][:, :LANES], outs[1][:, :LANES])
        den = jnp.where(first_half, outs[0][:, LANES:], outs[1][:, LANES:])
        if dil == DILATIONS[-1]:
            num_ref[rows, :] = num
            den_ref[rows, :] = den
        elif dil > 1:
            num_ref[rows, :] = num_ref[rows, :] + num
            den_ref[rows, :] = den_ref[rows, :] + den
        else:
            y = (num_ref[rows, :] + num) / (den_ref[rows, :] + den)
            o_ref[0, rows, out_ls] = (y * _silu(g_ref[0, rows, out_ls].astype(F32))).astype(BF16)

    pending = [stage_scores(u) for u in units[:LOOKAHEAD]]
    for t, u in enumerate(units):
        if t + LOOKAHEAD < len(units):
            pending.append(stage_scores(units[t + LOOKAHEAD]))
        stage_values(u, pending[t])
        pending[t] = None


def _band_call(layouts, bias, gate, shift=None, mode="attn"):
    B, S, _ = layouts[0].shape
    n_sb = S // SUPER_BLOCK
    pp = PAIRS_PER_STEP
    in_specs, args = [], []
    for qkv, dil in zip(layouts, DILATIONS):
        rows = SUPER_BLOCK // dil
        width = pp * dil * 3 * LANES
        n_halo = (S // dil) // BAND_HALF
        per_sb = rows // BAND_HALF
        main = pl.BlockSpec((1, rows, width), lambda b, pr, n: (b, n, pr))
        prev = pl.BlockSpec((1, BAND_HALF, width),
                            lambda b, pr, n, per_sb=per_sb: (b, jnp.maximum(n * per_sb - 1, 0), pr))
        nxt = pl.BlockSpec((1, BAND_HALF, width),
                           lambda b, pr, n, per_sb=per_sb, n_halo=n_halo: (b, jnp.minimum((n + 1) * per_sb, n_halo - 1), pr))
        in_specs += [main, prev, nxt]
        args += [qkv, qkv, qkv]
    in_specs.append(pl.BlockSpec((len(DILATIONS), 2 * pp, BAND_BLOCK, BAND_SPAN), lambda b, pr, n: (0, pr, 0, 0)))
    args.append(bias)
    tok_blk = pl.BlockSpec((1, SUPER_BLOCK, pp * LANES), lambda b, pr, n: (b, n, pr))
    shift_blk = pl.BlockSpec((1, pp, SUPER_BLOCK, LANES), lambda b, pr, n: (b, pr, n, 0))
    if mode != "max":
        in_specs.append(tok_blk)
        args.append(gate)
    if mode == "attn_shift":
        in_specs.append(shift_blk)
        args.append(shift)
    if mode == "max":
        out_specs, out_shape = shift_blk, jax.ShapeDtypeStruct((B, N_PAIRS, S, LANES), F32)
    else:
        out_specs, out_shape = tok_blk, jax.ShapeDtypeStruct((B, S, D_A), BF16)
    return pl.pallas_call(
        functools.partial(_band_kernel, mode=mode),
        grid=(B, N_PAIRS // pp, n_sb),
        in_specs=in_specs,
        out_specs=out_specs,
        out_shape=out_shape,
        scratch_shapes=[pltpu.VMEM((pp, SUPER_BLOCK, LANES), F32)] * 2,
        compiler_params=pltpu.CompilerParams(
            dimension_semantics=("arbitrary", "arbitrary", "arbitrary"), vmem_limit_bytes=VMEM_LIMIT),
        name="band_" + mode,
    )(*args)


def _out_kernel(x_ref, p_ref, ya_ref, ybt_ref, wo32_ref, wple32_ref, wgate32_ref, o_ref, wo_ref, wple_ref, wgate_ref):
    @pl.when((pl.program_id(0) == 0) & (pl.program_id(1) == 0))
    def _():
        wo_ref[...] = wo32_ref[...].astype(BF16)
        wple_ref[...] = wple32_ref[...].astype(BF16)
        wgate_ref[...] = wgate32_ref[...].astype(BF16)

    tm = x_ref.shape[1]
    yb = ybt_ref[0].reshape(D_B, tm).astype(F32).T.astype(BF16)
    x1 = x_ref[0] + _dot(ya_ref[0], wo_ref[:D_A, :]) + _dot(yb, wo_ref[D_A:, :])
    gate = jax.nn.sigmoid(_dot(x1.astype(BF16), wgate_ref[...]))
    ple = _dot(p_ref[0].astype(BF16), wple_ref[...])
    o_ref[0] = x1 + ple * gate


def _out_call(x, p, ya, ybt, wo, wple, wgate):
    B, S, D = x.shape
    tm = ROW_TILE
    full = lambda shape: pl.BlockSpec(shape, lambda b, i: (0,) * len(shape), pipeline_mode=pl.Buffered(1))
    tok = lambda c: pl.BlockSpec((1, tm, c), lambda b, i: (b, i, 0))
    return pl.pallas_call(
        _out_kernel,
        grid=(B, S // tm),
        in_specs=[tok(D), tok(p.shape[-1]), tok(D_A),
                  pl.BlockSpec((1, N_HEADS_B, HEAD_DIM, tm), lambda b, i: (b, 0, 0, i)),
                  full(wo.shape), full(wple.shape), full(wgate.shape)],
        out_specs=tok(D),
        out_shape=jax.ShapeDtypeStruct((B, S, D), F32),
        scratch_shapes=[pltpu.VMEM(w.shape, BF16) for w in (wo, wple, wgate)],
        compiler_params=pltpu.CompilerParams(
            dimension_semantics=("arbitrary", "arbitrary"), vmem_limit_bytes=VMEM_LIMIT),
        name="out_proj",
    )(x, p, ya, ybt, wo, wple, wgate)


def _rope_tables(S):
    n = HEAD_DIM // 4
    inv = ROPE_THETA ** (-jnp.arange(n, dtype=F32) / n)
    rows = S // GRID_W
    ang_row = inv[:, None] * jnp.arange(rows, dtype=jnp.int32).astype(F32)[None, :]
    ang_col = inv[:, None] * jnp.arange(GRID_W, dtype=jnp.int32).astype(F32)[None, :]
    expand = lambda fn: jnp.concatenate(
        [jnp.repeat(fn(ang_row), GRID_W, axis=1), jnp.tile(fn(ang_col), (1, rows))], axis=0)
    return expand(jnp.cos), expand(jnp.sin)


def _pairs_apart(t):
    lead, tail = t.shape[:-2], t.shape[-1]
    heads = t.shape[-2] // HEAD_DIM
    t = t.reshape(*lead, heads, HEAD_DIM // 2, 2, tail)
    return jnp.swapaxes(t, -3, -2).reshape(*lead, heads * HEAD_DIM, tail)


def kernel(x, p, ln_g, w_in, qn_a, kn_a, qn_b, kn_b, w_out, w_ple, w_pgate, rel_bias):
    B, S, D = x.shape
    depth = w_in.shape[0]
    cos_t, sin_t = _rope_tables(S)
    bias = _bias_call(rel_bias)
    for i in range(depth):
        wfeat = w_in[i].T.astype(BF16)
        wrope = _pairs_apart(wfeat[4 * D_A:4 * D_A + D_B + D_KV_B])
        wtok = w_in[i][:, 2 * D_A:4 * D_A].astype(BF16)
        gains_b = _pairs_apart(jnp.stack([qn_b[i], kn_b[i]], axis=-1))
        gains = jnp.concatenate([jnp.stack([qn_a[i], kn_a[i]]), gains_b.T]).astype(F32)
        gains = jnp.broadcast_to(gains[:, :, None], (4, HEAD_DIM, ROW_TILE // PROJ_SUBTILES))

        outs = _proj_call(
            x, ln_g[i][None, :], wtok, wfeat, wrope, gains, cos_t, sin_t)
        ga, gbt, qkv1, qbt, kb, vbt = outs

        gmax = jnp.max(jnp.abs(jnp.stack([qn_a[i], kn_a[i], qn_b[i], kn_b[i]]).astype(F32)), axis=1)
        root = HEAD_DIM ** 0.5
        bound_a = root * gmax[0] * gmax[1] + jnp.max(jnp.abs(rel_bias.astype(F32)))
        bound_b = root * gmax[2] * gmax[3]
        safe = jnp.maximum(bound_a, bound_b) <= SAFE_EXPONENT

        def unshifted(ops):
            qbt, kb, vbt, gbt, qkv1, bias, ga = ops
            ybt, (qkv4, qkv16) = _gqa_call(qbt, kb, vbt, gbt, qkv1)
            return ybt, _band_call((qkv1, qkv4, qkv16), bias, ga)

        def shifted(ops):
            qbt, kb, vbt, gbt, qkv1, bias, ga = ops
            shift_b = _shift_rows(_gqa_call(qbt, kb, vbt, gbt, qkv1, mode="max"))
            ybt, (qkv4, qkv16) = _gqa_call(qbt, kb, vbt, gbt, qkv1, shift_b, mode="attn_shift")
            layouts = (qkv1, qkv4, qkv16)
            shift_a = _band_call(layouts, bias, ga, mode="max")
            return ybt, _band_call(layouts, bias, ga, shift_a, mode="attn_shift")

        ybt, ya = lax.cond(safe, unshifted, shifted, (qbt, kb, vbt, gbt, qkv1, bias, ga))
        x = _out_call(x, p[i], ya, ybt, w_out[i], w_ple[i], w_pgate[i])
    return x
```

```python
import functools
import math

import numpy as np
import jax
import jax.numpy as jnp
from jax import lax
from jax.experimental import pallas as pl
from jax.experimental.pallas import tpu as pltpu

HEAD_DIM = 64
N_HEADS_A = 8
N_HEADS_B = 8
N_KV_B = 2
GQA_GROUP = N_HEADS_B // N_KV_B
D_A = N_HEADS_A * HEAD_DIM
D_B = N_HEADS_B * HEAD_DIM
D_KV_B = N_KV_B * HEAD_DIM
DILATIONS = (1, 4, 16)
BAND_BLOCK = 128
BAND_HALF = 64
BAND_SPAN = BAND_BLOCK + 2 * BAND_HALF
SUPER_BLOCK = BAND_BLOCK * max(DILATIONS)
LANES = 128
N_PAIRS = D_A // LANES
PAIRS_PER_STEP = 2
GRID_W = 64
ROPE_THETA = 10000.0
N_BUCKETS = 32
MAX_DISTANCE = 1024
EPS = 1e-6
NEG_INF = -1e30

ROW_TILE = 1024
PROJ_SUBTILES = 4
Q_TILE = 512
Q_TILES_PER_STEP = 2
KEY_STEP = 256
LOOKAHEAD = 2
V7X_VMEM_BYTES = 64 * 1024 * 1024
VMEM_LIMIT = V7X_VMEM_BYTES - 8 * 1024 * 1024
F32_SUBLANES = 8
LOG2_E = math.log2(math.e)
SAFE_EXPONENT = 60.0
SHIFT_TERMS = 3
SHIFT_ROWS = 16

F32 = jnp.float32
BF16 = jnp.bfloat16


def _nt_dot(a, b):
    return lax.dot_general(a, b, (((1,), (1,)), ((), ())), preferred_element_type=F32)


def _dot(a, b):
    return jnp.dot(a, b, preferred_element_type=F32)


def _silu(g):
    return g * jax.nn.sigmoid(g)


def _head_rms(t, gain):
    ms = jnp.mean(t * t, axis=1, keepdims=True)
    return t * lax.rsqrt(ms + EPS) * gain[None]


def _rope_halves(t, cos, sin):
    half = HEAD_DIM // 2
    x1 = t[:, :half, :]
    x2 = t[:, half:, :]
    c = cos[None]
    s = sin[None]
    return jnp.concatenate([x1 * c - x2 * s, x1 * s + x2 * c], axis=1)


def _zero_from(tile, width):
    bits = pltpu.bitcast(tile, jnp.uint32)
    zero = lax.shift_right_logical(lax.shift_right_logical(bits, jnp.uint32(16)), jnp.uint32(16))
    return jnp.tile(pltpu.bitcast(zero, F32), (1, width // LANES))


def _qkv_lanes(pr, dil, r, j):
    lo = ((pr * dil + r) * 3 + j) * LANES
    return slice(lo, lo + LANES)


def _residue_layout_jobs(src, j, scr, o4, o16):
    t = src.shape[1]

    def fill(pr):
        scr[pr] = src[0, :, _qkv_lanes(pr, 1, 0, j)].astype(F32)
        return None

    def gather(pr, dil, out, r):
        lanes = _qkv_lanes(pr, dil, r, j)
        lo = lanes.start
        out[0, :, lanes] = scr[pr, pl.ds(r, t // dil, stride=dil), :].astype(BF16)
        row = pl.multiple_of(jnp.minimum(pl.program_id(0), 0), 16)
        return out[0, pl.ds(row, 16), lo:lo + LANES]

    jobs = [functools.partial(fill, pr) for pr in range(N_PAIRS)]
    for pr in range(N_PAIRS):
        for dil, out in ((4, o4), (16, o16)):
            jobs += [functools.partial(gather, pr, dil, out, r) for r in range(dil)]
    return jobs


def _proj_kernel(x_ref, g_ref, wtok_ref, wfeat_ref, wrope_ref, gains_ref, cos_ref, sin_ref,
                 ga_ref, gbt_ref, qkv_ref, qbt_ref, kb_ref, vbt_ref):
    for sub in range(PROJ_SUBTILES):
        _proj_subtile(sub, x_ref, g_ref, wtok_ref, wfeat_ref, wrope_ref, gains_ref, cos_ref, sin_ref,
                      ga_ref, gbt_ref, qkv_ref, qbt_ref, kb_ref, vbt_ref)


def _proj_subtile(sub, x_ref, g_ref, wtok_ref, wfeat_ref, wrope_ref, gains_ref, cos_ref, sin_ref,
                  ga_ref, gbt_ref, qkv_ref, qbt_ref, kb_ref, vbt_ref):
    t = x_ref.shape[1] // PROJ_SUBTILES
    row0 = sub * t
    rows = slice(row0, row0 + t)
    scale = HEAD_DIM ** -0.5 * LOG2_E
    x = x_ref[0, rows, :]
    ms = jnp.mean(x * x, axis=-1, keepdims=True)
    h = (x * lax.rsqrt(ms + EPS) * g_ref[...]).astype(BF16)

    def store_pairs(j, y):
        for pr in range(N_PAIRS):
            qkv_ref[0, rows, _qkv_lanes(pr, 1, 0, j)] = y[:, pr * LANES:(pr + 1) * LANES].astype(BF16)

    qa = _nt_dot(wfeat_ref[:D_A, :], h).reshape(N_HEADS_A, HEAD_DIM, t)
    qa = _head_rms(qa, gains_ref[0]) * scale
    store_pairs(0, qa.reshape(D_A, t).T)
    ka = _nt_dot(wfeat_ref[D_A:2 * D_A, :], h).reshape(N_HEADS_A, HEAD_DIM, t)
    ka = _head_rms(ka, gains_ref[1])
    store_pairs(1, ka.reshape(D_A, t).T)
    store_pairs(2, _dot(h, wtok_ref[:, :D_A]))
    ga_ref[0, rows, :] = _dot(h, wtok_ref[:, D_A:]).astype(BF16)

    rope = _nt_dot(wrope_ref[...], h)
    o = 4 * D_A + D_B + D_KV_B
    vg = _nt_dot(wfeat_ref[o:, :], h)
    gbt_ref[0, :, :, rows] = vg[D_KV_B:].reshape(N_HEADS_B, HEAD_DIM, t).astype(BF16)
    cos = cos_ref[:, rows]
    sin = sin_ref[:, rows]
    qb = _head_rms(rope[:D_B].reshape(N_HEADS_B, HEAD_DIM, t), gains_ref[2])
    qbt_ref[0, :, :, rows] = (_rope_halves(qb, cos, sin) * scale).astype(BF16)
    kb = _head_rms(rope[D_B:].reshape(N_KV_B, HEAD_DIM, t), gains_ref[3])
    kb = _rope_halves(kb, cos, sin)
    ext = (lax.broadcasted_iota(jnp.int32, (HEAD_DIM, t), 0) < SHIFT_TERMS).astype(F32)
    for kv in range(N_KV_B):
        kb_ref[0, kv, rows, :] = jnp.concatenate([kb[kv], ext], axis=0).T.astype(BF16)
    vbt_ref[0, :, 0, :, rows] = vg[:D_KV_B].reshape(N_KV_B, HEAD_DIM, t).astype(BF16)


def _proj_call(x, ln_g, wtok, wfeat, wrope, gains, cos_t, sin_t):
    B, S, D = x.shape
    tm = ROW_TILE
    nt = S // tm
    full = lambda shape: pl.BlockSpec(shape, lambda b, i: (0,) * len(shape), pipeline_mode=pl.Buffered(1))
    tok_spec = pl.BlockSpec((1, tm, D_A), lambda b, i: (b, i, 0))
    tok_shape = jax.ShapeDtypeStruct((B, S, D_A), BF16)
    out_shape = (
        tok_shape,
        jax.ShapeDtypeStruct((B, N_HEADS_B, HEAD_DIM, S), BF16),
        jax.ShapeDtypeStruct((B, S, 3 * D_A), BF16),
        jax.ShapeDtypeStruct((B, N_HEADS_B, HEAD_DIM, S), BF16),
        jax.ShapeDtypeStruct((B, N_KV_B, S, 2 * HEAD_DIM), BF16),
        jax.ShapeDtypeStruct((B, N_KV_B, nt, HEAD_DIM, tm), BF16),
    )
    out_specs = (
        tok_spec, pl.BlockSpec((1, N_HEADS_B, HEAD_DIM, tm), lambda b, i: (b, 0, 0, i)),
        pl.BlockSpec((1, tm, 3 * D_A), lambda b, i: (b, i, 0)),
        pl.BlockSpec((1, N_HEADS_B, HEAD_DIM, tm), lambda b, i: (b, 0, 0, i)),
        pl.BlockSpec((1, N_KV_B, tm, 2 * HEAD_DIM), lambda b, i: (b, 0, i, 0)),
        pl.BlockSpec((1, N_KV_B, 1, HEAD_DIM, tm), lambda b, i: (b, 0, i, 0, 0)),
    )
    in_specs = [
        pl.BlockSpec((1, tm, D), lambda b, i: (b, i, 0)),
        full((1, D)),
        full(wtok.shape),
        full(wfeat.shape),
        full(wrope.shape),
        full(gains.shape),
        pl.BlockSpec((HEAD_DIM // 2, tm), lambda b, i: (0, i)),
        pl.BlockSpec((HEAD_DIM // 2, tm), lambda b, i: (0, i)),
    ]
    return pl.pallas_call(
        _proj_kernel,
        grid=(B, nt),
        in_specs=in_specs,
        out_specs=out_specs,
        out_shape=out_shape,
        compiler_params=pltpu.CompilerParams(
            dimension_semantics=("arbitrary", "arbitrary"), vmem_limit_bytes=VMEM_LIMIT),
        name="proj",
    )(x, ln_g, wtok, wfeat, wrope, gains, cos_t, sin_t)


def _gqa_kernel(*refs, mode):
    if mode == "max":
        qt_ref, k_ref, o_ref = refs
    else:
        n_in = 5 if mode == "attn_shift" else 4
        qt_ref, k_ref, vt_ref, gt_ref = refs[:4]
        c_ref = refs[4] if mode == "attn_shift" else None
        qkv_ref, o_ref, qkv4_ref, qkv16_ref = refs[n_in:n_in + 4]
        scr_refs = refs[n_in + 4:]
        side_jobs = []
        for j in range(3):
            side_jobs += _residue_layout_jobs(qkv_ref, j, scr_refs[j], qkv4_ref, qkv16_ref)
    tq = Q_TILE
    n_steps = k_ref.shape[2] // KEY_STEP
    sub = F32_SUBLANES
    heads = [(g, slice(j * tq, (j + 1) * tq)) for j in range(qt_ref.shape[3] // tq) for g in range(GQA_GROUP)]
    qext = {}
    for g, cols in heads:
        q = qt_ref[0, g, :, cols]
        if mode == "attn_shift":
            pad = jnp.zeros((HEAD_DIM - SHIFT_ROWS, tq), BF16)
            qext[g, cols.start] = jnp.concatenate([q, c_ref[0, g, :, cols], pad], axis=0)
        else:
            qext[g, cols.start] = jnp.concatenate([q, jnp.zeros_like(q)], axis=0)
    units = [(g, cols, t) for g, cols in heads for t in range(n_steps)]

    def scores(u):
        g, cols, t = u
        return _dot(k_ref[0, 0, t * KEY_STEP:(t + 1) * KEY_STEP, :], qext[g, cols.start])

    pending = [scores(u) for u in units[:LOOKAHEAD]]
    for i, (g, cols, t) in enumerate(units):
        if i + LOOKAHEAD < len(units):
            pending.append(scores(units[i + LOOKAHEAD]))
        s = pending[i]
        pending[i] = None
        if mode == "max":
            part = s.reshape(KEY_STEP // sub, sub, tq).max(axis=0)
            mx = part if t == 0 else jnp.maximum(mx, part)
            if t == n_steps - 1:
                o_ref[0, g, :, cols] = jnp.broadcast_to(mx.max(axis=0, keepdims=True), (sub, tq))
            continue
        if t == 0:
            acc = jnp.zeros((HEAD_DIM, tq), F32)
            den = jnp.zeros((sub, tq), F32)
        p = jnp.exp2(s)
        den = den + p.reshape(KEY_STEP // sub, sub, tq).sum(axis=0)
        if side_jobs:
            back = side_jobs.pop(0)()
            if back is not None:
                den = den + _zero_from(back, tq)
        tk = vt_ref.shape[4]
        lo = (t * KEY_STEP) % tk
        acc = acc + _dot(vt_ref[0, 0, (t * KEY_STEP) // tk, :, lo:lo + KEY_STEP], p.astype(BF16))
        if t == n_steps - 1:
            y = acc / den.sum(axis=0, keepdims=True)
            o_ref[0, g, :, cols] = (y * _silu(gt_ref[0, g, :, cols].astype(F32))).astype(BF16)
    if mode != "max":
        for job in side_jobs:
            job()


def _gqa_call(qbt, kb, vbt, gbt, window_qkv, shift=None, mode="attn"):
    B, H, _, S = qbt.shape
    tq = Q_TILE * min(Q_TILES_PER_STEP, S // Q_TILE)
    n_q = S // tq
    nt, rows, tk = vbt.shape[2], vbt.shape[3], vbt.shape[4]
    head_blk = lambda r: pl.BlockSpec((1, GQA_GROUP, r, tq), lambda b, kv, i: (b, kv, 0, i))
    in_specs = [head_blk(HEAD_DIM), pl.BlockSpec((1, 1, S, 2 * HEAD_DIM), lambda b, kv, i: (b, kv, 0, 0))]
    args = [qbt, kb]
    grid = (B, N_KV_B, n_q)
    params = pltpu.CompilerParams(
        dimension_semantics=("arbitrary", "arbitrary", "arbitrary"), vmem_limit_bytes=VMEM_LIMIT)
    if mode == "max":
        return pl.pallas_call(
            functools.partial(_gqa_kernel, mode=mode), grid=grid, in_specs=in_specs, out_specs=head_blk(8),
            out_shape=jax.ShapeDtypeStruct((B, H, 8, S), F32), compiler_params=params, name="gqa_" + mode,
        )(*args)
    in_specs += [pl.BlockSpec((1, 1, nt, rows, tk), lambda b, kv, i: (b, kv, 0, 0, 0)), head_blk(HEAD_DIM)]
    args += [vbt, gbt]
    if mode == "attn_shift":
        in_specs.append(head_blk(SHIFT_ROWS))
        args.append(shift)
    t = S // (N_KV_B * n_q)
    tile = lambda b, kv, i: (b, kv * n_q + i, 0)
    in_specs.append(pl.BlockSpec((1, t, 3 * D_A), tile))
    args.append(window_qkv)
    out_specs, out_shape = [head_blk(HEAD_DIM)], [jax.ShapeDtypeStruct((B, H, HEAD_DIM, S), BF16)]
    for dil in DILATIONS[1:]:
        out_specs.append(pl.BlockSpec((1, t // dil, dil * 3 * D_A), tile))
        out_shape.append(jax.ShapeDtypeStruct((B, S // dil, dil * 3 * D_A), BF16))
    outs = pl.pallas_call(
        functools.partial(_gqa_kernel, mode=mode), grid=grid, in_specs=in_specs, out_specs=out_specs,
        out_shape=out_shape, scratch_shapes=[pltpu.VMEM((N_PAIRS, t, LANES), F32)] * 3,
        compiler_params=params, name="gqa_" + mode,
    )(*args)
    return outs[0], outs[1:]


def _shift_rows(m):
    rest = m[:, :, :1, :]
    terms = []
    for _ in range(SHIFT_TERMS):
        t = rest.astype(BF16)
        terms.append(-t)
        rest = rest - t.astype(F32)
    pad = jnp.zeros(m.shape[:2] + (SHIFT_ROWS - SHIFT_TERMS, m.shape[3]), BF16)
    return jnp.concatenate(terms + [pad], axis=2)


def _band_bucket_index():
    nb = N_BUCKETS // 2
    max_exact = nb // 2
    rel = np.arange(BAND_SPAN)[None, :] - BAND_HALF - np.arange(BAND_BLOCK)[:, None]
    out = []
    for dilation in DILATIONS:
        dist = rel * dilation
        ret = np.where(dist > 0, nb, 0)
        n = np.abs(dist)
        nf = np.maximum(n, max_exact).astype(np.float32)
        large = max_exact + (np.log(nf / np.float32(max_exact)) / np.float32(math.log(MAX_DISTANCE / max_exact))
                             * np.float32(nb - max_exact)).astype(np.int32)
        large = np.minimum(large, nb - 1)
        bucket = ret + np.where(n < max_exact, n, large)
        out.append(np.where(np.abs(rel) <= BAND_HALF, bucket, -1))
    return np.stack(out).astype(np.int32)


def _bias_kernel(table_ref, idx_ref, o_ref):
    chunk = 16
    index = _band_bucket_index()
    for pat in range(len(DILATIONS)):
        present = [int(b) for b in np.unique(index[pat]) if b >= 0]
        for lo in range(0, BAND_BLOCK, chunk):
            idx = idx_ref[pat, lo:lo + chunk, :]
            acc = [jnp.full(idx.shape, NEG_INF, F32) for _ in range(N_HEADS_A)]
            for b in present:
                hit = idx == b
                for h in range(N_HEADS_A):
                    acc[h] = jnp.where(hit, table_ref[b, h] * LOG2_E, acc[h])
            for h in range(N_HEADS_A):
                o_ref[pat, h, lo:lo + chunk, :] = acc[h]


def _bias_call(rel_bias):
    idx = jnp.asarray(_band_bucket_index())
    n_pat = len(DILATIONS)
    return pl.pallas_call(
        _bias_kernel,
        in_specs=[pl.BlockSpec(memory_space=pltpu.SMEM), pl.BlockSpec(memory_space=pltpu.VMEM)],
        out_specs=pl.BlockSpec(memory_space=pltpu.VMEM),
        out_shape=jax.ShapeDtypeStruct((n_pat, N_HEADS_A, BAND_BLOCK, BAND_SPAN), F32),
        name="bias_bands",
    )(rel_bias, idx)


def _band_kernel(*refs, mode):
    n_in = 3 * len(DILATIONS)
    ins, bias_ref = refs[:n_in], refs[n_in]
    g_ref = refs[n_in + 1] if mode != "max" else None
    c_ref = refs[n_in + 2] if mode == "attn_shift" else None
    o_ref, nums_ref, dens_ref = refs[-3:]
    n = pl.program_id(2)
    last = pl.num_programs(2) - 1
    col = lax.broadcasted_iota(jnp.int32, (1, BAND_SPAN), 1)
    edge_lo = jnp.where((n == 0) & (col < BAND_HALF), NEG_INF, 0.0).astype(F32)
    edge_hi = jnp.where((n == last) & (col >= BAND_BLOCK + BAND_HALF), NEG_INF, 0.0).astype(F32)
    lane = lax.broadcasted_iota(jnp.int32, (BAND_BLOCK, LANES), 1)
    first_half = lane < HEAD_DIM
    ones = jnp.ones((BAND_SPAN, LANES), BF16)

    def band(main, prev, nxt, blk, n_blk, ls):
        lo = blk * BAND_BLOCK - BAND_HALF
        parts = []
        if blk == 0:
            parts.append(prev[0, :, ls])
        start = max(lo, 0)
        stop = min(lo + BAND_SPAN, n_blk * BAND_BLOCK)
        parts.append(main[0, start:stop, ls])
        if blk == n_blk - 1:
            parts.append(nxt[0, :, ls])
        return parts[0] if len(parts) == 1 else jnp.concatenate(parts, axis=0)

    units = []
    for pp in range(nums_ref.shape[0]):
        for pat in reversed(range(len(DILATIONS))):
            dil = DILATIONS[pat]
            main_ref, prev_ref, next_ref = ins[3 * pat:3 * pat + 3]
            n_blk = SUPER_BLOCK // (dil * BAND_BLOCK)
            for r in range(dil):
                for blk in range(n_blk):
                    units.append((pat, dil, r, blk, n_blk, main_ref, prev_ref, next_ref, pp))

    def stage_scores(u):
        pat, dil, r, blk, n_blk, main_ref, prev_ref, next_ref, pp = u
        q = main_ref[0, blk * BAND_BLOCK:(blk + 1) * BAND_BLOCK, _qkv_lanes(pp, dil, r, 0)]
        kband = band(main_ref, prev_ref, next_ref, blk, n_blk, _qkv_lanes(pp, dil, r, 1))
        zero = jnp.zeros_like(q)
        return [_nt_dot(jnp.where(first_half, q, zero), kband), _nt_dot(jnp.where(first_half, zero, q), kband)]

    def stage_values(u, s_pair):
        pat, dil, r, blk, n_blk, main_ref, prev_ref, next_ref, pp = u
        out_ls = slice(pp * LANES, (pp + 1) * LANES)
        num_ref, den_ref = nums_ref.at[pp], dens_ref.at[pp]
        if dil == 1:
            rows = slice(blk * BAND_BLOCK, (blk + 1) * BAND_BLOCK)
        else:
            rows = pl.ds(dil * blk * BAND_BLOCK + r, BAND_BLOCK, stride=dil)
        def biased(half):
            s = s_pair[half] + bias_ref[pat, 2 * pp + half]
            if blk == 0:
                s = s + edge_lo
            if blk == n_blk - 1:
                s = s + edge_hi
            return s

        if mode == "max":
            mx = jnp.where(first_half, biased(0).max(axis=-1, keepdims=True), biased(1).max(axis=-1, keepdims=True))
            if dil == DILATIONS[-1]:
                num_ref[rows, :] = mx
            elif dil > 1:
                num_ref[rows, :] = jnp.maximum(num_ref[rows, :], mx)
            else:
                o_ref[0, pp, rows, :] = jnp.maximum(num_ref[rows, :], mx)
            return
        vband = band(main_ref, prev_ref, next_ref, blk, n_blk, _qkv_lanes(pp, dil, r, 2))
        vext = jnp.concatenate([vband, ones], axis=1)
        outs = []
        for half in range(2):
            s = biased(half)
            if mode == "attn_shift":
                s = s - c_ref[0, pp, rows, :][:, half * HEAD_DIM:half * HEAD_DIM + 1]
            outs.append(_dot(jnp.exp2(s).astype(BF16), vext))
        num = jnp.where(first_half, outs[0][:, :LANES], outs[1][:, :LANES])
        den = jnp.where(first_half, outs[0][:, LANES:], outs[1][:, LANES:])
        if dil == DILATIONS[-1]:
            num_ref[rows, :] = num
            den_ref[rows, :] = den
        elif dil > 1:
            num_ref[rows, :] = num_ref[rows, :] + num
            den_ref[rows, :] = den_ref[rows, :] + den
        else:
            y = (num_ref[rows, :] + num) / (den_ref[rows, :] + den)
            o_ref[0, rows, out_ls] = (y * _silu(g_ref[0, rows, out_ls].astype(F32))).astype(BF16)

    pending = [stage_scores(u) for u in units[:LOOKAHEAD]]
    for t, u in enumerate(units):
        if t + LOOKAHEAD < len(units):
            pending.append(stage_scores(units[t + LOOKAHEAD]))
        stage_values(u, pending[t])
        pending[t] = None


def _band_call(layouts, bias, gate, shift=None, mode="attn"):
    B, S, _ = layouts[0].shape
    n_sb = S // SUPER_BLOCK
    pp = PAIRS_PER_STEP
    in_specs, args = [], []
    for qkv, dil in zip(layouts, DILATIONS):
        rows = SUPER_BLOCK // dil
        width = pp * dil * 3 * LANES
        n_halo = (S // dil) // BAND_HALF
        per_sb = rows // BAND_HALF
        main = pl.BlockSpec((1, rows, width), lambda b, pr, n: (b, n, pr))
        prev = pl.BlockSpec((1, BAND_HALF, width),
                            lambda b, pr, n, per_sb=per_sb: (b, jnp.maximum(n * per_sb - 1, 0), pr))
        nxt = pl.BlockSpec((1, BAND_HALF, width),
                           lambda b, pr, n, per_sb=per_sb, n_halo=n_halo: (b, jnp.minimum((n + 1) * per_sb, n_halo - 1), pr))
        in_specs += [main, prev, nxt]
        args += [qkv, qkv, qkv]
    in_specs.append(pl.BlockSpec((len(DILATIONS), 2 * pp, BAND_BLOCK, BAND_SPAN), lambda b, pr, n: (0, pr, 0, 0)))
    args.append(bias)
    tok_blk = pl.BlockSpec((1, SUPER_BLOCK, pp * LANES), lambda b, pr, n: (b, n, pr))
    shift_blk = pl.BlockSpec((1, pp, SUPER_BLOCK, LANES), lambda b, pr, n: (b, pr, n, 0))
    if mode != "max":
        in_specs.append(tok_blk)
        args.append(gate)
    if mode == "attn_shift":
        in_specs.append(shift_blk)
        args.append(shift)
    if mode == "max":
        out_specs, out_shape = shift_blk, jax.ShapeDtypeStruct((B, N_PAIRS, S, LANES), F32)
    else:
        out_specs, out_shape = tok_blk, jax.ShapeDtypeStruct((B, S, D_A), BF16)
    return pl.pallas_call(
        functools.partial(_band_kernel, mode=mode),
        grid=(B, N_PAIRS // pp, n_sb),
        in_specs=in_specs,
        out_specs=out_specs,
        out_shape=out_shape,
        scratch_shapes=[pltpu.VMEM((pp, SUPER_BLOCK, LANES), F32)] * 2,
        compiler_params=pltpu.CompilerParams(
            dimension_semantics=("arbitrary", "arbitrary", "arbitrary"), vmem_limit_bytes=VMEM_LIMIT),
        name="band_" + mode,
    )(*args)


def _out_kernel(x_ref, p_ref, ya_ref, ybt_ref, wo32_ref, wple32_ref, wgate32_ref, o_ref, wo_ref, wple_ref, wgate_ref):
    @pl.when((pl.program_id(0) == 0) & (pl.program_id(1) == 0))
    def _():
        wo_ref[...] = wo32_ref[...].astype(BF16)
        wple_ref[...] = wple32_ref[...].astype(BF16)
        wgate_ref[...] = wgate32_ref[...].astype(BF16)

    tm = x_ref.shape[1]
    yb = ybt_ref[0].reshape(D_B, tm).astype(F32).T.astype(BF16)
    x1 = x_ref[0] + _dot(ya_ref[0], wo_ref[:D_A, :]) + _dot(yb, wo_ref[D_A:, :])
    gate = jax.nn.sigmoid(_dot(x1.astype(BF16), wgate_ref[...]))
    ple = _dot(p_ref[0].astype(BF16), wple_ref[...])
    o_ref[0] = x1 + ple * gate


def _out_call(x, p, ya, ybt, wo, wple, wgate):
    B, S, D = x.shape
    tm = ROW_TILE
    full = lambda shape: pl.BlockSpec(shape, lambda b, i: (0,) * len(shape), pipeline_mode=pl.Buffered(1))
    tok = lambda c: pl.BlockSpec((1, tm, c), lambda b, i: (b, i, 0))
    return pl.pallas_call(
        _out_kernel,
        grid=(B, S // tm),
        in_specs=[tok(D), tok(p.shape[-1]), tok(D_A),
                  pl.BlockSpec((1, N_HEADS_B, HEAD_DIM, tm), lambda b, i: (b, 0, 0, i)),
                  full(wo.shape), full(wple.shape), full(wgate.shape)],
        out_specs=tok(D),
        out_shape=jax.ShapeDtypeStruct((B, S, D), F32),
        scratch_shapes=[pltpu.VMEM(w.shape, BF16) for w in (wo, wple, wgate)],
        compiler_params=pltpu.CompilerParams(
            dimension_semantics=("arbitrary", "arbitrary"), vmem_limit_bytes=VMEM_LIMIT),
        name="out_proj",
    )(x, p, ya, ybt, wo, wple, wgate)


def _rope_tables(S):
    n = HEAD_DIM // 4
    inv = ROPE_THETA ** (-jnp.arange(n, dtype=F32) / n)
    rows = S // GRID_W
    ang_row = inv[:, None] * jnp.arange(rows, dtype=jnp.int32).astype(F32)[None, :]
    ang_col = inv[:, None] * jnp.arange(GRID_W, dtype=jnp.int32).astype(F32)[None, :]
    expand = lambda fn: jnp.concatenate(
        [jnp.repeat(fn(ang_row), GRID_W, axis=1), jnp.tile(fn(ang_col), (1, rows))], axis=0)
    return expand(jnp.cos), expand(jnp.sin)


def _pairs_apart(t):
    lead, tail = t.shape[:-2], t.shape[-1]
    heads = t.shape[-2] // HEAD_DIM
    t = t.reshape(*lead, heads, HEAD_DIM // 2, 2, tail)
    return jnp.swapaxes(t, -3, -2).reshape(*lead, heads * HEAD_DIM, tail)


def kernel(x, p, ln_g, w_in, qn_a, kn_a, qn_b, kn_b, w_out, w_ple, w_pgate, rel_bias):
    B, S, D = x.shape
    depth = w_in.shape[0]
    cos_t, sin_t = _rope_tables(S)
    bias = _bias_call(rel_bias)
    for i in range(depth):
        wfeat = w_in[i].T.astype(BF16)
        wrope = _pairs_apart(wfeat[4 * D_A:4 * D_A + D_B + D_KV_B])
        wtok = w_in[i][:, 2 * D_A:4 * D_A].astype(BF16)
        gains_b = _pairs_apart(jnp.stack([qn_b[i], kn_b[i]], axis=-1))
        gains = jnp.concatenate([jnp.stack([qn_a[i], kn_a[i]]), gains_b.T]).astype(F32)
        gains = jnp.broadcast_to(gains[:, :, None], (4, HEAD_DIM, ROW_TILE // PROJ_SUBTILES))

        outs = _proj_call(
            x, ln_g[i][None, :], wtok, wfeat, wrope, gains, cos_t, sin_t)
        ga, gbt, qkv1, qbt, kb, vbt = outs

        gmax = jnp.max(jnp.abs(jnp.stack([qn_a[i], kn_a[i], qn_b[i], kn_b[i]]).astype(F32)), axis=1)
        root = HEAD_DIM ** 0.5
        bound_a = root * gmax[0] * gmax[1] + jnp.max(jnp.abs(rel_bias.astype(F32)))
        bound_b = root * gmax[2] * gmax[3]
        safe = jnp.maximum(bound_a, bound_b) <= SAFE_EXPONENT

        def unshifted(ops):
            qbt, kb, vbt, gbt, qkv1, bias, ga = ops
            ybt, (qkv4, qkv16) = _gqa_call(qbt, kb, vbt, gbt, qkv1)
            return ybt, _band_call((qkv1, qkv4, qkv16), bias, ga)

        def shifted(ops):
            qbt, kb, vbt, gbt, qkv1, bias, ga = ops
            shift_b = _shift_rows(_gqa_call(qbt, kb, vbt, gbt, qkv1, mode="max"))
            ybt, (qkv4, qkv16) = _gqa_call(qbt, kb, vbt, gbt, qkv1, shift_b, mode="attn_shift")
            layouts = (qkv1, qkv4, qkv16)
            shift_a = _band_call(layouts, bias, ga, mode="max")
            return ybt, _band_call(layouts, bias, ga, shift_a, mode="attn_shift")

        ybt, ya = lax.cond(safe, unshifted, shifted, (qbt, kb, vbt, gbt, qkv1, bias, ga))
        x = _out_call(x, p[i], ya, ybt, w_out[i], w_ple[i], w_pgate[i])
    return x
```
